```python
import math
import jax
import jax.numpy as jnp
from jax import lax
import numpy as np

D_MODEL = 1024
BATCH = 4
SEQ = 4096
DEPTH = 2
DEC_BATCH = 128
DEC_SEQ = 1
PAST_LEN = 2048
PAGE_SIZE = 128

D_FF = 2816
NORM_EPS = 1e-6
FOX_HEADS = 8
FOX_DH = 64
DIFF_HEADS = 4
DIFF_DH = 64
QBLOCK = 128
MLSTM_HEADS = 4
MLSTM_DK = 64
MLSTM_DV = 128
MLSTM_CHUNK = 64
CONV_CH = 512
CONV_W = 31
N_ATTN_LAYERS = (DEPTH + 1) // 2
N_REC_LAYERS = DEPTH // 2
FOX_W = FOX_HEADS * FOX_DH
DIFF_W = DIFF_HEADS * 2 * DIFF_DH
ATTN_IN = 3 * FOX_W + FOX_HEADS + 3 * DIFF_W
MLSTM_QK = MLSTM_HEADS * MLSTM_DK
MLSTM_V = MLSTM_HEADS * MLSTM_DV
REC_IN = 2 * MLSTM_QK + 2 * MLSTM_V + 2 * MLSTM_HEADS + 2 * CONV_CH
MIX_W = FOX_W + DIFF_W
NEG = -1e30
F32 = jnp.float32

kernel_name = 'hybrid_fox_diff_mlstm_conformer_step'


def _split_cols(u, sizes):
    idx, acc = [], 0
    for s in sizes[:-1]:
        acc += s
        idx.append(acc)
    return jnp.split(u, idx, axis=-1)


def rmsnorm(x, g):
    xf = x.astype(F32)
    y = xf * lax.rsqrt(jnp.mean(xf * xf, axis=-1, keepdims=True) + NORM_EPS)
    return (y * g.astype(F32)).astype(x.dtype)


def layernorm(x, g, b):
    xf = x.astype(F32)
    xc = xf - jnp.mean(xf, axis=-1, keepdims=True)
    var = jnp.mean(xc * xc, axis=-1, keepdims=True)
    return (xc * lax.rsqrt(var + NORM_EPS) * g.astype(F32) + b.astype(F32)).astype(x.dtype)


def swiglu_half(x, g, w_in, w_out):
    a, b = jnp.split(rmsnorm(x, g) @ w_in, 2, axis=-1)
    return x + 0.5 * ((jax.nn.silu(a) * b) @ w_out)


def causal_mask(q_pos, k_pos):
    return k_pos[None, :] <= q_pos[:, None]


def fox_attend(q, k, v, fq, fk, q_pos, k_pos):
    s = jnp.einsum('bqhd,bkhd->bhqk', q.astype(F32), k.astype(F32)) * (FOX_DH ** -0.5)
    s = s + jnp.swapaxes(fq, 1, 2)[:, :, :, None] - jnp.swapaxes(fk, 1, 2)[:, :, None, :]
    s = jnp.where(causal_mask(q_pos, k_pos), s, NEG)
    p = jax.nn.softmax(s, axis=-1)
    return jnp.einsum('bhqk,bkhd->bqhd', p, v.astype(F32))


def diff_attend(q, k, v, q_pos, k_pos, lam):
    k2 = k.reshape(k.shape[0], k.shape[1], DIFF_HEADS, 2, DIFF_DH)
    s = jnp.einsum('bqhcd,bkhcd->bhcqk', q.astype(F32), k2.astype(F32)) * (DIFF_DH ** -0.5)
    s = jnp.where(causal_mask(q_pos, k_pos), s, NEG)
    p = jax.nn.softmax(s, axis=-1)
    a = p[:, :, 0] - lam * p[:, :, 1]
    return jnp.einsum('bhqk,bkhe->bqhe', a, v.astype(F32))


def sweep_query_blocks(fn, seq_len, *q_side):
    n_blocks = seq_len // QBLOCK

    def one_block(i):
        start = i * QBLOCK
        blocks = [lax.dynamic_slice_in_dim(a, start, QBLOCK, axis=1) for a in q_side]
        return fn(start + jnp.arange(QBLOCK), *blocks)

    out = jnp.swapaxes(lax.map(one_block, jnp.arange(n_blocks)), 0, 1)
    return out.reshape((out.shape[0], seq_len) + out.shape[3:])


def diff_lambda_value(lam_p, lam_init):
    lp = lam_p.astype(F32)
    return jnp.exp(jnp.sum(lp[0] * lp[1])) - jnp.exp(jnp.sum(lp[2] * lp[3])) + lam_init


def attn_project(h, w_in, b_f):
    B, T, _ = h.shape
    q_f, k_f, v_f, f_logit, q_d, k_d, v_d = _split_cols(
        h @ w_in, [FOX_W, FOX_W, FOX_W, FOX_HEADS, DIFF_W, DIFF_W, DIFF_W])
    fh = (B, T, FOX_HEADS, FOX_DH)
    dh = (B, T, DIFF_HEADS, 2 * DIFF_DH)
    logf = jax.nn.log_sigmoid((f_logit + b_f).astype(F32))
    return (q_f.reshape(fh), k_f.reshape(fh), v_f.reshape(fh), logf,
            q_d.reshape(B, T, DIFF_HEADS, 2, DIFF_DH), k_d.reshape(dh), v_d.reshape(dh))


def attn_merge(o_f, o_d, lam_init, subln_g, w_out, dtype):
    B, T = o_f.shape[:2]
    o_d = rmsnorm(o_d, subln_g) * (1.0 - lam_init)
    o = jnp.concatenate([o_f.reshape(B, T, FOX_W), o_d.reshape(B, T, DIFF_W)], axis=-1)
    return o.astype(dtype) @ w_out


def attn_mixer_prompt(h, w_in, b_f, lam_p, subln_g, w_out, lam_init):
    T = h.shape[1]
    q_f, k_f, v_f, logf, q_d, k_d, v_d = attn_project(h, w_in, b_f)
    pos = jnp.arange(T)
    cum_f = jnp.cumsum(logf, axis=1)
    o_f = sweep_query_blocks(
        lambda qp, qb, fb: fox_attend(qb, k_f, v_f, fb, cum_f, qp, pos), T, q_f, cum_f)
    lam = diff_lambda_value(lam_p, lam_init)
    o_d = sweep_query_blocks(
        lambda qp, qb: diff_attend(qb, k_d, v_d, qp, pos, lam), T, q_d)
    y = attn_merge(o_f, o_d, lam_init, subln_g, w_out, h.dtype)
    return y, jnp.stack([k_f, v_f], axis=2), logf, jnp.stack([k_d, v_d], axis=2)


def gather_pages(cache, j, page_table):
    rows = cache[j, page_table]
    return rows.reshape((page_table.shape[0], page_table.shape[1] * cache.shape[2]) + cache.shape[3:])


def attn_mixer_sample(h, kv_f_cache, logf_cache, kv_d_cache, j, page_table,
                      w_in, b_f, lam_p, subln_g, w_out, lam_init):
    T = h.shape[1]
    past = page_table.shape[1] * PAGE_SIZE
    q_f, k_f, v_f, logf, q_d, k_d, v_d = attn_project(h, w_in, b_f)
    new_kv_f = jnp.stack([k_f, v_f], axis=2)
    new_kv_d = jnp.stack([k_d, v_d], axis=2)
    kv_f = jnp.concatenate([gather_pages(kv_f_cache, j, page_table), new_kv_f], axis=1)
    kv_d = jnp.concatenate([gather_pages(kv_d_cache, j, page_table), new_kv_d], axis=1)
    logf_all = jnp.concatenate([gather_pages(logf_cache, j, page_table).astype(F32), logf], axis=1)
    cum_f = jnp.cumsum(logf_all, axis=1)
    q_pos = past + jnp.arange(T)
    k_pos = jnp.arange(past + T)
    o_f = fox_attend(q_f, kv_f[:, :, 0], kv_f[:, :, 1], cum_f[:, past:], cum_f, q_pos, k_pos)
    lam = diff_lambda_value(lam_p, lam_init)
    o_d = diff_attend(q_d, kv_d[:, :, 0], kv_d[:, :, 1], q_pos, k_pos, lam)
    y = attn_merge(o_f, o_d, lam_init, subln_g, w_out, h.dtype)
    return y, new_kv_f, logf, new_kv_d


def mlstm_chunkwise(q, k, v, ig, logf, C0, n0, m0):
    B, T = q.shape[:2]
    L = min(MLSTM_CHUNK, T)
    n_chunks = -(-T // L)
    pad = n_chunks * L - T

    def to_chunks(a, fill):
        a = a.astype(F32)
        if pad:
            a = jnp.pad(a, [(0, 0), (0, pad)] + [(0, 0)] * (a.ndim - 2), constant_values=fill)
        return jnp.swapaxes(a.reshape((B, n_chunks, L) + a.shape[2:]), 0, 1)

    xs = (to_chunks(q, 0.0), to_chunks(k, 0.0), to_chunks(v, 0.0),
          to_chunks(ig, NEG), to_chunks(logf, 0.0))
    tri = jnp.tril(jnp.ones((L, L), dtype=bool))

    def chunk_step(carry, inp):
        C, n, m = carry
        qc, kc, vc, igc, lfc = inp
        cum = jnp.swapaxes(jnp.cumsum(lfc, axis=1), 1, 2)
        it = jnp.swapaxes(igc, 1, 2)
        d = jnp.where(tri, cum[..., :, None] - cum[..., None, :] + it[..., None, :], NEG)
        m_inter = cum + m[..., None]
        m_t = jnp.maximum(jnp.max(d, axis=-1), m_inter)
        s = jnp.einsum('blhd,bshd->bhls', qc, kc) * jnp.exp(d - m_t[..., None])
        w_inter = jnp.exp(m_inter - m_t)
        num = jnp.einsum('bhls,bshv->bhlv', s, vc) + w_inter[..., None] * jnp.einsum('bhvd,blhd->bhlv', C, qc)
        den = jnp.sum(s, axis=-1) + w_inter * jnp.einsum('bhd,blhd->bhl', n, qc)
        h = num / jnp.maximum(jnp.abs(den), jnp.exp(-m_t))[..., None]
        m_new = m_t[..., -1]
        decay = jnp.exp(cum[..., -1] + m - m_new)
        w_key = jnp.exp(cum[..., -1:] - cum + it - m_new[..., None])
        C_new = decay[..., None, None] * C + jnp.einsum('bhs,bshv,bshd->bhvd', w_key, vc, kc)
        n_new = decay[..., None] * n + jnp.einsum('bhs,bshd->bhd', w_key, kc)
        return (C_new, n_new, m_new), jnp.swapaxes(h, 1, 2)

    (C, n, m), hs = lax.scan(chunk_step, (C0.astype(F32), n0.astype(F32), m0.astype(F32)), xs)
    hs = jnp.swapaxes(hs, 0, 1).reshape(B, n_chunks * L, MLSTM_HEADS, MLSTM_DV)[:, :T]
    return hs, C, n, m


def causal_depthwise(u, prev, w, b):
    full = jnp.concatenate([prev.astype(u.dtype), u], axis=1)
    y = lax.conv_general_dilated(full, w[:, None, :].astype(u.dtype), window_strides=(1,), padding='VALID',
                                 dimension_numbers=('NWC', 'WIO', 'NWC'), feature_group_count=u.shape[-1])
    return y + b, full[:, -(CONV_W - 1):]


def rec_mixer(h, C0, n0, m0, conv_prev, w_in, b_i, b_f, norm_g, conv_w, conv_b, ln_g, ln_b, w_out):
    B, T, _ = h.shape
    q, k, v, i_logit, f_logit, o, ua, ub = _split_cols(
        h @ w_in, [MLSTM_QK, MLSTM_QK, MLSTM_V, MLSTM_HEADS, MLSTM_HEADS, MLSTM_V, CONV_CH, CONV_CH])
    q = q.reshape(B, T, MLSTM_HEADS, MLSTM_DK) * (MLSTM_DK ** -0.5)
    k = k.reshape(B, T, MLSTM_HEADS, MLSTM_DK)
    v = v.reshape(B, T, MLSTM_HEADS, MLSTM_DV)
    ig = (i_logit + b_i).astype(F32)
    logf = jax.nn.log_sigmoid((f_logit + b_f).astype(F32))
    hm, C, n, m = mlstm_chunkwise(q, k, v, ig, logf, C0, n0, m0)
    hm = rmsnorm(hm, norm_g.reshape(MLSTM_HEADS, MLSTM_DV)).reshape(B, T, MLSTM_V)
    hm = hm * jax.nn.sigmoid(o.astype(F32))
    u = ua * jax.nn.sigmoid(ub)
    c, conv_state = causal_depthwise(u, conv_prev, conv_w, conv_b)
    c = jax.nn.silu(layernorm(c, ln_g, ln_b))
    y = jnp.concatenate([hm.astype(h.dtype), c.astype(h.dtype)], axis=-1) @ w_out
    return y, C, n, m, conv_state


def setup_inputs(seed: int = 0) -> dict:
    key = jax.random.key(seed)
    ks = iter(jax.random.split(key, 40))

    def normal(shape, scale):
        return jax.random.normal(next(ks), shape, F32) * scale

    n_pages = PAST_LEN // PAGE_SIZE
    n_used = DEC_BATCH * n_pages
    n_pool = n_used + max(1, n_used // 4)
    NA, NR = N_ATTN_LAYERS, N_REC_LAYERS
    x_prompt = normal((BATCH, SEQ, D_MODEL), 1.0)
    x_sample = normal((DEC_BATCH, DEC_SEQ, D_MODEL), 1.0)
    cache_fox_kv = normal((NA, n_pool, PAGE_SIZE, 2, FOX_HEADS, FOX_DH), 1.0)
    cache_fox_logf = jax.nn.log_sigmoid(normal((NA, n_pool, PAGE_SIZE, FOX_HEADS), 1.0) + 3.0)
    cache_diff_kv = normal((NA, n_pool, PAGE_SIZE, 2, DIFF_HEADS, 2 * DIFF_DH), 1.0)
    state_mlstm_C = normal((NR, DEC_BATCH, MLSTM_HEADS, MLSTM_DV, MLSTM_DK), 0.5)
    state_mlstm_n = normal((NR, DEC_BATCH, MLSTM_HEADS, MLSTM_DK), 0.5)
    state_mlstm_m = normal((NR, DEC_BATCH, MLSTM_HEADS), 1.0)
    state_conv = normal((NR, DEC_BATCH, CONV_W - 1, CONV_CH), 1.0)
    page_table = jax.random.permutation(next(ks), n_pool)[:n_used].reshape(DEC_BATCH, n_pages).astype(jnp.int32)
    norm_g = 1.0 + normal((DEPTH, 3, D_MODEL), 0.01)
    final_g = 1.0 + normal((D_MODEL,), 0.01)
    ffn_w_in = normal((DEPTH, 2, D_MODEL, 2 * D_FF), D_MODEL ** -0.5)
    ffn_w_out = normal((DEPTH, 2, D_FF, D_MODEL), D_FF ** -0.5)
    attn_w_in = normal((NA, D_MODEL, ATTN_IN), D_MODEL ** -0.5)
    attn_b_f = 1.0 + normal((NA, FOX_HEADS), 0.1)
    diff_lam = normal((NA, 4, DIFF_DH), 0.1)
    diff_subln_g = 1.0 + normal((NA, 2 * DIFF_DH), 0.01)
    attn_w_out = normal((NA, MIX_W, D_MODEL), MIX_W ** -0.5)
    rec_w_in = normal((NR, D_MODEL, REC_IN), D_MODEL ** -0.5)
    rec_b_i = normal((NR, MLSTM_HEADS), 0.1)
    rec_b_f = jnp.linspace(3.0, 6.0, MLSTM_HEADS, dtype=F32)[None, :] + normal((NR, MLSTM_HEADS), 0.1)
    mlstm_norm_g = 1.0 + normal((NR, MLSTM_V), 0.01)
    conv_w = normal((NR, CONV_W, CONV_CH), CONV_W ** -0.5)
    conv_b = normal((NR, CONV_CH), 0.01)
    conv_ln_g = 1.0 + normal((NR, CONV_CH), 0.01)
    conv_ln_b = normal((NR, CONV_CH), 0.01)
    rec_w_out = normal((NR, MIX_W, D_MODEL), MIX_W ** -0.5)
    return {'x_prompt': x_prompt, 'x_sample': x_sample,
            'cache_fox_kv': cache_fox_kv, 'cache_fox_logf': cache_fox_logf, 'cache_diff_kv': cache_diff_kv,
            'state_mlstm_C': state_mlstm_C, 'state_mlstm_n': state_mlstm_n, 'state_mlstm_m': state_mlstm_m,
            'state_conv': state_conv, 'page_table': page_table,
            'norm_g': norm_g, 'final_g': final_g, 'ffn_w_in': ffn_w_in, 'ffn_w_out': ffn_w_out,
            'attn_w_in': attn_w_in, 'attn_b_f': attn_b_f, 'diff_lam': diff_lam, 'diff_subln_g': diff_subln_g,
            'attn_w_out': attn_w_out, 'rec_w_in': rec_w_in, 'rec_b_i': rec_b_i, 'rec_b_f': rec_b_f,
            'mlstm_norm_g': mlstm_norm_g, 'conv_w': conv_w, 'conv_b': conv_b, 'conv_ln_g': conv_ln_g,
            'conv_ln_b': conv_ln_b, 'rec_w_out': rec_w_out}


def reference(x_prompt, x_sample, cache_fox_kv, cache_fox_logf, cache_diff_kv,
              state_mlstm_C, state_mlstm_n, state_mlstm_m, state_conv, page_table,
              norm_g, final_g, ffn_w_in, ffn_w_out, attn_w_in, attn_b_f, diff_lam, diff_subln_g,
              attn_w_out, rec_w_in, rec_b_i, rec_b_f, mlstm_norm_g, conv_w, conv_b, conv_ln_g,
              conv_ln_b, rec_w_out):
    xp, xs = x_prompt, x_sample
    Bp = x_prompt.shape[0]
    fkv_p, fkv_s, flf_p, flf_s, dkv_p, dkv_s = [], [], [], [], [], []
    mC_p, mC_s, mn_p, mn_s, mm_p, mm_s, cv_p, cv_s = [], [], [], [], [], [], [], []
    for l in range(DEPTH):
        xp = swiglu_half(xp, norm_g[l, 0], ffn_w_in[l, 0], ffn_w_out[l, 0])
        xs = swiglu_half(xs, norm_g[l, 0], ffn_w_in[l, 0], ffn_w_out[l, 0])
        hp = rmsnorm(xp, norm_g[l, 1])
        hs = rmsnorm(xs, norm_g[l, 1])
        j = l // 2
        if l % 2 == 0:
            lam_init = 0.8 - 0.6 * math.exp(-0.3 * l)
            yp, a_p, b_p, c_p = attn_mixer_prompt(hp, attn_w_in[j], attn_b_f[j], diff_lam[j],
                                                  diff_subln_g[j], attn_w_out[j], lam_init)
            ys, a_s, b_s, c_s = attn_mixer_sample(hs, cache_fox_kv, cache_fox_logf, cache_diff_kv, j, page_table,
                                                  attn_w_in[j], attn_b_f[j], diff_lam[j],
                                                  diff_subln_g[j], attn_w_out[j], lam_init)
            fkv_p.append(a_p); flf_p.append(b_p); dkv_p.append(c_p)
            fkv_s.append(a_s); flf_s.append(b_s); dkv_s.append(c_s)
        else:
            rp = (rec_w_in[j], rec_b_i[j], rec_b_f[j], mlstm_norm_g[j], conv_w[j], conv_b[j],
                  conv_ln_g[j], conv_ln_b[j], rec_w_out[j])
            C0 = jnp.zeros((Bp, MLSTM_HEADS, MLSTM_DV, MLSTM_DK), F32)
            n0 = jnp.zeros((Bp, MLSTM_HEADS, MLSTM_DK), F32)
            m0 = jnp.full((Bp, MLSTM_HEADS), NEG, F32)
            cprev = jnp.zeros((Bp, CONV_W - 1, CONV_CH), hp.dtype)
            yp, C_p, n_p, m_p, s_p = rec_mixer(hp, C0, n0, m0, cprev, *rp)
            ys, C_s, n_s, m_s, s_s = rec_mixer(hs, state_mlstm_C[j], state_mlstm_n[j], state_mlstm_m[j],
                                               state_conv[j], *rp)
            mC_p.append(C_p); mn_p.append(n_p); mm_p.append(m_p); cv_p.append(s_p)
            mC_s.append(C_s); mn_s.append(n_s); mm_s.append(m_s); cv_s.append(s_s)
        xp = xp + yp
        xs = xs + ys
        xp = swiglu_half(xp, norm_g[l, 2], ffn_w_in[l, 1], ffn_w_out[l, 1])
        xs = swiglu_half(xs, norm_g[l, 2], ffn_w_in[l, 1], ffn_w_out[l, 1])
    y_prompt = rmsnorm(xp, final_g)
    y_sample = rmsnorm(xs, final_g)
    return (y_prompt, y_sample,
            jnp.stack(fkv_p), jnp.stack(fkv_s), jnp.stack(flf_p), jnp.stack(flf_s),
            jnp.stack(dkv_p), jnp.stack(dkv_s),
            jnp.stack(mC_p), jnp.stack(mC_s), jnp.stack(mn_p), jnp.stack(mn_s),
            jnp.stack(mm_p), jnp.stack(mm_s), jnp.stack(cv_p), jnp.stack(cv_s))
```

```python
import functools
import math

import jax
import jax.numpy as jnp
from jax import lax
from jax.experimental import pallas as pl
from jax.experimental.pallas import tpu as pltpu

F32 = jnp.float32
BF16 = jnp.bfloat16
NORM_EPS = 1e-6
NEG = -1e30

FOX_HEADS = 8
FOX_DH = 64
DIFF_HEADS = 4
DIFF_DH = 64
MLSTM_HEADS = 4
MLSTM_DK = 64
MLSTM_DV = 128
CONV_CH = 512
CONV_W = 31
FOX_W = FOX_HEADS * FOX_DH
DIFF_W = DIFF_HEADS * 2 * DIFF_DH
MLSTM_QK = MLSTM_HEADS * MLSTM_DK
MLSTM_V = MLSTM_HEADS * MLSTM_DV

LANES = 128
SUBLANES = 8
VMEM_LIMIT_BYTES = 56 * 1024 * 1024

_NT = (((1,), (1,)), ((), ()))


def _params(*sem):
    return pltpu.CompilerParams(dimension_semantics=sem, vmem_limit_bytes=VMEM_LIMIT_BYTES)


def _rms(x, g):
    return x * lax.rsqrt(jnp.mean(x * x, axis=-1, keepdims=True) + NORM_EPS) * g


def _log_sigmoid(z):
    return jnp.minimum(z, 0.0) - jnp.log(1.0 + jnp.exp(-jnp.abs(z)))


def _split3(x):
    hi = x.astype(BF16)
    r = x - hi.astype(F32)
    mid = r.astype(BF16)
    lo = (r - mid.astype(F32)).astype(BF16)
    return hi, mid, lo


def _dot(a, b):
    return jnp.dot(a, b, preferred_element_type=F32)


def _dot_nt(a, b):
    return lax.dot_general(a, b, _NT, preferred_element_type=F32)


def _dot3_left(x, t):
    hi, mid, lo = _split3(x)
    return _dot(hi, t) + _dot(mid, t) + _dot(lo, t)


def _dot3_right(t, x):
    hi, mid, lo = _split3(x)
    return _dot(t, hi) + _dot(t, mid) + _dot(t, lo)


def _tri(n, upper):
    r = lax.broadcasted_iota(jnp.int32, (n, n), 0)
    c = lax.broadcasted_iota(jnp.int32, (n, n), 1)
    keep = (r <= c) if upper else (r >= c)
    return jnp.where(keep, 1.0, 0.0).astype(BF16)


def _ffn_kernel(x_ref, g_ref, wa_ref, wb_ref, wo_ref, fg_ref, o_ref, h_scr, acc_scr, *, final_norm):
    j = pl.program_id(1)

    @pl.when(j == 0)
    def _():
        h_scr[...] = _rms(x_ref[...], g_ref[...]).astype(BF16)
        acc_scr[...] = jnp.zeros_like(acc_scr)

    h = h_scr[...]
    a = _dot(h, wa_ref[...])
    b = _dot(h, wb_ref[...])
    gated = (a * jax.nn.sigmoid(a) * b).astype(BF16)
    acc_scr[...] += _dot(gated, wo_ref[...])

    @pl.when(j == pl.num_programs(1) - 1)
    def _():
        y = x_ref[...] + 0.5 * acc_scr[...]
        if final_norm:
            y = _rms(y, fg_ref[...])
        o_ref[...] = y


def _ffn(x, g, w_in, w_out, final_g=None):
    m, d = x.shape
    f = w_out.shape[0]
    tm = min(1024, m)
    tf = 256
    nf = f // tf
    fg = jnp.ones((1, d), F32) if final_g is None else final_g.reshape(1, d)
    return pl.pallas_call(
        functools.partial(_ffn_kernel, final_norm=final_g is not None),
        out_shape=jax.ShapeDtypeStruct((m, d), F32),
        grid=(m // tm, nf),
        in_specs=[
            pl.BlockSpec((tm, d), lambda i, j: (i, 0)),
            pl.BlockSpec((1, d), lambda i, j: (0, 0)),
            pl.BlockSpec((d, tf), lambda i, j: (0, j)),
            pl.BlockSpec((d, tf), lambda i, j: (0, j + nf)),
            pl.BlockSpec((tf, d), lambda i, j: (j, 0)),
            pl.BlockSpec((1, d), lambda i, j: (0, 0)),
        ],
        out_specs=pl.BlockSpec((tm, d), lambda i, j: (i, 0)),
        scratch_shapes=[pltpu.VMEM((tm, d), BF16), pltpu.VMEM((tm, d), F32)],
        compiler_params=_params("parallel", "arbitrary"),
        name="ffn",
    )(x, g.reshape(1, d), w_in, w_in, w_out, fg)


def _merge_kernel(x_ref, oa_ref, ob_ref, w_ref, o_ref):
    wa = oa_ref.shape[-1]
    y = _dot(oa_ref[...].astype(BF16), w_ref[:wa, :])
    y += _dot(ob_ref[...].astype(BF16), w_ref[wa:, :])
    o_ref[...] = x_ref[...] + y


def _merge(x, oa, ob, w_out):
    m, d = x.shape
    tm = min(1024, m)
    wa, wb = oa.shape[1], ob.shape[1]
    return pl.pallas_call(
        _merge_kernel,
        out_shape=jax.ShapeDtypeStruct((m, d), F32),
        grid=(m // tm,),
        in_specs=[
            pl.BlockSpec((tm, d), lambda i: (i, 0)),
            pl.BlockSpec((tm, wa), lambda i: (i, 0)),
            pl.BlockSpec((tm, wb), lambda i: (i, 0)),
            pl.BlockSpec((wa + wb, d), lambda i: (0, 0)),
        ],
        out_specs=pl.BlockSpec((tm, d), lambda i: (i, 0)),
        compiler_params=_params("parallel"),
        name="merge",
    )(x, oa, ob, w_out)


def _attn_proj_kernel(x_ref, g_ref, w_ref, wft_ref, bf_ref, qf_ref, kvf_ref, qd_ref, kvd_ref,
                      lf_ref, cum_ref, carry_scr):
    t = pl.program_id(1)
    h = _rms(x_ref[0], g_ref[...]).astype(BF16)
    qf_ref[0] = _dot(h, w_ref[:, 0:FOX_W])
    kvf_ref[0] = _dot(h, w_ref[:, FOX_W:3 * FOX_W])
    qd_ref[0] = _dot(h, w_ref[:, 3 * FOX_W:3 * FOX_W + DIFF_W])
    kvd_ref[0] = _dot(h, w_ref[:, 3 * FOX_W + DIFF_W:])
    logf = _log_sigmoid(_dot_nt(wft_ref[...], h) + bf_ref[...])
    lf_ref[0] = logf

    @pl.when(t == 0)
    def _():
        carry_scr[...] = jnp.zeros_like(carry_scr)

    tm = logf.shape[1]
    cum = _dot3_left(logf, _tri(tm, upper=True)) + carry_scr[:, 0:1]
    cum_ref[0] = cum
    carry_scr[...] = jnp.broadcast_to(cum[:, tm - 1:tm], carry_scr.shape)


def _attn_proj(x, g, w_main, wft, bf):
    b, t, d = x.shape
    tm = min(512, t)
    n_main = w_main.shape[1]
    row = lambda i, j: (i, j, 0)
    col = lambda i, j: (i, 0, j)
    const = lambda i, j: (0, 0)
    return pl.pallas_call(
        _attn_proj_kernel,
        out_shape=[
            jax.ShapeDtypeStruct((b, t, FOX_W), F32),
            jax.ShapeDtypeStruct((b, t, 2 * FOX_W), F32),
            jax.ShapeDtypeStruct((b, t, DIFF_W), F32),
            jax.ShapeDtypeStruct((b, t, 2 * DIFF_W), F32),
            jax.ShapeDtypeStruct((b, FOX_HEADS, t), F32),
            jax.ShapeDtypeStruct((b, FOX_HEADS, t), F32),
        ],
        grid=(b, t // tm),
        in_specs=[
            pl.BlockSpec((1, tm, d), row),
            pl.BlockSpec((1, d), const),
            pl.BlockSpec((d, n_main), const),
            pl.BlockSpec((FOX_HEADS, d), const),
            pl.BlockSpec((FOX_HEADS, 1), const),
        ],
        out_specs=[
            pl.BlockSpec((1, tm, FOX_W), row),
            pl.BlockSpec((1, tm, 2 * FOX_W), row),
            pl.BlockSpec((1, tm, DIFF_W), row),
            pl.BlockSpec((1, tm, 2 * DIFF_W), row),
            pl.BlockSpec((1, FOX_HEADS, tm), col),
            pl.BlockSpec((1, FOX_HEADS, tm), col),
        ],
        scratch_shapes=[pltpu.VMEM((FOX_HEADS, LANES), F32)],
        compiler_params=_params("parallel", "arbitrary"),
        name="attn_proj",
    )(x, g.reshape(1, d), w_main, wft, bf)


def _attn_prompt_kernel(*refs, fox, tq, lam_init):
    if fox:
        q_ref, k_ref, v_ref, cq_ref, ck_ref, o_ref, kb_scr, vb_scr = refs
    else:
        q_ref, k_ref, v_ref, lam_ref, g_ref, o_ref, kb_scr, vb_scr = refs
    p = pl.program_id(1)
    qi = pl.program_id(2)

    @pl.when(qi == 0)
    def _():
        kb_scr[...] = k_ref[0].astype(BF16)
        vb_scr[...] = v_ref[0].astype(BF16)

    q = q_ref[0] * (FOX_DH ** -0.5)
    lane = lax.broadcasted_iota(jnp.int32, q.shape, 1)
    half = LANES // 2
    qs = [jnp.where(lane < half, q, 0.0).astype(BF16), jnp.where(lane >= half, q, 0.0).astype(BF16)]
    if fox:
        cq = cq_ref[0]
        hl = lax.broadcasted_iota(jnp.int32, cq.shape, 1)
        fq = [jnp.sum(jnp.where(hl == 2 * p + c, cq, 0.0), axis=1, keepdims=True) for c in range(2)]
    row = lax.broadcasted_iota(jnp.int32, (tq, tq), 0)
    colm = lax.broadcasted_iota(jnp.int32, (tq, tq), 1)
    causal = colm <= row

    def block(kb, carry, masked):
        start = pl.multiple_of(kb * tq, tq)
        k = kb_scr[pl.ds(start, tq), :]
        v = vb_scr[pl.ds(start, tq), :]
        out = []
        for c in range(2):
            m, l, acc = carry[c]
            s = _dot_nt(qs[c], k)
            if fox:
                fk = ck_ref[0, pl.ds(2 * p + c, 1), pl.ds(start, tq)]
                s = s + (fq[c] - fk)
            if masked:
                s = jnp.where(causal, s, NEG)
            m_new = jnp.maximum(m, jnp.max(s, axis=1, keepdims=True))
            alpha = jnp.exp(m - m_new)
            pr = jnp.exp(s - m_new)
            l = alpha * l + jnp.sum(pr, axis=1, keepdims=True)
            acc = alpha * acc + _dot(pr.astype(BF16), v)
            out.append((m_new, l, acc))
        return tuple(out)

    init = tuple((jnp.full((tq, 1), NEG, F32), jnp.zeros((tq, 1), F32), jnp.zeros((tq, LANES), F32))
                 for _ in range(2))
    carry = lax.fori_loop(0, qi, lambda kb, cr: block(kb, cr, False), init)
    (_, l0, a0), (_, l1, a1) = block(qi, carry, True)
    o0 = a0 / l0
    o1 = a1 / l1
    if fox:
        o_ref[0] = jnp.where(lane < half, o0, o1)
    else:
        lp = lam_ref[...]
        lam = (jnp.exp(jnp.sum(lp[0:1] * lp[1:2], axis=1, keepdims=True))
               - jnp.exp(jnp.sum(lp[2:3] * lp[3:4], axis=1, keepdims=True)) + lam_init)
        o = o0 - lam * o1
        o_ref[0] = _rms(o, g_ref[...]) * (1.0 - lam_init)


def _attn_prompt(fox, q, kv, extra_a, extra_b, lam_init):
    b, t, w = q.shape
    groups = w // LANES
    tq = min(256, t)
    kernel = functools.partial(_attn_prompt_kernel, fox=fox, tq=tq, lam_init=lam_init)
    if fox:
        extra_specs = [pl.BlockSpec((1, tq, FOX_HEADS), lambda i, p, j: (i, j, 0)),
                       pl.BlockSpec((1, FOX_HEADS, t), lambda i, p, j: (i, 0, 0))]
    else:
        extra_specs = [pl.BlockSpec(extra_a.shape, lambda i, p, j: (0, 0)),
                       pl.BlockSpec(extra_b.shape, lambda i, p, j: (0, 0))]
    return pl.pallas_call(
        kernel,
        out_shape=jax.ShapeDtypeStruct((b, t, w), F32),
        grid=(b, groups, t // tq),
        in_specs=[
            pl.BlockSpec((1, tq, LANES), lambda i, p, j: (i, j, p)),
            pl.BlockSpec((1, t, LANES), lambda i, p, j: (i, 0, p)),
            pl.BlockSpec((1, t, LANES), lambda i, p, j: (i, 0, groups + p)),
        ] + extra_specs,
        out_specs=pl.BlockSpec((1, tq, LANES), lambda i, p, j: (i, j, p)),
        scratch_shapes=[pltpu.VMEM((t, LANES), BF16), pltpu.VMEM((t, LANES), BF16)],
        compiler_params=_params("parallel", "parallel", "arbitrary"),
        name="fox_prompt" if fox else "diff_prompt",
    )(q, kv, kv, extra_a, extra_b)


def _decode_kernel(pt_ref, qf_ref, qd_ref, nkf_ref, nkd_ref, nlf_ref, lam_ref, g_ref, sfx_ref,
                   cf_ref, lf_ref, cd_ref, of_ref, od_ref,
                   mf_scr, lfs_scr, af_scr, md_scr, lds_scr, ad_scr, carry_scr, *, lam_init):
    del pt_ref
    b = pl.program_id(0)
    j = pl.program_id(1)
    page = cf_ref.shape[1]
    fj = 2 * FOX_HEADS
    dj = 2 * DIFF_HEADS
    scale = FOX_DH ** -0.5

    qf = qf_ref[0]
    qd4 = qd_ref[0]
    lane4 = lax.broadcasted_iota(jnp.int32, qd4.shape, 1)
    half = LANES // 2
    qd = jnp.concatenate([jnp.where(lane4 < half, qd4, 0.0), jnp.where(lane4 >= half, qd4, 0.0)], axis=0)

    @pl.when(j == 0)
    def _():
        kn = nkf_ref[0, 0]
        mf_scr[...] = jnp.broadcast_to(jnp.sum(qf * kn, axis=1, keepdims=True) * scale, mf_scr.shape)
        lfs_scr[...] = jnp.ones_like(lfs_scr)
        af_scr[...] = nkf_ref[0, 1]
        kd = nkd_ref[0, 0]
        kd8 = jnp.concatenate([kd, kd], axis=0)
        md_scr[...] = jnp.broadcast_to(jnp.sum(qd * kd8, axis=1, keepdims=True) * scale, md_scr.shape)
        lds_scr[...] = jnp.ones_like(lds_scr)
        vd = nkd_ref[0, 1]
        ad_scr[...] = jnp.concatenate([vd, vd], axis=0)
        nl = nlf_ref[...]
        sl = lax.broadcasted_iota(jnp.int32, nl.shape, 1)
        carry_scr[...] = jnp.broadcast_to(
            jnp.sum(jnp.where(sl == b, nl, 0.0), axis=1, keepdims=True), carry_scr.shape)

    xf = cf_ref[0].reshape(page * fj, FOX_DH).astype(BF16)
    lt = lf_ref[0]
    carry = carry_scr[:, 0:1]
    hi, mid, lo = _split3(lt)
    r3 = _dot(jnp.concatenate([hi, mid, lo], axis=0), sfx_ref[...])
    bias = carry + r3[0:8] + r3[8:16] + r3[16:24]
    s = _dot_nt(qf.astype(BF16), xf) * scale + bias
    rr = lax.broadcasted_iota(jnp.int32, s.shape, 0)
    ll = lax.broadcasted_iota(jnp.int32, s.shape, 1)
    s = jnp.where((ll % fj) == rr, s, NEG)
    m_old = mf_scr[:, 0:1]
    m_new = jnp.maximum(m_old, jnp.max(s, axis=1, keepdims=True))
    alpha = jnp.exp(m_old - m_new)
    pr = jnp.exp(s - m_new)
    lfs_scr[...] = alpha * lfs_scr[...] + jnp.sum(pr, axis=1, keepdims=True)
    pv = pltpu.roll(pr, FOX_HEADS, axis=1).astype(BF16)
    af_scr[...] = alpha * af_scr[...] + _dot(pv, xf)
    mf_scr[...] = jnp.broadcast_to(m_new, mf_scr.shape)
    carry_scr[...] = carry_scr[...] + jnp.sum(lt, axis=1, keepdims=True)

    xd = cd_ref[0].reshape(page * dj, 2 * DIFF_DH).astype(BF16)
    sd = _dot_nt(qd.astype(BF16), xd) * scale
    rd = lax.broadcasted_iota(jnp.int32, sd.shape, 0)
    ld = lax.broadcasted_iota(jnp.int32, sd.shape, 1)
    sd = jnp.where((ld % dj) == (rd % DIFF_HEADS), sd, NEG)
    md_old = md_scr[:, 0:1]
    md_new = jnp.maximum(md_old, jnp.max(sd, axis=1, keepdims=True))
    alphad = jnp.exp(md_old - md_new)
    pd = jnp.exp(sd - md_new)
    lds_scr[...] = alphad * lds_scr[...] + jnp.sum(pd, axis=1, keepdims=True)
    pvd = pltpu.roll(pd, DIFF_HEADS, axis=1).astype(BF16)
    ad_scr[...] = alphad * ad_scr[...] + _dot(pvd, xd)
    md_scr[...] = jnp.broadcast_to(md_new, md_scr.shape)

    @pl.when(j == pl.num_programs(1) - 1)
    def _():
        of_ref[0] = af_scr[...] / lfs_scr[:, 0:1]
        od_all = ad_scr[...] / lds_scr[:, 0:1]
        lp = lam_ref[...]
        lam = (jnp.exp(jnp.sum(lp[0:1] * lp[1:2], axis=1, keepdims=True))
               - jnp.exp(jnp.sum(lp[2:3] * lp[3:4], axis=1, keepdims=True)) + lam_init)
        o = od_all[0:DIFF_HEADS] - lam * od_all[DIFF_HEADS:]
        od_ref[0] = _rms(o, g_ref[...]) * (1.0 - lam_init)


def _decode(page_table, qf, qd, new_kvf, new_kvd, new_lft, lam_p, subln_g, cache_f, cache_lt, cache_d, lam_init):
    nb, n_pages = page_table.shape
    page = cache_f.shape[1]
    fj, dj = 2 * FOX_HEADS, 2 * DIFF_HEADS
    k_idx = lax.broadcasted_iota(jnp.int32, (page, page * fj), 0)
    l_idx = lax.broadcasted_iota(jnp.int32, (page, page * fj), 1) // fj
    sfx = jnp.where(k_idx > l_idx, 1.0, 0.0).astype(BF16)
    rev = lambda i, j, pt: (pt[i, n_pages - 1 - j], 0, 0, 0)
    rev3 = lambda i, j, pt: (pt[i, n_pages - 1 - j], 0, 0)
    samp3 = lambda i, j, pt: (i, 0, 0)
    samp4 = lambda i, j, pt: (i, 0, 0, 0)
    const = lambda i, j, pt: (0, 0)
    grid_spec = pltpu.PrefetchScalarGridSpec(
        num_scalar_prefetch=1,
        grid=(nb, n_pages),
        in_specs=[
            pl.BlockSpec((1, FOX_HEADS, FOX_DH), samp3),
            pl.BlockSpec((1, DIFF_HEADS, 2 * DIFF_DH), samp3),
            pl.BlockSpec((1, 2, FOX_HEADS, FOX_DH), samp4),
            pl.BlockSpec((1, 2, DIFF_HEADS, 2 * DIFF_DH), samp4),
            pl.BlockSpec(new_lft.shape, const),
            pl.BlockSpec(lam_p.shape, const),
            pl.BlockSpec(subln_g.shape, const),
            pl.BlockSpec(sfx.shape, const),
            pl.BlockSpec((1, page, fj, FOX_DH), rev),
            pl.BlockSpec((1, FOX_HEADS, page), rev3),
            pl.BlockSpec((1, page, dj, 2 * DIFF_DH), rev),
        ],
        out_specs=[
            pl.BlockSpec((1, FOX_HEADS, FOX_DH), samp3),
            pl.BlockSpec((1, DIFF_HEADS, 2 * DIFF_DH), samp3),
        ],
        scratch_shapes=[
            pltpu.VMEM((FOX_HEADS, LANES), F32), pltpu.VMEM((FOX_HEADS, LANES), F32),
            pltpu.VMEM((FOX_HEADS, FOX_DH), F32),
            pltpu.VMEM((2 * DIFF_HEADS, LANES), F32), pltpu.VMEM((2 * DIFF_HEADS, LANES), F32),
            pltpu.VMEM((2 * DIFF_HEADS, 2 * DIFF_DH), F32),
            pltpu.VMEM((FOX_HEADS, LANES), F32),
        ],
    )
    return pl.pallas_call(
        functools.partial(_decode_kernel, lam_init=lam_init),
        out_shape=[jax.ShapeDtypeStruct((nb, FOX_HEADS, FOX_DH), F32),
                   jax.ShapeDtypeStruct((nb, DIFF_HEADS, 2 * DIFF_DH), F32)],
        grid_spec=grid_spec,
        compiler_params=_params("parallel", "arbitrary"),
        name="decode_attn",
    )(page_table, qf, qd, new_kvf, new_kvd, new_lft, lam_p, subln_g, sfx, cache_f, cache_lt, cache_d)


def _rec_proj_kernel(x_ref, g_ref, w_ref, wgt_ref, bg_ref, q_ref, k_ref, v_ref, og_ref, u_ref, gt_ref):
    h = _rms(x_ref[0], g_ref[...]).astype(BF16)
    c0 = 0
    q_ref[0] = _dot(h, w_ref[:, c0:c0 + MLSTM_QK]) * (MLSTM_DK ** -0.5)
    c0 += MLSTM_QK
    k_ref[0] = _dot(h, w_ref[:, c0:c0 + MLSTM_QK])
    c0 += MLSTM_QK
    v_ref[0] = _dot(h, w_ref[:, c0:c0 + MLSTM_V])
    c0 += MLSTM_V
    og_ref[0] = _dot(h, w_ref[:, c0:c0 + MLSTM_V])
    c0 += MLSTM_V
    ua = _dot(h, w_ref[:, c0:c0 + CONV_CH])
    c0 += CONV_CH
    ub = _dot(h, w_ref[:, c0:c0 + CONV_CH])
    u_ref[0] = ua * jax.nn.sigmoid(ub)
    z = _dot_nt(wgt_ref[...], h) + bg_ref[...]
    rowi = lax.broadcasted_iota(jnp.int32, z.shape, 0)
    gt_ref[0] = jnp.where(rowi < MLSTM_HEADS, z, _log_sigmoid(z))


def _rec_proj(x, g, w_main, wgt, bg):
    b, t, d = x.shape
    tm = min(512, t)
    row = lambda i, j: (i, j, 0)
    const = lambda i, j: (0, 0)
    widths = [MLSTM_QK, MLSTM_QK, MLSTM_V, MLSTM_V, CONV_CH]
    return pl.pallas_call(
        _rec_proj_kernel,
        out_shape=[jax.ShapeDtypeStruct((b, t, w), F32) for w in widths]
        + [jax.ShapeDtypeStruct((b, 2 * MLSTM_HEADS, t), F32)],
        grid=(b, t // tm),
        in_specs=[
            pl.BlockSpec((1, tm, d), row),
            pl.BlockSpec((1, d), const),
            pl.BlockSpec(w_main.shape, const),
            pl.BlockSpec(wgt.shape, const),
            pl.BlockSpec(bg.shape, const),
        ],
        out_specs=[pl.BlockSpec((1, tm, w), row) for w in widths]
        + [pl.BlockSpec((1, 2 * MLSTM_HEADS, tm), lambda i, j: (i, 0, j))],
        compiler_params=_params("parallel", "parallel"),
        name="rec_proj",
    )(x, g.reshape(1, d), w_main, wgt, bg)


def _mlstm_prompt_kernel(q_ref, k_ref, v_ref, og_ref, gt_ref, gc_ref, ng_ref,
                         hm_ref, c_ref, n_ref, m_ref, ct_scr, n_scr, m_scr, *, chunk):
    t = pl.program_id(1)
    L = chunk
    half = LANES // 2

    @pl.when(t == 0)
    def _():
        ct_scr[...] = jnp.zeros_like(ct_scr)
        n_scr[...] = jnp.zeros_like(n_scr)
        m_scr[...] = jnp.full(m_scr.shape, NEG, F32)

    gt = gt_ref[0]
    gc = gc_ref[0]
    cum_rows = _dot3_left(gt, _tri(L, upper=True))
    cum_cols = _dot3_right(_tri(L, upper=False), gc)
    rr = lax.broadcasted_iota(jnp.int32, (L, L), 0)
    cc = lax.broadcasted_iota(jnp.int32, (L, L), 1)
    tri = cc <= rr
    lane = lax.broadcasted_iota(jnp.int32, (L, LANES), 1)

    for pair in range(MLSTM_HEADS // 2):
        qp = q_ref[0, :, pair * LANES:(pair + 1) * LANES]
        kp = k_ref[0, :, pair * LANES:(pair + 1) * LANES]
        kp_b = kp.astype(BF16)
        kt_b = kp.T.astype(BF16)
        ct = ct_scr[pair]
        n_row = n_scr[pair]
        ct_new = []
        n_new = []
        for c in range(2):
            h = 2 * pair + c
            mine = (lane < half) if c == 0 else (lane >= half)
            qm = jnp.where(mine, qp, 0.0)
            qm_b = qm.astype(BF16)
            vh = v_ref[0, :, h * MLSTM_DV:(h + 1) * MLSTM_DV]
            a_col = cum_cols[:, MLSTM_HEADS + h:MLSTM_HEADS + h + 1]
            ig_col = gc[:, h:h + 1]
            b_row = gt[h:h + 1, :] - cum_rows[MLSTM_HEADS + h:MLSTM_HEADS + h + 1, :]
            m_prev = m_scr[h:h + 1, 0:1]
            d = jnp.where(tri, a_col + b_row, NEG)
            m_inter = a_col + m_prev
            m_t = jnp.maximum(jnp.max(d, axis=1, keepdims=True), m_inter)
            s = _dot_nt(qm_b, kp_b) * jnp.exp(d - m_t)
            w_inter = jnp.exp(m_inter - m_t)
            num = _dot(s.astype(BF16), vh.astype(BF16)) + w_inter * _dot(qm_b, ct.astype(BF16))
            den = jnp.sum(s, axis=1, keepdims=True) + w_inter * jnp.sum(qm * n_row, axis=1, keepdims=True)
            hout = num / jnp.maximum(jnp.abs(den), jnp.exp(-m_t))
            gh = ng_ref[:, h * MLSTM_DV:(h + 1) * MLSTM_DV]
            oh = og_ref[0, :, h * MLSTM_DV:(h + 1) * MLSTM_DV]
            hm_ref[0, :, h * MLSTM_DV:(h + 1) * MLSTM_DV] = _rms(hout, gh) * jax.nn.sigmoid(oh)
            m_new = m_t[L - 1:L, :]
            cum_last = a_col[L - 1:L, :]
            decay = jnp.exp(cum_last + m_prev - m_new)
            w_key = jnp.exp(cum_last - a_col + ig_col - m_new)
            upd = _dot(kt_b, (w_key * vh).astype(BF16))
            lo, hi_ = c * half, (c + 1) * half
            ct_new.append(decay * ct[lo:hi_] + upd[lo:hi_])
            n_new.append(decay * n_row + jnp.sum(w_key * kp, axis=0, keepdims=True))
            m_scr[h:h + 1, :] = jnp.broadcast_to(m_new, (1, LANES))
        ct_scr[pair] = jnp.concatenate(ct_new, axis=0)
        lane1 = lax.broadcasted_iota(jnp.int32, (1, LANES), 1)
        n_scr[pair] = jnp.where(lane1 < half, n_new[0], n_new[1])

    @pl.when(t == pl.num_programs(1) - 1)
    def _():
        for pair in range(MLSTM_HEADS // 2):
            c_pair = ct_scr[pair].T
            n_row = n_scr[pair]
            for c in range(2):
                h = 2 * pair + c
                c_ref[0, h] = c_pair[:, c * half:(c + 1) * half]
                n_ref[0, h:h + 1, :] = n_row[:, c * half:(c + 1) * half]
                m_ref[0, :, h:h + 1] = m_scr[h:h + 1, 0:1]


def _mlstm_prompt(q, k, v, og, gt, gc, norm_g):
    b, t, _ = q.shape
    chunk = min(256, t)
    row = lambda i, j: (i, j, 0)
    const = lambda i, j: (0, 0)
    return pl.pallas_call(
        functools.partial(_mlstm_prompt_kernel, chunk=chunk),
        out_shape=[
            jax.ShapeDtypeStruct((b, t, MLSTM_V), F32),
            jax.ShapeDtypeStruct((b, MLSTM_HEADS, MLSTM_DV, MLSTM_DK), F32),
            jax.ShapeDtypeStruct((b, MLSTM_HEADS, MLSTM_DK), F32),
            jax.ShapeDtypeStruct((b, 1, MLSTM_HEADS), F32),
        ],
        grid=(b, t // chunk),
        in_specs=[
            pl.BlockSpec((1, chunk, MLSTM_QK), row),
            pl.BlockSpec((1, chunk, MLSTM_QK), row),
            pl.BlockSpec((1, chunk, MLSTM_V), row),
            pl.BlockSpec((1, chunk, MLSTM_V), row),
            pl.BlockSpec((1, 2 * MLSTM_HEADS, chunk), lambda i, j: (i, 0, j)),
            pl.BlockSpec((1, chunk, 2 * MLSTM_HEADS), row),
            pl.BlockSpec((1, MLSTM_V), const),
        ],
        out_specs=[
            pl.BlockSpec((1, chunk, MLSTM_V), row),
            pl.BlockSpec((1, MLSTM_HEADS, MLSTM_DV, MLSTM_DK), lambda i, j: (i, 0, 0, 0)),
            pl.BlockSpec((1, MLSTM_HEADS, MLSTM_DK), lambda i, j: (i, 0, 0)),
            pl.BlockSpec((1, 1, MLSTM_HEADS), lambda i, j: (i, 0, 0)),
        ],
        scratch_shapes=[
            pltpu.VMEM((MLSTM_HEADS // 2, LANES, MLSTM_DV), F32),
            pltpu.VMEM((MLSTM_HEADS // 2, 1, LANES), F32),
            pltpu.VMEM((SUBLANES, LANES), F32),
        ],
        compiler_params=_params("parallel", "arbitrary"),
        name="mlstm_prompt",
    )(q, k, v, og, gt, gc, norm_g.reshape(1, MLSTM_V))


def _mlstm_sample_kernel(q_ref, k_ref, v_ref, og_ref, gc_ref, m_ref, kall_ref, vall_ref, ng_ref, c_ref, n_ref,
                         hm_ref, co_ref, no_ref, mo_ref, vt_scr, *, bs):
    i = pl.program_id(0)
    nb = kall_ref.shape[0]

    @pl.when(i == 0)
    def _():
        for h in range(MLSTM_HEADS):
            vt_scr[h] = vall_ref[:, h * MLSTM_DV:(h + 1) * MLSTM_DV].T

    rows = lax.broadcasted_iota(jnp.int32, (nb, MLSTM_DK), 0)

    def body(bl, _):
        b = i * bs + bl
        q_row = q_ref[bl]
        k_row = k_ref[bl]
        v_row = v_ref[bl]
        og_row = og_ref[bl]
        g_row = gc_ref[bl]
        m_row = m_ref[bl]
        for h in range(MLSTM_HEADS):
            qh = q_row[:, h * MLSTM_DK:(h + 1) * MLSTM_DK]
            kh = k_row[:, h * MLSTM_DK:(h + 1) * MLSTM_DK]
            vh = v_row[:, h * MLSTM_DV:(h + 1) * MLSTM_DV]
            it = g_row[:, h:h + 1]
            lf = g_row[:, MLSTM_HEADS + h:MLSTM_HEADS + h + 1]
            m_prev = m_row[:, h:h + 1]
            m_inter = lf + m_prev
            m_t = jnp.maximum(it, m_inter)
            e_i = jnp.exp(it - m_t)
            w_inter = jnp.exp(m_inter - m_t)
            s = jnp.sum(qh * kh, axis=1, keepdims=True) * e_i
            cm = c_ref[bl, h]
            nh = n_ref[bl, h:h + 1, :]
            num = s * vh + w_inter * _dot_nt(qh.astype(BF16), cm.astype(BF16))
            den = s + w_inter * jnp.sum(nh * qh, axis=1, keepdims=True)
            hrow = num / jnp.maximum(jnp.abs(den), jnp.exp(-m_t))
            gh = ng_ref[:, h * MLSTM_DV:(h + 1) * MLSTM_DV]
            oh = og_row[:, h * MLSTM_DV:(h + 1) * MLSTM_DV]
            hm_ref[bl, :, h * MLSTM_DV:(h + 1) * MLSTM_DV] = _rms(hrow, gh) * jax.nn.sigmoid(oh)
            k_all = kall_ref[:, h * MLSTM_DK:(h + 1) * MLSTM_DK]
            k_sel = jnp.where(rows == b, k_all, 0.0).astype(BF16)
            outer = _dot(vt_scr[h].astype(BF16), k_sel)
            co_ref[bl, h] = w_inter * cm + e_i * outer
            no_ref[bl, h:h + 1, :] = w_inter * nh + e_i * kh
            mo_ref[bl, :, h:h + 1] = m_t
        return 0

    lax.fori_loop(0, bs, body, 0)


def _mlstm_sample(q, k, v, og, gc, norm_g, c0, n0, m0):
    nb = q.shape[0]
    bs = min(16, nb)
    full = lambda i: (0, 0)
    rows = lambda i: (i, 0, 0)
    per_sample = [a.reshape(nb, 1, a.shape[1]) for a in (q, k, v, og, gc, m0)]
    hm, c1, n1, m1 = pl.pallas_call(
        functools.partial(_mlstm_sample_kernel, bs=bs),
        out_shape=[
            jax.ShapeDtypeStruct((nb, 1, MLSTM_V), F32),
            jax.ShapeDtypeStruct(c0.shape, F32),
            jax.ShapeDtypeStruct(n0.shape, F32),
            jax.ShapeDtypeStruct((nb, 1, MLSTM_HEADS), F32),
        ],
        grid=(nb // bs,),
        in_specs=[pl.BlockSpec((bs, 1, a.shape[2]), rows) for a in per_sample] + [
            pl.BlockSpec(k.shape, full),
            pl.BlockSpec(v.shape, full),
            pl.BlockSpec((1, MLSTM_V), full),
            pl.BlockSpec((bs, MLSTM_HEADS, MLSTM_DV, MLSTM_DK), lambda i: (i, 0, 0, 0)),
            pl.BlockSpec((bs, MLSTM_HEADS, MLSTM_DK), rows),
        ],
        out_specs=[
            pl.BlockSpec((bs, 1, MLSTM_V), rows),
            pl.BlockSpec((bs, MLSTM_HEADS, MLSTM_DV, MLSTM_DK), lambda i: (i, 0, 0, 0)),
            pl.BlockSpec((bs, MLSTM_HEADS, MLSTM_DK), rows),
            pl.BlockSpec((bs, 1, MLSTM_HEADS), rows),
        ],
        scratch_shapes=[pltpu.VMEM((MLSTM_HEADS, MLSTM_DV, nb), F32)],
        compiler_params=_params("arbitrary"),
        name="mlstm_sample",
    )(*per_sample, k, v, norm_g.reshape(1, MLSTM_V), c0, n0)
    return hm.reshape(nb, MLSTM_V), c1, n1, m1.reshape(nb, MLSTM_HEADS)


_CONV_HALO = 32
_CONV_ROWS = 32


def _ln_silu(y, g, b):
    yc = y - jnp.mean(y, axis=-1, keepdims=True)
    var = jnp.mean(yc * yc, axis=-1, keepdims=True)
    z = yc * lax.rsqrt(var + NORM_EPS) * g + b
    return z * jax.nn.sigmoid(z)


def _conv_prompt_kernel(u_ref, prev_ref, w_ref, b_ref, g_ref, be_ref, c_ref, st_ref, full_scr, *, tt):
    t = pl.program_id(1)
    pad = _CONV_HALO - (CONV_W - 1)

    @pl.when(t == 0)
    def _():
        full_scr[0:_CONV_HALO, :] = jnp.zeros((_CONV_HALO, CONV_CH), F32)
        full_scr[pad:_CONV_HALO, :] = prev_ref[0]

    full_scr[_CONV_HALO:, :] = u_ref[0]
    for r0 in range(0, tt, _CONV_ROWS):
        acc = jnp.broadcast_to(b_ref[...], (_CONV_ROWS, CONV_CH))
        for k in range(CONV_W):
            acc = acc + w_ref[k:k + 1, :] * full_scr[r0 + pad + k:r0 + pad + k + _CONV_ROWS, :]
        c_ref[0, r0:r0 + _CONV_ROWS, :] = _ln_silu(acc, g_ref[...], be_ref[...])

    @pl.when(t == pl.num_programs(1) - 1)
    def _():
        st_ref[0] = full_scr[_CONV_HALO + tt - (CONV_W - 1):, :]

    full_scr[0:_CONV_HALO, :] = full_scr[tt:tt + _CONV_HALO, :]


def _conv_prompt(u, prev, w, bias, ln_g, ln_b):
    b, t, ch = u.shape
    tt = min(256, t)
    const = lambda i, j: (0, 0)
    return pl.pallas_call(
        functools.partial(_conv_prompt_kernel, tt=tt),
        out_shape=[jax.ShapeDtypeStruct((b, t, ch), F32),
                   jax.ShapeDtypeStruct((b, CONV_W - 1, ch), F32)],
        grid=(b, t // tt),
        in_specs=[
            pl.BlockSpec((1, tt, ch), lambda i, j: (i, j, 0)),
            pl.BlockSpec((1, CONV_W - 1, ch), lambda i, j: (i, 0, 0)),
            pl.BlockSpec((CONV_W, ch), const),
            pl.BlockSpec((1, ch), const),
            pl.BlockSpec((1, ch), const),
            pl.BlockSpec((1, ch), const),
        ],
        out_specs=[pl.BlockSpec((1, tt, ch), lambda i, j: (i, j, 0)),
                   pl.BlockSpec((1, CONV_W - 1, ch), lambda i, j: (i, 0, 0))],
        scratch_shapes=[pltpu.VMEM((_CONV_HALO + tt, ch), F32)],
        compiler_params=_params("parallel", "arbitrary"),
        name="conv_prompt",
    )(u, prev, w, bias.reshape(1, ch), ln_g.reshape(1, ch), ln_b.reshape(1, ch))


def _conv_sample_kernel(u_ref, prev_ref, w_ref, b_ref, g_ref, be_ref, c_ref, st_ref, *, bs):
    hist = CONV_W - 1

    def body(s, _):
        prev = prev_ref[s]
        u_row = u_ref[s]
        y = (jnp.sum(prev * w_ref[0:hist, :], axis=0, keepdims=True)
             + w_ref[hist:hist + 1, :] * u_row + b_ref[...])
        c_ref[s] = _ln_silu(y, g_ref[...], be_ref[...])
        st_ref[s, 0:hist - 1, :] = prev[1:hist, :]
        st_ref[s, hist - 1:hist, :] = u_row
        return 0

    lax.fori_loop(0, bs, body, 0)


def _conv_sample(u, prev, w, bias, ln_g, ln_b):
    nb, ch = u.shape
    bs = min(32, nb)
    const = lambda i: (0, 0)
    rows = lambda i: (i, 0, 0)
    c, st = pl.pallas_call(
        functools.partial(_conv_sample_kernel, bs=bs),
        out_shape=[jax.ShapeDtypeStruct((nb, 1, ch), F32),
                   jax.ShapeDtypeStruct((nb, CONV_W - 1, ch), F32)],
        grid=(nb // bs,),
        in_specs=[
            pl.BlockSpec((bs, 1, ch), rows),
            pl.BlockSpec((bs, CONV_W - 1, ch), rows),
            pl.BlockSpec((CONV_W, ch), const),
            pl.BlockSpec((1, ch), const),
            pl.BlockSpec((1, ch), const),
            pl.BlockSpec((1, ch), const),
        ],
        out_specs=[pl.BlockSpec((bs, 1, ch), rows),
                   pl.BlockSpec((bs, CONV_W - 1, ch), rows)],
        compiler_params=_params("parallel"),
        name="conv_sample",
    )(u.reshape(nb, 1, ch), prev, w, bias.reshape(1, ch), ln_g.reshape(1, ch), ln_b.reshape(1, ch))
    return c.reshape(nb, ch), st


def kernel(x_prompt, x_sample, cache_fox_kv, cache_fox_logf, cache_diff_kv, state_mlstm_C, state_mlstm_n,
           state_mlstm_m, state_conv, page_table, norm_g, final_g, ffn_w_in, ffn_w_out, attn_w_in, attn_b_f,
           diff_lam, diff_subln_g, attn_w_out, rec_w_in, rec_b_i, rec_b_f, mlstm_norm_g, conv_w, conv_b,
           conv_ln_g, conv_ln_b, rec_w_out):
    bp, t, d = x_prompt.shape
    nb = x_sample.shape[0]
    depth = norm_g.shape[0]
    n_pool, page = cache_fox_kv.shape[1], cache_fox_kv.shape[2]
    xp = x_prompt.reshape(bp * t, d)
    xs = x_sample.reshape(nb, d)
    w_in_b = ffn_w_in.astype(BF16)
    w_out_b = ffn_w_out.astype(BF16)

    outs = {k: [] for k in ("fkv_p", "fkv_s", "flf_p", "flf_s", "dkv_p", "dkv_s",
                            "c_p", "c_s", "n_p", "n_s", "m_p", "m_s", "cv_p", "cv_s")}
    for l in range(depth):
        j = l // 2
        xp = _ffn(xp, norm_g[l, 0], w_in_b[l, 0], w_out_b[l, 0])
        xs = _ffn(xs, norm_g[l, 0], w_in_b[l, 0], w_out_b[l, 0])
        if l % 2 == 0:
            lam_init = 0.8 - 0.6 * math.exp(-0.3 * l)
            w = attn_w_in[j]
            c1, c2 = 3 * FOX_W, 3 * FOX_W + FOX_HEADS
            w_main = jnp.concatenate([w[:, :c1], w[:, c2:]], axis=1).astype(BF16)
            wft = w[:, c1:c2].T.astype(BF16)
            bf = attn_b_f[j].reshape(FOX_HEADS, 1)
            w_o = attn_w_out[j].astype(BF16)
            qf, kvf, qd, kvd, lft, cumt = _attn_proj(xp.reshape(bp, t, d), norm_g[l, 1], w_main, wft, bf)
            o_f = _attn_prompt(True, qf, kvf, jnp.swapaxes(cumt, 1, 2), cumt, lam_init)
            o_d = _attn_prompt(False, qd, kvd, diff_lam[j], diff_subln_g[j].reshape(1, -1), lam_init)
            xp = _merge(xp, o_f.reshape(bp * t, FOX_W), o_d.reshape(bp * t, DIFF_W), w_o)
            outs["fkv_p"].append(kvf.reshape(bp, t, 2, FOX_HEADS, FOX_DH))
            outs["flf_p"].append(jnp.swapaxes(lft, 1, 2))
            outs["dkv_p"].append(kvd.reshape(bp, t, 2, DIFF_HEADS, 2 * DIFF_DH))
            sqf, skvf, sqd, skvd, slft, _ = _attn_proj(xs.reshape(1, nb, d), norm_g[l, 1], w_main, wft, bf)
            so_f, so_d = _decode(
                page_table,
                sqf.reshape(nb, FOX_HEADS, FOX_DH), sqd.reshape(nb, DIFF_HEADS, 2 * DIFF_DH),
                skvf.reshape(nb, 2, FOX_HEADS, FOX_DH), skvd.reshape(nb, 2, DIFF_HEADS, 2 * DIFF_DH),
                slft[0], diff_lam[j], diff_subln_g[j].reshape(1, -1),
                cache_fox_kv[j].reshape(n_pool, page, 2 * FOX_HEADS, FOX_DH),
                jnp.swapaxes(cache_fox_logf[j], 1, 2),
                cache_diff_kv[j].reshape(n_pool, page, 2 * DIFF_HEADS, 2 * DIFF_DH),
                lam_init)
            xs = _merge(xs, so_f.reshape(nb, FOX_W), so_d.reshape(nb, DIFF_W), w_o)
            outs["fkv_s"].append(skvf.reshape(nb, 1, 2, FOX_HEADS, FOX_DH))
            outs["flf_s"].append(jnp.swapaxes(slft, 1, 2).reshape(nb, 1, FOX_HEADS))
            outs["dkv_s"].append(skvd.reshape(nb, 1, 2, DIFF_HEADS, 2 * DIFF_DH))
        else:
            w = rec_w_in[j]
            c1 = 2 * MLSTM_QK + MLSTM_V
            c2 = c1 + 2 * MLSTM_HEADS
            w_main = jnp.concatenate([w[:, :c1], w[:, c2:]], axis=1).astype(BF16)
            wgt = w[:, c1:c2].T.astype(BF16)
            bg = jnp.concatenate([rec_b_i[j], rec_b_f[j]]).reshape(2 * MLSTM_HEADS, 1)
            w_o = rec_w_out[j].astype(BF16)
            q, k, v, og, u, gt = _rec_proj(xp.reshape(bp, t, d), norm_g[l, 1], w_main, wgt, bg)
            hm, c_p, n_p, m_p = _mlstm_prompt(q, k, v, og, gt, jnp.swapaxes(gt, 1, 2), mlstm_norm_g[j])
            cv, st_p = _conv_prompt(u, jnp.zeros((bp, CONV_W - 1, CONV_CH), F32), conv_w[j], conv_b[j],
                                    conv_ln_g[j], conv_ln_b[j])
            xp = _merge(xp, hm.reshape(bp * t, MLSTM_V), cv.reshape(bp * t, CONV_CH), w_o)
            outs["c_p"].append(c_p)
            outs["n_p"].append(n_p)
            outs["m_p"].append(m_p.reshape(bp, MLSTM_HEADS))
            outs["cv_p"].append(st_p)
            sq, sk, sv, sog, su, sgt = _rec_proj(xs.reshape(1, nb, d), norm_g[l, 1], w_main, wgt, bg)
            shm, c_s, n_s, m_s = _mlstm_sample(sq[0], sk[0], sv[0], sog[0], sgt[0].T, mlstm_norm_g[j],
                                               state_mlstm_C[j], state_mlstm_n[j], state_mlstm_m[j])
            scv, st_s = _conv_sample(su[0], state_conv[j], conv_w[j], conv_b[j], conv_ln_g[j], conv_ln_b[j])
            xs = _merge(xs, shm, scv, w_o)
            outs["c_s"].append(c_s)
            outs["n_s"].append(n_s)
            outs["m_s"].append(m_s)
            outs["cv_s"].append(st_s)
        last = l == depth - 1
        xp = _ffn(xp, norm_g[l, 2], w_in_b[l, 1], w_out_b[l, 1], final_g if last else None)
        xs = _ffn(xs, norm_g[l, 2], w_in_b[l, 1], w_out_b[l, 1], final_g if last else None)

    st = jnp.stack
    return (xp.reshape(bp, t, d), xs.reshape(nb, 1, d),
            st(outs["fkv_p"]), st(outs["fkv_s"]), st(outs["flf_p"]), st(outs["flf_s"]),
            st(outs["dkv_p"]), st(outs["dkv_s"]),
            st(outs["c_p"]), st(outs["c_s"]), st(outs["n_p"]), st(outs["n_s"]),
            st(outs["m_p"]), st(outs["m_s"]), st(outs["cv_p"]), st(outs["cv_s"]))
```

```python
import functools
import math

import jax
import jax.numpy as jnp
from jax import lax
from jax.experimental import pallas as pl
from jax.experimental.pallas import tpu as pltpu

F32 = jnp.float32
BF16 = jnp.bfloat16
NORM_EPS = 1e-6
NEG = -1e30
LOG2E = 1.4426950408889634

FOX_HEADS = 8
FOX_DH = 64
DIFF_HEADS = 4
DIFF_DH = 64
MLSTM_HEADS = 4
MLSTM_DK = 64
MLSTM_DV = 128
CONV_CH = 512
CONV_W = 31
FOX_W = FOX_HEADS * FOX_DH
DIFF_W = DIFF_HEADS * 2 * DIFF_DH
MLSTM_QK = MLSTM_HEADS * MLSTM_DK
MLSTM_V = MLSTM_HEADS * MLSTM_DV

LANES = 128
SUBLANES = 8
VMEM_LIMIT_BYTES = 56 * 1024 * 1024

_NT = (((1,), (1,)), ((), ()))


def _params(*sem):
    return pltpu.CompilerParams(dimension_semantics=sem, vmem_limit_bytes=VMEM_LIMIT_BYTES)


def _rms(x, g):
    return x * lax.rsqrt(jnp.mean(x * x, axis=-1, keepdims=True) + NORM_EPS) * g


def _log_sigmoid(z):
    return jnp.minimum(z, 0.0) - jnp.log(1.0 + jnp.exp(-jnp.abs(z)))


def _split3(x):
    hi = x.astype(BF16)
    r = x - hi.astype(F32)
    mid = r.astype(BF16)
    lo = (r - mid.astype(F32)).astype(BF16)
    return hi, mid, lo


def _dot(a, b):
    return jnp.dot(a, b, preferred_element_type=F32)


def _dot_nt(a, b):
    return lax.dot_general(a, b, _NT, preferred_element_type=F32)


def _dot3_left(x, t):
    hi, mid, lo = _split3(x)
    return _dot(hi, t) + _dot(mid, t) + _dot(lo, t)


def _dot3_right(t, x):
    hi, mid, lo = _split3(x)
    return _dot(t, hi) + _dot(t, mid) + _dot(t, lo)


def _tri(n, upper):
    r = lax.broadcasted_iota(jnp.int32, (n, n), 0)
    c = lax.broadcasted_iota(jnp.int32, (n, n), 1)
    keep = (r <= c) if upper else (r >= c)
    return jnp.where(keep, 1.0, 0.0).astype(BF16)


def _ffn_kernel(x_ref, g_ref, wa_ref, wb_ref, wo_ref, fg_ref, o_ref, h_scr, acc_scr, *, final_norm):
    j = pl.program_id(1)

    @pl.when(j == 0)
    def _():
        h_scr[...] = _rms(x_ref[...], g_ref[...]).astype(BF16)
        acc_scr[...] = jnp.zeros_like(acc_scr)

    h = h_scr[...]
    a = _dot(h, wa_ref[...])
    b = _dot(h, wb_ref[...])
    gated = (a * jax.nn.sigmoid(a) * b).astype(BF16)
    acc_scr[...] += _dot(gated, wo_ref[...])

    @pl.when(j == pl.num_programs(1) - 1)
    def _():
        y = x_ref[...] + 0.5 * acc_scr[...]
        if final_norm:
            y = _rms(y, fg_ref[...])
        o_ref[...] = y


def _ffn(x, g, w_in, w_out, final_g=None):
    m, d = x.shape
    f = w_out.shape[0]
    tm = min(1024, m)
    tf = 256
    nf = f // tf
    fg = jnp.ones((1, d), F32) if final_g is None else final_g.reshape(1, d)
    return pl.pallas_call(
        functools.partial(_ffn_kernel, final_norm=final_g is not None),
        out_shape=jax.ShapeDtypeStruct((m, d), F32),
        grid=(m // tm, nf),
        in_specs=[
            pl.BlockSpec((tm, d), lambda i, j: (i, 0)),
            pl.BlockSpec((1, d), lambda i, j: (0, 0)),
            pl.BlockSpec((d, tf), lambda i, j: (0, j)),
            pl.BlockSpec((d, tf), lambda i, j: (0, j + nf)),
            pl.BlockSpec((tf, d), lambda i, j: (j, 0)),
            pl.BlockSpec((1, d), lambda i, j: (0, 0)),
        ],
        out_specs=pl.BlockSpec((tm, d), lambda i, j: (i, 0)),
        scratch_shapes=[pltpu.VMEM((tm, d), BF16), pltpu.VMEM((tm, d), F32)],
        compiler_params=_params("parallel", "arbitrary"),
        name="ffn",
    )(x, g.reshape(1, d), w_in, w_in, w_out, fg)


def _merge_kernel(x_ref, oa_ref, ob_ref, w_ref, o_ref):
    wa = oa_ref.shape[-1]
    y = _dot(oa_ref[...].astype(BF16), w_ref[:wa, :])
    y += _dot(ob_ref[...].astype(BF16), w_ref[wa:, :])
    o_ref[...] = x_ref[...] + y


def _merge(x, oa, ob, w_out):
    m, d = x.shape
    tm = min(1024, m)
    wa, wb = oa.shape[1], ob.shape[1]
    return pl.pallas_call(
        _merge_kernel,
        out_shape=jax.ShapeDtypeStruct((m, d), F32),
        grid=(m // tm,),
        in_specs=[
            pl.BlockSpec((tm, d), lambda i: (i, 0)),
            pl.BlockSpec((tm, wa), lambda i: (i, 0)),
            pl.BlockSpec((tm, wb), lambda i: (i, 0)),
            pl.BlockSpec((wa + wb, d), lambda i: (0, 0)),
        ],
        out_specs=pl.BlockSpec((tm, d), lambda i: (i, 0)),
        compiler_params=_params("parallel"),
        name="merge",
    )(x, oa, ob, w_out)


def _attn_proj_kernel(x_ref, g_ref, w_ref, wft_ref, bf_ref, qf_ref, kvf_ref, qd_ref, kvd_ref,
                      lf_ref, cum_ref, carry_scr):
    t = pl.program_id(1)
    h = _rms(x_ref[0], g_ref[...]).astype(BF16)
    qf_ref[0] = _dot(h, w_ref[:, 0:FOX_W])
    kvf_ref[0] = _dot(h, w_ref[:, FOX_W:3 * FOX_W])
    qd_ref[0] = _dot(h, w_ref[:, 3 * FOX_W:3 * FOX_W + DIFF_W])
    kvd_ref[0] = _dot(h, w_ref[:, 3 * FOX_W + DIFF_W:])
    logf = _log_sigmoid(_dot_nt(wft_ref[...], h) + bf_ref[...])
    lf_ref[0] = logf

    @pl.when(t == 0)
    def _():
        carry_scr[...] = jnp.zeros_like(carry_scr)

    tm = logf.shape[1]
    cum = _dot3_left(logf, _tri(tm, upper=True)) + carry_scr[:, 0:1]
    cum_ref[0] = cum
    carry_scr[...] = jnp.broadcast_to(cum[:, tm - 1:tm], carry_scr.shape)


def _attn_proj(x, g, w_main, wft, bf):
    b, t, d = x.shape
    tm = min(512, t)
    n_main = w_main.shape[1]
    row = lambda i, j: (i, j, 0)
    col = lambda i, j: (i, 0, j)
    const = lambda i, j: (0, 0)
    return pl.pallas_call(
        _attn_proj_kernel,
        out_shape=[
            jax.ShapeDtypeStruct((b, t, FOX_W), F32),
            jax.ShapeDtypeStruct((b, t, 2 * FOX_W), F32),
            jax.ShapeDtypeStruct((b, t, DIFF_W), F32),
            jax.ShapeDtypeStruct((b, t, 2 * DIFF_W), F32),
            jax.ShapeDtypeStruct((b, FOX_HEADS, t), F32),
            jax.ShapeDtypeStruct((b, FOX_HEADS, t), F32),
        ],
        grid=(b, t // tm),
        in_specs=[
            pl.BlockSpec((1, tm, d), row),
            pl.BlockSpec((1, d), const),
            pl.BlockSpec((d, n_main), const),
            pl.BlockSpec((FOX_HEADS, d), const),
            pl.BlockSpec((FOX_HEADS, 1), const),
        ],
        out_specs=[
            pl.BlockSpec((1, tm, FOX_W), row),
            pl.BlockSpec((1, tm, 2 * FOX_W), row),
            pl.BlockSpec((1, tm, DIFF_W), row),
            pl.BlockSpec((1, tm, 2 * DIFF_W), row),
            pl.BlockSpec((1, FOX_HEADS, tm), col),
            pl.BlockSpec((1, FOX_HEADS, tm), col),
        ],
        scratch_shapes=[pltpu.VMEM((FOX_HEADS, LANES), F32)],
        compiler_params=_params("parallel", "arbitrary"),
        name="attn_proj",
    )(x, g.reshape(1, d), w_main, wft, bf)


def _attn_prompt_kernel(*refs, fox, tq, lam_init):
    if fox:
        q_ref, k_ref, v_ref, cq_ref, ck_ref, o_ref, kb_scr, vb_scr, m_scr, acc_scr = refs
    else:
        q_ref, k_ref, v_ref, lam_ref, g_ref, o_ref, kb_scr, vb_scr, m_scr, acc_scr = refs
    p = pl.program_id(1)
    qi = pl.program_id(2)
    half = LANES // 2
    n_chunks = tq // LANES

    @pl.when(qi == 0)
    def _():
        kb_scr[...] = k_ref[0].astype(BF16)
        vb_scr[:, 0:LANES] = v_ref[0].astype(BF16)
        ones_lane = lax.broadcasted_iota(jnp.int32, (vb_scr.shape[0], LANES), 1) == 0
        vb_scr[:, LANES:] = jnp.where(ones_lane, 1.0, 0.0).astype(BF16)

    q = q_ref[0] * (FOX_DH ** -0.5 * LOG2E)
    lane = lax.broadcasted_iota(jnp.int32, q.shape, 1)
    qs = [jnp.where(lane < half, q, 0.0).astype(BF16), jnp.where(lane >= half, q, 0.0).astype(BF16)]
    if fox:
        cq = cq_ref[0]
        hl = lax.broadcasted_iota(jnp.int32, cq.shape, 1)
        fq = [jnp.sum(jnp.where(hl == 2 * p + c, cq, 0.0), axis=1, keepdims=True) * LOG2E for c in range(2)]
    row = lax.broadcasted_iota(jnp.int32, (tq, LANES), 0)
    m_scr[...] = jnp.full(m_scr.shape, NEG, F32)
    acc_scr[...] = jnp.zeros_like(acc_scr)

    def block(kb, masked):
        start = pl.multiple_of(kb * tq, tq)
        k = kb_scr[pl.ds(start, tq), :]
        v = vb_scr[pl.ds(start, tq), :]
        for c in range(2):
            s = _dot_nt(qs[c], k)
            chunks = [s[:, i * LANES:(i + 1) * LANES] for i in range(n_chunks)]
            if fox:
                fk = ck_ref[0, pl.ds(2 * p + c, 1), pl.ds(start, tq)] * LOG2E
                chunks = [ch - fk[:, i * LANES:(i + 1) * LANES] for i, ch in enumerate(chunks)]
            if masked:
                chunks = [jnp.where(lane + i * LANES <= row, ch, NEG) for i, ch in enumerate(chunks)]
            mx = functools.reduce(jnp.maximum, chunks)
            row_max = jnp.max(mx, axis=1, keepdims=True)
            if fox:
                row_max = row_max + fq[c]
            m_old = m_scr[c]
            m_new = jnp.maximum(m_old, row_max)
            shift = (m_new - fq[c]) if fox else m_new
            pr = jnp.concatenate([jnp.exp2(ch - shift).astype(BF16) for ch in chunks], axis=1)
            alpha = jnp.exp2(m_old - m_new)
            acc_scr[c] = jnp.concatenate([alpha, alpha], axis=1) * acc_scr[c] + _dot(pr, v)
            m_scr[c] = m_new

    def body(kb, carry):
        block(kb, False)
        return carry

    lax.fori_loop(0, qi, body, 0)
    block(qi, True)
    outs = []
    for c in range(2):
        acc = acc_scr[c]
        outs.append(acc[:, 0:LANES] / acc[:, LANES:LANES + 1])
    if fox:
        o_ref[0] = jnp.where(lane < half, outs[0], outs[1])
    else:
        lp = lam_ref[...]
        lam = (jnp.exp(jnp.sum(lp[0:1] * lp[1:2], axis=1, keepdims=True))
               - jnp.exp(jnp.sum(lp[2:3] * lp[3:4], axis=1, keepdims=True)) + lam_init)
        o = outs[0] - lam * outs[1]
        o_ref[0] = _rms(o, g_ref[...]) * (1.0 - lam_init)


def _attn_prompt(fox, q, kv, extra_a, extra_b, lam_init):
    b, t, w = q.shape
    groups = w // LANES
    tq = min(512, t)
    kernel = functools.partial(_attn_prompt_kernel, fox=fox, tq=tq, lam_init=lam_init)
    if fox:
        extra_specs = [pl.BlockSpec((1, tq, FOX_HEADS), lambda i, p, j: (i, j, 0)),
                       pl.BlockSpec((1, FOX_HEADS, t), lambda i, p, j: (i, 0, 0))]
    else:
        extra_specs = [pl.BlockSpec(extra_a.shape, lambda i, p, j: (0, 0)),
                       pl.BlockSpec(extra_b.shape, lambda i, p, j: (0, 0))]
    return pl.pallas_call(
        kernel,
        out_shape=jax.ShapeDtypeStruct((b, t, w), F32),
        grid=(b, groups, t // tq),
        in_specs=[
            pl.BlockSpec((1, tq, LANES), lambda i, p, j: (i, j, p)),
            pl.BlockSpec((1, t, LANES), lambda i, p, j: (i, 0, p)),
            pl.BlockSpec((1, t, LANES), lambda i, p, j: (i, 0, groups + p)),
        ] + extra_specs,
        out_specs=pl.BlockSpec((1, tq, LANES), lambda i, p, j: (i, j, p)),
        scratch_shapes=[pltpu.VMEM((t, LANES), BF16), pltpu.VMEM((t, 2 * LANES), BF16),
                        pltpu.VMEM((2, tq, LANES), F32), pltpu.VMEM((2, tq, 2 * LANES), F32)],
        compiler_params=_params("parallel", "parallel", "arbitrary"),
        name="fox_prompt" if fox else "diff_prompt",
    )(q, kv, kv, extra_a, extra_b)


_PAGES_PER_STEP = 8


def _decode_kernel(pt_ref, qf_ref, qd_ref, nkf_ref, nkd_ref, nlf_ref, lam_ref, g_ref, sfx_ref, *rest,
                   pps, lam_init):
    del pt_ref
    page_refs = rest[:3 * pps]
    of_ref, od_ref = rest[3 * pps:3 * pps + 2]
    mf_scr, lf_scr, af_scr, md_scr, ld_scr, ad_scr, carry_scr = rest[3 * pps + 2:]
    b = pl.program_id(0)
    j = pl.program_id(1)
    scale = FOX_DH ** -0.5
    dj = 2 * DIFF_HEADS
    hd = 2 * DIFF_DH

    q_row = qf_ref[0]
    rf = lax.broadcasted_iota(jnp.int32, (FOX_HEADS, FOX_W), 0)
    lf_ = lax.broadcasted_iota(jnp.int32, (FOX_HEADS, FOX_W), 1)
    own = (lf_ // FOX_DH) == rf
    qblk = jnp.where(own, jnp.broadcast_to(q_row, (FOX_HEADS, FOX_W)), 0.0)
    qblk_b = qblk.astype(BF16)

    qd_row = qd_ref[0]
    r8 = lax.broadcasted_iota(jnp.int32, (dj, hd), 0)
    l8 = lax.broadcasted_iota(jnp.int32, (dj, hd), 1)
    head_of_row = r8 % DIFF_HEADS
    in_map = (l8 // DIFF_DH) == (r8 // DIFF_HEADS)

    def rows_from(vec, offset):
        out = jnp.zeros((dj, hd), F32)
        for h in range(DIFF_HEADS):
            piece = jnp.broadcast_to(vec[:, offset + h * hd:offset + (h + 1) * hd], (dj, hd))
            out = jnp.where(head_of_row == h, piece, out)
        return out

    q8 = jnp.where(in_map, rows_from(qd_row, 0), 0.0)
    q8_h = [jnp.where(head_of_row == h, q8, 0.0).astype(BF16) for h in range(DIFF_HEADS)]

    @pl.when(j == 0)
    def _():
        kvn = nkf_ref[0]
        kn = jnp.broadcast_to(kvn[:, 0:FOX_W], (FOX_HEADS, FOX_W))
        mf_scr[...] = jnp.broadcast_to(jnp.sum(qblk * kn, axis=1, keepdims=True) * scale, mf_scr.shape)
        lf_scr[...] = jnp.ones_like(lf_scr)
        af_scr[...] = jnp.broadcast_to(kvn[:, FOX_W:], (FOX_HEADS, FOX_W))
        kvd = nkd_ref[0]
        md_scr[...] = jnp.broadcast_to(
            jnp.sum(q8 * rows_from(kvd, 0), axis=1, keepdims=True) * scale, md_scr.shape)
        ld_scr[...] = jnp.ones_like(ld_scr)
        ad_scr[...] = rows_from(kvd, DIFF_W)
        nl = nlf_ref[...]
        sl = lax.broadcasted_iota(jnp.int32, nl.shape, 1)
        carry_scr[...] = jnp.broadcast_to(
            jnp.sum(jnp.where(sl == b, nl, 0.0), axis=1, keepdims=True), carry_scr.shape)

    carry = carry_scr[:, 0:1]
    s_f = []
    for i in range(pps):
        kt = page_refs[3 * i][0, 0].astype(BF16)
        lt = page_refs[3 * i + 1][0]
        hi, mid, lo = _split3(lt)
        r3 = _dot(jnp.concatenate([hi, mid, lo], axis=0), sfx_ref[...])
        bias = carry + r3[0:8] + r3[8:16] + r3[16:24]
        s_f.append(_dot(qblk_b, kt) * scale + bias)
        carry = carry + jnp.sum(lt, axis=1, keepdims=True)
    carry_scr[...] = jnp.broadcast_to(carry, carry_scr.shape)
    m_old = mf_scr[:, 0:1]
    m_new = jnp.maximum(m_old, jnp.max(functools.reduce(jnp.maximum, s_f), axis=1, keepdims=True))
    alpha = jnp.exp(m_old - m_new)
    l_add = jnp.zeros((FOX_HEADS, 1), F32)
    pv = jnp.zeros((FOX_HEADS, FOX_W), F32)
    for i in range(pps):
        pr = jnp.exp(s_f[i] - m_new)
        l_add = l_add + jnp.sum(pr, axis=1, keepdims=True)
        vt = page_refs[3 * i][0, 1].astype(BF16)
        pv = pv + _dot_nt(pr.astype(BF16), vt)
    lf_scr[...] = alpha * lf_scr[...] + l_add
    af_scr[...] = alpha * af_scr[...] + pv
    mf_scr[...] = jnp.broadcast_to(m_new, mf_scr.shape)

    page = page_refs[2].shape[1] // dj
    s_d = []
    for i in range(pps):
        xd = page_refs[3 * i + 2]
        s = jnp.zeros((dj, page), F32)
        for h in range(DIFF_HEADS):
            k_h = xd[0, pl.ds(h, page, stride=dj), :].astype(BF16)
            s = s + _dot_nt(q8_h[h], k_h)
        s_d.append(s * scale)
    md_old = md_scr[:, 0:1]
    md_new = jnp.maximum(md_old, jnp.max(functools.reduce(jnp.maximum, s_d), axis=1, keepdims=True))
    alphad = jnp.exp(md_old - md_new)
    ld_add = jnp.zeros((dj, 1), F32)
    pvd = jnp.zeros((dj, hd), F32)
    for i in range(pps):
        xd = page_refs[3 * i + 2]
        pr = jnp.exp(s_d[i] - md_new)
        ld_add = ld_add + jnp.sum(pr, axis=1, keepdims=True)
        for h in range(DIFF_HEADS):
            v_h = xd[0, pl.ds(DIFF_HEADS + h, page, stride=dj), :].astype(BF16)
            pvd = pvd + _dot(jnp.where(head_of_row == h, pr, 0.0).astype(BF16), v_h)
    ld_scr[...] = alphad * ld_scr[...] + ld_add
    ad_scr[...] = alphad * ad_scr[...] + pvd
    md_scr[...] = jnp.broadcast_to(md_new, md_scr.shape)

    @pl.when(j == pl.num_programs(1) - 1)
    def _():
        o_all = af_scr[...] / lf_scr[:, 0:1]
        of_ref[0] = jnp.sum(jnp.where(own, o_all, 0.0), axis=0, keepdims=True)
        od_all = ad_scr[...] / ld_scr[:, 0:1]
        lp = lam_ref[...]
        lam = (jnp.exp(jnp.sum(lp[0:1] * lp[1:2], axis=1, keepdims=True))
               - jnp.exp(jnp.sum(lp[2:3] * lp[3:4], axis=1, keepdims=True)) + lam_init)
        o = od_all[0:DIFF_HEADS] - lam * od_all[DIFF_HEADS:]
        od_ref[0] = _rms(o, g_ref[...]) * (1.0 - lam_init)


def _decode(page_table, qf, qd, new_kvf, new_kvd, new_lft, lam_p, subln_g, cache_ft, cache_lt, cache_d, lam_init):
    nb, n_pages = page_table.shape
    page = cache_lt.shape[2]
    pps = math.gcd(_PAGES_PER_STEP, n_pages)
    dj, hd = 2 * DIFF_HEADS, 2 * DIFF_DH
    kk = lax.broadcasted_iota(jnp.int32, (page, page), 0)
    kc = lax.broadcasted_iota(jnp.int32, (page, page), 1)
    sfx = jnp.where(kk > kc, 1.0, 0.0).astype(BF16)

    def page_idx(i):
        return lambda s, j, pt: pt[s, n_pages - 1 - (j * pps + i)]

    page_specs = []
    page_args = []
    for i in range(pps):
        pick = page_idx(i)
        page_specs += [
            pl.BlockSpec((1, 2, FOX_W, page), lambda s, j, pt, pick=pick: (pick(s, j, pt), 0, 0, 0)),
            pl.BlockSpec((1, FOX_HEADS, page), lambda s, j, pt, pick=pick: (pick(s, j, pt), 0, 0)),
            pl.BlockSpec((1, page * dj, hd), lambda s, j, pt, pick=pick: (pick(s, j, pt), 0, 0)),
        ]
        page_args += [cache_ft, cache_lt, cache_d]
    samp = lambda s, j, pt: (s, 0, 0)
    const = lambda s, j, pt: (0, 0)
    grid_spec = pltpu.PrefetchScalarGridSpec(
        num_scalar_prefetch=1,
        grid=(nb, n_pages // pps),
        in_specs=[
            pl.BlockSpec((1, 1, FOX_W), samp),
            pl.BlockSpec((1, 1, DIFF_W), samp),
            pl.BlockSpec((1, 1, 2 * FOX_W), samp),
            pl.BlockSpec((1, 1, 2 * DIFF_W), samp),
            pl.BlockSpec(new_lft.shape, const),
            pl.BlockSpec(lam_p.shape, const),
            pl.BlockSpec(subln_g.shape, const),
            pl.BlockSpec(sfx.shape, const),
        ] + page_specs,
        out_specs=[
            pl.BlockSpec((1, 1, FOX_W), samp),
            pl.BlockSpec((1, DIFF_HEADS, hd), samp),
        ],
        scratch_shapes=[
            pltpu.VMEM((FOX_HEADS, LANES), F32), pltpu.VMEM((FOX_HEADS, LANES), F32),
            pltpu.VMEM((FOX_HEADS, FOX_W), F32),
            pltpu.VMEM((dj, LANES), F32), pltpu.VMEM((dj, LANES), F32),
            pltpu.VMEM((dj, hd), F32),
            pltpu.VMEM((FOX_HEADS, LANES), F32),
        ],
    )
    return pl.pallas_call(
        functools.partial(_decode_kernel, pps=pps, lam_init=lam_init),
        out_shape=[jax.ShapeDtypeStruct((nb, 1, FOX_W), F32),
                   jax.ShapeDtypeStruct((nb, DIFF_HEADS, hd), F32)],
        grid_spec=grid_spec,
        compiler_params=_params("parallel", "arbitrary"),
        name="decode_attn",
    )(page_table, qf, qd, new_kvf, new_kvd, new_lft, lam_p, subln_g, sfx, *page_args)


def _rec_proj_kernel(x_ref, g_ref, w_ref, wgt_ref, bg_ref, q_ref, k_ref, v_ref, og_ref, u_ref, gt_ref):
    h = _rms(x_ref[0], g_ref[...]).astype(BF16)
    c0 = 0
    q_ref[0] = _dot(h, w_ref[:, c0:c0 + MLSTM_QK]) * (MLSTM_DK ** -0.5)
    c0 += MLSTM_QK
    k_ref[0] = _dot(h, w_ref[:, c0:c0 + MLSTM_QK])
    c0 += MLSTM_QK
    v_ref[0] = _dot(h, w_ref[:, c0:c0 + MLSTM_V])
    c0 += MLSTM_V
    og_ref[0] = _dot(h, w_ref[:, c0:c0 + MLSTM_V])
    c0 += MLSTM_V
    ua = _dot(h, w_ref[:, c0:c0 + CONV_CH])
    c0 += CONV_CH
    ub = _dot(h, w_ref[:, c0:c0 + CONV_CH])
    u_ref[0] = ua * jax.nn.sigmoid(ub)
    z = _dot_nt(wgt_ref[...], h) + bg_ref[...]
    rowi = lax.broadcasted_iota(jnp.int32, z.shape, 0)
    gt_ref[0] = jnp.where(rowi < MLSTM_HEADS, z, _log_sigmoid(z))


def _rec_proj(x, g, w_main, wgt, bg):
    b, t, d = x.shape
    tm = min(512, t)
    row = lambda i, j: (i, j, 0)
    const = lambda i, j: (0, 0)
    widths = [MLSTM_QK, MLSTM_QK, MLSTM_V, MLSTM_V, CONV_CH]
    return pl.pallas_call(
        _rec_proj_kernel,
        out_shape=[jax.ShapeDtypeStruct((b, t, w), F32) for w in widths]
        + [jax.ShapeDtypeStruct((b, 2 * MLSTM_HEADS, t), F32)],
        grid=(b, t // tm),
        in_specs=[
            pl.BlockSpec((1, tm, d), row),
            pl.BlockSpec((1, d), const),
            pl.BlockSpec(w_main.shape, const),
            pl.BlockSpec(wgt.shape, const),
            pl.BlockSpec(bg.shape, const),
        ],
        out_specs=[pl.BlockSpec((1, tm, w), row) for w in widths]
        + [pl.BlockSpec((1, 2 * MLSTM_HEADS, tm), lambda i, j: (i, 0, j))],
        compiler_params=_params("parallel", "parallel"),
        name="rec_proj",
    )(x, g.reshape(1, d), w_main, wgt, bg)


def _mlstm_prompt_kernel(q_ref, k_ref, v_ref, og_ref, gt_ref, gc_ref, ng_ref,
                         hm_ref, c_ref, n_ref, m_ref, ct_scr, n_scr, m_scr, *, chunk):
    t = pl.program_id(1)
    L = chunk
    half = LANES // 2

    @pl.when(t == 0)
    def _():
        ct_scr[...] = jnp.zeros_like(ct_scr)
        n_scr[...] = jnp.zeros_like(n_scr)
        m_scr[...] = jnp.full(m_scr.shape, NEG, F32)

    gt = gt_ref[0]
    gc = gc_ref[0]
    cum_rows = _dot3_left(gt, _tri(L, upper=True))
    cum_cols = _dot3_right(_tri(L, upper=False), gc)
    rr = lax.broadcasted_iota(jnp.int32, (L, L), 0)
    cc = lax.broadcasted_iota(jnp.int32, (L, L), 1)
    tri = cc <= rr
    lane = lax.broadcasted_iota(jnp.int32, (L, LANES), 1)

    for pair in range(MLSTM_HEADS // 2):
        qp = q_ref[0, :, pair * LANES:(pair + 1) * LANES]
        kp = k_ref[0, :, pair * LANES:(pair + 1) * LANES]
        kp_b = kp.astype(BF16)
        kt_b = kp.T.astype(BF16)
        ct = ct_scr[pair]
        n_row = n_scr[pair]
        ct_new = []
        n_new = []
        for c in range(2):
            h = 2 * pair + c
            mine = (lane < half) if c == 0 else (lane >= half)
            qm = jnp.where(mine, qp, 0.0)
            qm_b = qm.astype(BF16)
            vh = v_ref[0, :, h * MLSTM_DV:(h + 1) * MLSTM_DV]
            a_col = cum_cols[:, MLSTM_HEADS + h:MLSTM_HEADS + h + 1]
            ig_col = gc[:, h:h + 1]
            b_row = gt[h:h + 1, :] - cum_rows[MLSTM_HEADS + h:MLSTM_HEADS + h + 1, :]
            m_prev = m_scr[h:h + 1, 0:1]
            d = jnp.where(tri, a_col + b_row, NEG)
            m_inter = a_col + m_prev
            m_t = jnp.maximum(jnp.max(d, axis=1, keepdims=True), m_inter)
            s = _dot_nt(qm_b, kp_b) * jnp.exp(d - m_t)
            w_inter = jnp.exp(m_inter - m_t)
            num = _dot(s.astype(BF16), vh.astype(BF16)) + w_inter * _dot(qm_b, ct.astype(BF16))
            den = jnp.sum(s, axis=1, keepdims=True) + w_inter * jnp.sum(qm * n_row, axis=1, keepdims=True)
            hout = num / jnp.maximum(jnp.abs(den), jnp.exp(-m_t))
            gh = ng_ref[:, h * MLSTM_DV:(h + 1) * MLSTM_DV]
            oh = og_ref[0, :, h * MLSTM_DV:(h + 1) * MLSTM_DV]
            hm_ref[0, :, h * MLSTM_DV:(h + 1) * MLSTM_DV] = _rms(hout, gh) * jax.nn.sigmoid(oh)
            m_new = m_t[L - 1:L, :]
            cum_last = a_col[L - 1:L, :]
            decay = jnp.exp(cum_last + m_prev - m_new)
            w_key = jnp.exp(cum_last - a_col + ig_col - m_new)
            upd = _dot(kt_b, (w_key * vh).astype(BF16))
            lo, hi_ = c * half, (c + 1) * half
            ct_new.append(decay * ct[lo:hi_] + upd[lo:hi_])
            n_new.append(decay * n_row + jnp.sum(w_key * kp, axis=0, keepdims=True))
            m_scr[h:h + 1, :] = jnp.broadcast_to(m_new, (1, LANES))
        ct_scr[pair] = jnp.concatenate(ct_new, axis=0)
        lane1 = lax.broadcasted_iota(jnp.int32, (1, LANES), 1)
        n_scr[pair] = jnp.where(lane1 < half, n_new[0], n_new[1])

    @pl.when(t == pl.num_programs(1) - 1)
    def _():
        for pair in range(MLSTM_HEADS // 2):
            c_pair = ct_scr[pair].T
            n_row = n_scr[pair]
            for c in range(2):
                h = 2 * pair + c
                c_ref[0, h] = c_pair[:, c * half:(c + 1) * half]
                n_ref[0, h:h + 1, :] = n_row[:, c * half:(c + 1) * half]
                m_ref[0, :, h:h + 1] = m_scr[h:h + 1, 0:1]


def _mlstm_prompt(q, k, v, og, gt, gc, norm_g):
    b, t, _ = q.shape
    chunk = min(256, t)
    row = lambda i, j: (i, j, 0)
    const = lambda i, j: (0, 0)
    return pl.pallas_call(
        functools.partial(_mlstm_prompt_kernel, chunk=chunk),
        out_shape=[
            jax.ShapeDtypeStruct((b, t, MLSTM_V), F32),
            jax.ShapeDtypeStruct((b, MLSTM_HEADS, MLSTM_DV, MLSTM_DK), F32),
            jax.ShapeDtypeStruct((b, MLSTM_HEADS, MLSTM_DK), F32),
            jax.ShapeDtypeStruct((b, 1, MLSTM_HEADS), F32),
        ],
        grid=(b, t // chunk),
        in_specs=[
            pl.BlockSpec((1, chunk, MLSTM_QK), row),
            pl.BlockSpec((1, chunk, MLSTM_QK), row),
            pl.BlockSpec((1, chunk, MLSTM_V), row),
            pl.BlockSpec((1, chunk, MLSTM_V), row),
            pl.BlockSpec((1, 2 * MLSTM_HEADS, chunk), lambda i, j: (i, 0, j)),
            pl.BlockSpec((1, chunk, 2 * MLSTM_HEADS), row),
            pl.BlockSpec((1, MLSTM_V), const),
        ],
        out_specs=[
            pl.BlockSpec((1, chunk, MLSTM_V), row),
            pl.BlockSpec((1, MLSTM_HEADS, MLSTM_DV, MLSTM_DK), lambda i, j: (i, 0, 0, 0)),
            pl.BlockSpec((1, MLSTM_HEADS, MLSTM_DK), lambda i, j: (i, 0, 0)),
            pl.BlockSpec((1, 1, MLSTM_HEADS), lambda i, j: (i, 0, 0)),
        ],
        scratch_shapes=[
            pltpu.VMEM((MLSTM_HEADS // 2, LANES, MLSTM_DV), F32),
            pltpu.VMEM((MLSTM_HEADS // 2, 1, LANES), F32),
            pltpu.VMEM((SUBLANES, LANES), F32),
        ],
        compiler_params=_params("parallel", "arbitrary"),
        name="mlstm_prompt",
    )(q, k, v, og, gt, gc, norm_g.reshape(1, MLSTM_V))


def _mlstm_sample_kernel(q_ref, k_ref, v_ref, og_ref, gc_ref, m_ref, kall_ref, vall_ref, ng_ref, c_ref, n_ref,
                         hm_ref, co_ref, no_ref, mo_ref, vt_scr, *, bs):
    i = pl.program_id(0)
    nb = kall_ref.shape[0]

    @pl.when(i == 0)
    def _():
        for h in range(MLSTM_HEADS):
            vt_scr[h] = vall_ref[:, h * MLSTM_DV:(h + 1) * MLSTM_DV].T

    rows = lax.broadcasted_iota(jnp.int32, (nb, MLSTM_DK), 0)

    def body(bl, _):
        b = i * bs + bl
        q_row = q_ref[bl]
        k_row = k_ref[bl]
        v_row = v_ref[bl]
        og_row = og_ref[bl]
        g_row = gc_ref[bl]
        m_row = m_ref[bl]
        for h in range(MLSTM_HEADS):
            qh = q_row[:, h * MLSTM_DK:(h + 1) * MLSTM_DK]
            kh = k_row[:, h * MLSTM_DK:(h + 1) * MLSTM_DK]
            vh = v_row[:, h * MLSTM_DV:(h + 1) * MLSTM_DV]
            it = g_row[:, h:h + 1]
            lf = g_row[:, MLSTM_HEADS + h:MLSTM_HEADS + h + 1]
            m_prev = m_row[:, h:h + 1]
            m_inter = lf + m_prev
            m_t = jnp.maximum(it, m_inter)
            e_i = jnp.exp(it - m_t)
            w_inter = jnp.exp(m_inter - m_t)
            s = jnp.sum(qh * kh, axis=1, keepdims=True) * e_i
            cm = c_ref[bl, h]
            nh = n_ref[bl, h:h + 1, :]
            num = s * vh + w_inter * _dot_nt(qh.astype(BF16), cm.astype(BF16))
            den = s + w_inter * jnp.sum(nh * qh, axis=1, keepdims=True)
            hrow = num / jnp.maximum(jnp.abs(den), jnp.exp(-m_t))
            gh = ng_ref[:, h * MLSTM_DV:(h + 1) * MLSTM_DV]
            oh = og_row[:, h * MLSTM_DV:(h + 1) * MLSTM_DV]
            hm_ref[bl, :, h * MLSTM_DV:(h + 1) * MLSTM_DV] = _rms(hrow, gh) * jax.nn.sigmoid(oh)
            k_all = kall_ref[:, h * MLSTM_DK:(h + 1) * MLSTM_DK]
            k_sel = jnp.where(rows == b, k_all, 0.0).astype(BF16)
            outer = _dot(vt_scr[h].astype(BF16), k_sel)
            co_ref[bl, h] = w_inter * cm + e_i * outer
            no_ref[bl, h:h + 1, :] = w_inter * nh + e_i * kh
            mo_ref[bl, :, h:h + 1] = m_t
        return 0

    lax.fori_loop(0, bs, body, 0)


def _mlstm_sample(q, k, v, og, gc, norm_g, c0, n0, m0):
    nb = q.shape[0]
    bs = min(16, nb)
    full = lambda i: (0, 0)
    rows = lambda i: (i, 0, 0)
    per_sample = [a.reshape(nb, 1, a.shape[1]) for a in (q, k, v, og, gc, m0)]
    hm, c1, n1, m1 = pl.pallas_call(
        functools.partial(_mlstm_sample_kernel, bs=bs),
        out_shape=[
            jax.ShapeDtypeStruct((nb, 1, MLSTM_V), F32),
            jax.ShapeDtypeStruct(c0.shape, F32),
            jax.ShapeDtypeStruct(n0.shape, F32),
            jax.ShapeDtypeStruct((nb, 1, MLSTM_HEADS), F32),
        ],
        grid=(nb // bs,),
        in_specs=[pl.BlockSpec((bs, 1, a.shape[2]), rows) for a in per_sample] + [
            pl.BlockSpec(k.shape, full),
            pl.BlockSpec(v.shape, full),
            pl.BlockSpec((1, MLSTM_V), full),
            pl.BlockSpec((bs, MLSTM_HEADS, MLSTM_DV, MLSTM_DK), lambda i: (i, 0, 0, 0)),
            pl.BlockSpec((bs, MLSTM_HEADS, MLSTM_DK), rows),
        ],
        out_specs=[
            pl.BlockSpec((bs, 1, MLSTM_V), rows),
            pl.BlockSpec((bs, MLSTM_HEADS, MLSTM_DV, MLSTM_DK), lambda i: (i, 0, 0, 0)),
            pl.BlockSpec((bs, MLSTM_HEADS, MLSTM_DK), rows),
            pl.BlockSpec((bs, 1, MLSTM_HEADS), rows),
        ],
        scratch_shapes=[pltpu.VMEM((MLSTM_HEADS, MLSTM_DV, nb), F32)],
        compiler_params=_params("arbitrary"),
        name="mlstm_sample",
    )(*per_sample, k, v, norm_g.reshape(1, MLSTM_V), c0, n0)
    return hm.reshape(nb, MLSTM_V), c1, n1, m1.reshape(nb, MLSTM_HEADS)


_CONV_HALO = 32
_CONV_ROWS = 32


def _ln_silu(y, g, b):
    yc = y - jnp.mean(y, axis=-1, keepdims=True)
    var = jnp.mean(yc * yc, axis=-1, keepdims=True)
    z = yc * lax.rsqrt(var + NORM_EPS) * g + b
    return z * jax.nn.sigmoid(z)


def _conv_prompt_kernel(u_ref, prev_ref, w_ref, b_ref, g_ref, be_ref, c_ref, st_ref, full_scr, *, tt):
    t = pl.program_id(1)
    pad = _CONV_HALO - (CONV_W - 1)

    @pl.when(t == 0)
    def _():
        full_scr[0:_CONV_HALO, :] = jnp.zeros((_CONV_HALO, CONV_CH), F32)
        full_scr[pad:_CONV_HALO, :] = prev_ref[0]

    full_scr[_CONV_HALO:, :] = u_ref[0]
    for r0 in range(0, tt, _CONV_ROWS):
        acc = jnp.broadcast_to(b_ref[...], (_CONV_ROWS, CONV_CH))
        for k in range(CONV_W):
            acc = acc + w_ref[k:k + 1, :] * full_scr[r0 + pad + k:r0 + pad + k + _CONV_ROWS, :]
        c_ref[0, r0:r0 + _CONV_ROWS, :] = _ln_silu(acc, g_ref[...], be_ref[...])

    @pl.when(t == pl.num_programs(1) - 1)
    def _():
        st_ref[0] = full_scr[_CONV_HALO + tt - (CONV_W - 1):, :]

    full_scr[0:_CONV_HALO, :] = full_scr[tt:tt + _CONV_HALO, :]


def _conv_prompt(u, prev, w, bias, ln_g, ln_b):
    b, t, ch = u.shape
    tt = min(256, t)
    const = lambda i, j: (0, 0)
    return pl.pallas_call(
        functools.partial(_conv_prompt_kernel, tt=tt),
        out_shape=[jax.ShapeDtypeStruct((b, t, ch), F32),
                   jax.ShapeDtypeStruct((b, CONV_W - 1, ch), F32)],
        grid=(b, t // tt),
        in_specs=[
            pl.BlockSpec((1, tt, ch), lambda i, j: (i, j, 0)),
            pl.BlockSpec((1, CONV_W - 1, ch), lambda i, j: (i, 0, 0)),
            pl.BlockSpec((CONV_W, ch), const),
            pl.BlockSpec((1, ch), const),
            pl.BlockSpec((1, ch), const),
            pl.BlockSpec((1, ch), const),
        ],
        out_specs=[pl.BlockSpec((1, tt, ch), lambda i, j: (i, j, 0)),
                   pl.BlockSpec((1, CONV_W - 1, ch), lambda i, j: (i, 0, 0))],
        scratch_shapes=[pltpu.VMEM((_CONV_HALO + tt, ch), F32)],
        compiler_params=_params("parallel", "arbitrary"),
        name="conv_prompt",
    )(u, prev, w, bias.reshape(1, ch), ln_g.reshape(1, ch), ln_b.reshape(1, ch))


def _conv_sample_kernel(u_ref, prev_ref, w_ref, b_ref, g_ref, be_ref, c_ref, st_ref, *, bs):
    hist = CONV_W - 1

    def body(s, _):
        prev = prev_ref[s]
        u_row = u_ref[s]
        y = (jnp.sum(prev * w_ref[0:hist, :], axis=0, keepdims=True)
             + w_ref[hist:hist + 1, :] * u_row + b_ref[...])
        c_ref[s] = _ln_silu(y, g_ref[...], be_ref[...])
        st_ref[s, 0:hist - 1, :] = prev[1:hist, :]
        st_ref[s, hist - 1:hist, :] = u_row
        return 0

    lax.fori_loop(0, bs, body, 0)


def _conv_sample(u, prev, w, bias, ln_g, ln_b):
    nb, ch = u.shape
    bs = min(32, nb)
    const = lambda i: (0, 0)
    rows = lambda i: (i, 0, 0)
    c, st = pl.pallas_call(
        functools.partial(_conv_sample_kernel, bs=bs),
        out_shape=[jax.ShapeDtypeStruct((nb, 1, ch), F32),
                   jax.ShapeDtypeStruct((nb, CONV_W - 1, ch), F32)],
        grid=(nb // bs,),
        in_specs=[
            pl.BlockSpec((bs, 1, ch), rows),
            pl.BlockSpec((bs, CONV_W - 1, ch), rows),
            pl.BlockSpec((CONV_W, ch), const),
            pl.BlockSpec((1, ch), const),
            pl.BlockSpec((1, ch), const),
            pl.BlockSpec((1, ch), const),
        ],
        out_specs=[pl.BlockSpec((bs, 1, ch), rows),
                   pl.BlockSpec((bs, CONV_W - 1, ch), rows)],
        compiler_params=_params("parallel"),
        name="conv_sample",
    )(u.reshape(nb, 1, ch), prev, w, bias.reshape(1, ch), ln_g.reshape(1, ch), ln_b.reshape(1, ch))
    return c.reshape(nb, ch), st


def kernel(x_prompt, x_sample, cache_fox_kv, cache_fox_logf, cache_diff_kv, state_mlstm_C, state_mlstm_n,
           state_mlstm_m, state_conv, page_table, norm_g, final_g, ffn_w_in, ffn_w_out, attn_w_in, attn_b_f,
           diff_lam, diff_subln_g, attn_w_out, rec_w_in, rec_b_i, rec_b_f, mlstm_norm_g, conv_w, conv_b,
           conv_ln_g, conv_ln_b, rec_w_out):
    bp, t, d = x_prompt.shape
    nb = x_sample.shape[0]
    depth = norm_g.shape[0]
    n_pool, page = cache_fox_kv.shape[1], cache_fox_kv.shape[2]
    xp = x_prompt.reshape(bp * t, d)
    xs = x_sample.reshape(nb, d)
    w_in_b = ffn_w_in.astype(BF16)
    w_out_b = ffn_w_out.astype(BF16)

    outs = {k: [] for k in ("fkv_p", "fkv_s", "flf_p", "flf_s", "dkv_p", "dkv_s",
                            "c_p", "c_s", "n_p", "n_s", "m_p", "m_s", "cv_p", "cv_s")}
    for l in range(depth):
        j = l // 2
        xp = _ffn(xp, norm_g[l, 0], w_in_b[l, 0], w_out_b[l, 0])
        xs = _ffn(xs, norm_g[l, 0], w_in_b[l, 0], w_out_b[l, 0])
        if l % 2 == 0:
            lam_init = 0.8 - 0.6 * math.exp(-0.3 * l)
            w = attn_w_in[j]
            c1, c2 = 3 * FOX_W, 3 * FOX_W + FOX_HEADS
            w_main = jnp.concatenate([w[:, :c1], w[:, c2:]], axis=1).astype(BF16)
            wft = w[:, c1:c2].T.astype(BF16)
            bf = attn_b_f[j].reshape(FOX_HEADS, 1)
            w_o = attn_w_out[j].astype(BF16)
            qf, kvf, qd, kvd, lft, cumt = _attn_proj(xp.reshape(bp, t, d), norm_g[l, 1], w_main, wft, bf)
            o_f = _attn_prompt(True, qf, kvf, jnp.swapaxes(cumt, 1, 2), cumt, lam_init)
            o_d = _attn_prompt(False, qd, kvd, diff_lam[j], diff_subln_g[j].reshape(1, -1), lam_init)
            xp = _merge(xp, o_f.reshape(bp * t, FOX_W), o_d.reshape(bp * t, DIFF_W), w_o)
            outs["fkv_p"].append(kvf.reshape(bp, t, 2, FOX_HEADS, FOX_DH))
            outs["flf_p"].append(jnp.swapaxes(lft, 1, 2))
            outs["dkv_p"].append(kvd.reshape(bp, t, 2, DIFF_HEADS, 2 * DIFF_DH))
            sqf, skvf, sqd, skvd, slft, _ = _attn_proj(xs.reshape(1, nb, d), norm_g[l, 1], w_main, wft, bf)
            so_f, so_d = _decode(
                page_table,
                sqf.reshape(nb, 1, FOX_W), sqd.reshape(nb, 1, DIFF_W),
                skvf.reshape(nb, 1, 2 * FOX_W), skvd.reshape(nb, 1, 2 * DIFF_W),
                slft[0], diff_lam[j], diff_subln_g[j].reshape(1, -1),
                jnp.transpose(cache_fox_kv[j], (0, 2, 3, 4, 1)).reshape(n_pool, 2, FOX_W, page),
                jnp.swapaxes(cache_fox_logf[j], 1, 2),
                cache_diff_kv[j].reshape(n_pool, page * 2 * DIFF_HEADS, 2 * DIFF_DH),
                lam_init)
            xs = _merge(xs, so_f.reshape(nb, FOX_W), so_d.reshape(nb, DIFF_W), w_o)
            outs["fkv_s"].append(skvf.reshape(nb, 1, 2, FOX_HEADS, FOX_DH))
            outs["flf_s"].append(jnp.swapaxes(slft, 1, 2).reshape(nb, 1, FOX_HEADS))
            outs["dkv_s"].append(skvd.reshape(nb, 1, 2, DIFF_HEADS, 2 * DIFF_DH))
        else:
            w = rec_w_in[j]
            c1 = 2 * MLSTM_QK + MLSTM_V
            c2 = c1 + 2 * MLSTM_HEADS
            w_main = jnp.concatenate([w[:, :c1], w[:, c2:]], axis=1).astype(BF16)
            wgt = w[:, c1:c2].T.astype(BF16)
            bg = jnp.concatenate([rec_b_i[j], rec_b_f[j]]).reshape(2 * MLSTM_HEADS, 1)
            w_o = rec_w_out[j].astype(BF16)
            q, k, v, og, u, gt = _rec_proj(xp.reshape(bp, t, d), norm_g[l, 1], w_main, wgt, bg)
            hm, c_p, n_p, m_p = _mlstm_prompt(q, k, v, og, gt, jnp.swapaxes(gt, 1, 2), mlstm_norm_g[j])
            cv, st_p = _conv_prompt(u, jnp.zeros((bp, CONV_W - 1, CONV_CH), F32), conv_w[j], conv_b[j],
                                    conv_ln_g[j], conv_ln_b[j])
            xp = _merge(xp, hm.reshape(bp * t, MLSTM_V), cv.reshape(bp * t, CONV_CH), w_o)
            outs["c_p"].append(c_p)
            outs["n_p"].append(n_p)
            outs["m_p"].append(m_p.reshape(bp, MLSTM_HEADS))
            outs["cv_p"].append(st_p)
            sq, sk, sv, sog, su, sgt = _rec_proj(xs.reshape(1, nb, d), norm_g[l, 1], w_main, wgt, bg)
            shm, c_s, n_s, m_s = _mlstm_sample(sq[0], sk[0], sv[0], sog[0], sgt[0].T, mlstm_norm_g[j],
                                               state_mlstm_C[j], state_mlstm_n[j], state_mlstm_m[j])
            scv, st_s = _conv_sample(su[0], state_conv[j], conv_w[j], conv_b[j], conv_ln_g[j], conv_ln_b[j])
            xs = _merge(xs, shm, scv, w_o)
            outs["c_s"].append(c_s)
            outs["n_s"].append(n_s)
            outs["m_s"].append(m_s)
            outs["cv_s"].append(st_s)
        last = l == depth - 1
        xp = _ffn(xp, norm_g[l, 2], w_in_b[l, 1], w_out_b[l, 1], final_g if last else None)
        xs = _ffn(xs, norm_g[l, 2], w_in_b[l, 1], w_out_b[l, 1], final_g if last else None)

    st = jnp.stack
    return (xp.reshape(bp, t, d), xs.reshape(nb, 1, d),
            st(outs["fkv_p"]), st(outs["fkv_s"]), st(outs["flf_p"]), st(outs["flf_s"]),
            st(outs["dkv_p"]), st(outs["dkv_s"]),
            st(outs["c_p"]), st(outs["c_s"]), st(outs["n_p"]), st(outs["n_s"]),
            st(outs["m_p"]), st(outs["m_s"]), st(outs["cv_p"]), st(outs["cv_s"]))
```

```python
import functools
import math

import jax
import jax.numpy as jnp
from jax import lax
from jax.experimental import pallas as pl
from jax.experimental.pallas import tpu as pltpu

F32 = jnp.float32
BF16 = jnp.bfloat16
NORM_EPS = 1e-6
NEG = -1e30
LOG2E = 1.4426950408889634

FOX_HEADS = 8
FOX_DH = 64
DIFF_HEADS = 4
DIFF_DH = 64
MLSTM_HEADS = 4
MLSTM_DK = 64
MLSTM_DV = 128
CONV_CH = 512
CONV_W = 31
FOX_W = FOX_HEADS * FOX_DH
DIFF_W = DIFF_HEADS * 2 * DIFF_DH
MLSTM_QK = MLSTM_HEADS * MLSTM_DK
MLSTM_V = MLSTM_HEADS * MLSTM_DV

LANES = 128
SUBLANES = 8
VMEM_LIMIT_BYTES = 56 * 1024 * 1024

_NT = (((1,), (1,)), ((), ()))


def _params(*sem):
    return pltpu.CompilerParams(dimension_semantics=sem, vmem_limit_bytes=VMEM_LIMIT_BYTES)


def _rms(x, g):
    return x * lax.rsqrt(jnp.mean(x * x, axis=-1, keepdims=True) + NORM_EPS) * g


def _log_sigmoid(z):
    return jnp.minimum(z, 0.0) - jnp.log(1.0 + jnp.exp(-jnp.abs(z)))


def _split3(x):
    hi = x.astype(BF16)
    r = x - hi.astype(F32)
    mid = r.astype(BF16)
    lo = (r - mid.astype(F32)).astype(BF16)
    return hi, mid, lo


def _dot(a, b):
    return jnp.dot(a, b, preferred_element_type=F32)


def _dot_nt(a, b):
    return lax.dot_general(a, b, _NT, preferred_element_type=F32)


def _dot3_left(x, t):
    hi, mid, lo = _split3(x)
    return _dot(hi, t) + _dot(mid, t) + _dot(lo, t)


def _dot3_right(t, x):
    hi, mid, lo = _split3(x)
    return _dot(t, hi) + _dot(t, mid) + _dot(t, lo)


def _tri(n, upper):
    r = lax.broadcasted_iota(jnp.int32, (n, n), 0)
    c = lax.broadcasted_iota(jnp.int32, (n, n), 1)
    keep = (r <= c) if upper else (r >= c)
    return jnp.where(keep, 1.0, 0.0).astype(BF16)


def _ffn_kernel(x_ref, g_ref, wa_ref, wb_ref, wo_ref, fg_ref, o_ref, h_scr, gated_scr, *, final_norm):
    j = pl.program_id(1)
    tf = wa_ref.shape[1]

    @pl.when(j == 0)
    def _():
        h_scr[...] = _rms(x_ref[...], g_ref[...]).astype(BF16)

    h = h_scr[...]
    a = _dot(h, wa_ref[...])
    b = _dot(h, wb_ref[...])
    gated_scr[:, pl.ds(pl.multiple_of(j * tf, tf), tf)] = (a * jax.nn.sigmoid(a) * b).astype(BF16)

    @pl.when(j == pl.num_programs(1) - 1)
    def _():
        y = x_ref[...] + 0.5 * _dot(gated_scr[...], wo_ref[...])
        if final_norm:
            y = _rms(y, fg_ref[...])
        o_ref[...] = y


def _ffn(x, g, w_in, w_out, final_g=None):
    m, d = x.shape
    f = w_out.shape[0]
    tm = min(1024, m)
    tf = 256
    nf = f // tf
    fg = jnp.ones((1, d), F32) if final_g is None else final_g.reshape(1, d)
    return pl.pallas_call(
        functools.partial(_ffn_kernel, final_norm=final_g is not None),
        out_shape=jax.ShapeDtypeStruct((m, d), F32),
        grid=(m // tm, nf),
        in_specs=[
            pl.BlockSpec((tm, d), lambda i, j: (i, 0)),
            pl.BlockSpec((1, d), lambda i, j: (0, 0)),
            pl.BlockSpec((d, tf), lambda i, j: (0, j)),
            pl.BlockSpec((d, tf), lambda i, j: (0, j + nf)),
            pl.BlockSpec((f, d), lambda i, j: (0, 0)),
            pl.BlockSpec((1, d), lambda i, j: (0, 0)),
        ],
        out_specs=pl.BlockSpec((tm, d), lambda i, j: (i, 0)),
        scratch_shapes=[pltpu.VMEM((tm, d), BF16), pltpu.VMEM((tm, f), BF16)],
        compiler_params=_params("parallel", "arbitrary"),
        name="ffn",
    )(x, g.reshape(1, d), w_in, w_in, w_out, fg)


def _merge_kernel(x_ref, oa_ref, ob_ref, w_ref, o_ref):
    wa = oa_ref.shape[-1]
    y = _dot(oa_ref[...].astype(BF16), w_ref[:wa, :])
    y += _dot(ob_ref[...].astype(BF16), w_ref[wa:, :])
    o_ref[...] = x_ref[...] + y


def _merge(x, oa, ob, w_out):
    m, d = x.shape
    tm = min(1024, m)
    wa, wb = oa.shape[1], ob.shape[1]
    return pl.pallas_call(
        _merge_kernel,
        out_shape=jax.ShapeDtypeStruct((m, d), F32),
        grid=(m // tm,),
        in_specs=[
            pl.BlockSpec((tm, d), lambda i: (i, 0)),
            pl.BlockSpec((tm, wa), lambda i: (i, 0)),
            pl.BlockSpec((tm, wb), lambda i: (i, 0)),
            pl.BlockSpec((wa + wb, d), lambda i: (0, 0)),
        ],
        out_specs=pl.BlockSpec((tm, d), lambda i: (i, 0)),
        compiler_params=_params("parallel"),
        name="merge",
    )(x, oa, ob, w_out)


def _attn_proj_kernel(x_ref, g_ref, w_ref, wft_ref, bf_ref, qf_ref, kvf_ref, qd_ref, kvd_ref,
                      lf_ref, cum_ref, carry_scr):
    t = pl.program_id(1)
    h = _rms(x_ref[0], g_ref[...]).astype(BF16)
    qf_ref[0] = _dot(h, w_ref[:, 0:FOX_W])
    kvf_ref[0] = _dot(h, w_ref[:, FOX_W:3 * FOX_W])
    qd_ref[0] = _dot(h, w_ref[:, 3 * FOX_W:3 * FOX_W + DIFF_W])
    kvd_ref[0] = _dot(h, w_ref[:, 3 * FOX_W + DIFF_W:])
    logf = _log_sigmoid(_dot_nt(wft_ref[...], h) + bf_ref[...])
    lf_ref[0] = logf

    @pl.when(t == 0)
    def _():
        carry_scr[...] = jnp.zeros_like(carry_scr)

    tm = logf.shape[1]
    cum = _dot3_left(logf, _tri(tm, upper=True)) + carry_scr[:, 0:1]
    cum_ref[0] = cum
    carry_scr[...] = jnp.broadcast_to(cum[:, tm - 1:tm], carry_scr.shape)


def _attn_proj(x, g, w_main, wft, bf):
    b, t, d = x.shape
    tm = min(512, t)
    n_main = w_main.shape[1]
    row = lambda i, j: (i, j, 0)
    col = lambda i, j: (i, 0, j)
    const = lambda i, j: (0, 0)
    return pl.pallas_call(
        _attn_proj_kernel,
        out_shape=[
            jax.ShapeDtypeStruct((b, t, FOX_W), F32),
            jax.ShapeDtypeStruct((b, t, 2 * FOX_W), F32),
            jax.ShapeDtypeStruct((b, t, DIFF_W), F32),
            jax.ShapeDtypeStruct((b, t, 2 * DIFF_W), F32),
            jax.ShapeDtypeStruct((b, FOX_HEADS, t), F32),
            jax.ShapeDtypeStruct((b, FOX_HEADS, t), F32),
        ],
        grid=(b, t // tm),
        in_specs=[
            pl.BlockSpec((1, tm, d), row),
            pl.BlockSpec((1, d), const),
            pl.BlockSpec((d, n_main), const),
            pl.BlockSpec((FOX_HEADS, d), const),
            pl.BlockSpec((FOX_HEADS, 1), const),
        ],
        out_specs=[
            pl.BlockSpec((1, tm, FOX_W), row),
            pl.BlockSpec((1, tm, 2 * FOX_W), row),
            pl.BlockSpec((1, tm, DIFF_W), row),
            pl.BlockSpec((1, tm, 2 * DIFF_W), row),
            pl.BlockSpec((1, FOX_HEADS, tm), col),
            pl.BlockSpec((1, FOX_HEADS, tm), col),
        ],
        scratch_shapes=[pltpu.VMEM((FOX_HEADS, LANES), F32)],
        compiler_params=_params("parallel", "arbitrary"),
        name="attn_proj",
    )(x, g.reshape(1, d), w_main, wft, bf)


_V_ROWS = 144


def _attn_prompt_kernel(*refs, fox, tq, lam_init):
    if fox:
        q_ref, k_ref, v_ref, ccol_ref, crow_ref, o_ref, ka_scr, vt_scr, m_scr, acc_scr = refs
    else:
        q_ref, k_ref, v_ref, lam_ref, g_ref, o_ref, ka_scr, vt_scr, m_scr, acc_scr = refs
    p = pl.program_id(1)
    qi = pl.program_id(2)
    half = LANES // 2
    t_all = k_ref.shape[1]

    @pl.when(qi == 0)
    def _():
        ka_scr[:, 0:LANES] = k_ref[0].astype(BF16)
        if fox:
            cc = ccol_ref[0]
            hl = lax.broadcasted_iota(jnp.int32, cc.shape, 1)
            lane = lax.broadcasted_iota(jnp.int32, (t_all, LANES), 1)
            extra = jnp.zeros((t_all, LANES), F32)
            for c in range(2):
                fk = jnp.sum(jnp.where(hl == 2 * p + c, cc, 0.0), axis=1, keepdims=True) * LOG2E
                for i, piece in enumerate(_split3(fk)):
                    extra = jnp.where(lane == 3 * c + i, piece.astype(F32), extra)
            ka_scr[:, LANES:] = extra.astype(BF16)
        vt_scr[0:LANES, :] = v_ref[0].T.astype(BF16)
        r = lax.broadcasted_iota(jnp.int32, (_V_ROWS - LANES, t_all), 0)
        vt_scr[LANES:, :] = jnp.where(r == 0, 1.0, 0.0).astype(BF16)

    q = q_ref[0] * (FOX_DH ** -0.5 * LOG2E)
    lane = lax.broadcasted_iota(jnp.int32, q.shape, 1)
    members = []
    for c in range(2):
        x = jnp.where((lane < half) if c == 0 else (lane >= half), q, 0.0)
        if fox:
            pick = (lane >= 3 * c) & (lane < 3 * c + 3)
            x = jnp.concatenate([x, jnp.where(pick, -1.0, 0.0)], axis=1)
        members.append(x)
    qa = jnp.concatenate(members, axis=0).astype(BF16)
    if fox:
        fq = jnp.concatenate([crow_ref[0, pl.ds(2 * p + c, 1), :] for c in range(2)], axis=1) * LOG2E
    m_scr[...] = jnp.full(m_scr.shape, NEG, F32)
    acc_scr[...] = jnp.zeros_like(acc_scr)

    def scores(kb, masked):
        start = pl.multiple_of(kb * tq, tq)
        s = _dot_nt(ka_scr[pl.ds(start, tq), :], qa)
        if masked:
            krow = lax.broadcasted_iota(jnp.int32, (tq, tq), 0)
            qcol = lax.broadcasted_iota(jnp.int32, (tq, tq), 1)
            keep = krow <= qcol
            s = jnp.where(jnp.concatenate([keep, keep], axis=1), s, NEG)
        col_max = jnp.max(s, axis=0, keepdims=True)
        if fox:
            col_max = col_max + fq
        return start, s, col_max

    def accumulate(start, s, col_max):
        m_old = m_scr[...]
        m_new = jnp.maximum(m_old, col_max)
        shift = (m_new - fq) if fox else m_new
        pr = jnp.exp2(s - shift).astype(BF16)
        alpha = jnp.exp2(m_old - m_new)
        acc_scr[...] = alpha * acc_scr[...] + _dot(vt_scr[:, pl.ds(start, tq)], pr)
        m_scr[...] = m_new

    def pair(kb, last_masked):
        a = scores(kb, False)
        b = scores(kb + 1, last_masked)
        accumulate(*a)
        accumulate(*b)

    def body(i, carry):
        pair(2 * i, False)
        return carry

    lax.fori_loop(0, qi // 2, body, 0)

    @pl.when(qi % 2 == 1)
    def _():
        pair(qi - 1, True)

    @pl.when(qi % 2 == 0)
    def _():
        accumulate(*scores(qi, True))

    acc = acc_scr[...]
    outs_t = [acc[0:LANES, c * tq:(c + 1) * tq] / acc[LANES:LANES + 1, c * tq:(c + 1) * tq] for c in range(2)]
    if fox:
        o_ref[0] = jnp.concatenate([outs_t[0][:half], outs_t[1][half:]], axis=0).T
    else:
        lp = lam_ref[...]
        lam = (jnp.exp(jnp.sum(lp[0:1] * lp[1:2], axis=1, keepdims=True))
               - jnp.exp(jnp.sum(lp[2:3] * lp[3:4], axis=1, keepdims=True)) + lam_init)
        o = (outs_t[0] - lam * outs_t[1]).T
        o_ref[0] = _rms(o, g_ref[...]) * (1.0 - lam_init)


def _attn_prompt(fox, q, kv, extra_a, extra_b, lam_init):
    b, t, w = q.shape
    groups = w // LANES
    tq = min(512, t)
    kernel = functools.partial(_attn_prompt_kernel, fox=fox, tq=tq, lam_init=lam_init)
    if fox:
        extra_specs = [pl.BlockSpec((1, t, FOX_HEADS), lambda i, p, j: (i, 0, 0)),
                       pl.BlockSpec((1, FOX_HEADS, tq), lambda i, p, j: (i, 0, j))]
    else:
        extra_specs = [pl.BlockSpec(extra_a.shape, lambda i, p, j: (0, 0)),
                       pl.BlockSpec(extra_b.shape, lambda i, p, j: (0, 0))]
    return pl.pallas_call(
        kernel,
        out_shape=jax.ShapeDtypeStruct((b, t, w), F32),
        grid=(b, groups, t // tq),
        in_specs=[
            pl.BlockSpec((1, tq, LANES), lambda i, p, j: (i, j, p)),
            pl.BlockSpec((1, t, LANES), lambda i, p, j: (i, 0, p)),
            pl.BlockSpec((1, t, LANES), lambda i, p, j: (i, 0, groups + p)),
        ] + extra_specs,
        out_specs=pl.BlockSpec((1, tq, LANES), lambda i, p, j: (i, j, p)),
        scratch_shapes=[pltpu.VMEM((t, 2 * LANES if fox else LANES), BF16), pltpu.VMEM((_V_ROWS, t), BF16),
                        pltpu.VMEM((1, 2 * tq), F32), pltpu.VMEM((_V_ROWS, 2 * tq), F32)],
        compiler_params=_params("parallel", "parallel", "arbitrary"),
        name="fox_prompt" if fox else "diff_prompt",
    )(q, kv, kv, extra_a, extra_b)


_PAGES_PER_STEP = 8


def _decode_kernel(pt_ref, qf_ref, qd_ref, nkf_ref, nkd_ref, nlf_ref, lam_ref, g_ref, sfx_ref, *rest,
                   pps, lam_init):
    del pt_ref
    page_refs = rest[:3 * pps]
    of_ref, od_ref = rest[3 * pps:3 * pps + 2]
    mf_scr, lf_scr, af_scr, md_scr, ld_scr, ad_scr, carry_scr = rest[3 * pps + 2:]
    b = pl.program_id(0)
    j = pl.program_id(1)
    scale = FOX_DH ** -0.5
    dj = 2 * DIFF_HEADS
    hd = 2 * DIFF_DH

    q_row = qf_ref[0]
    rf = lax.broadcasted_iota(jnp.int32, (FOX_HEADS, FOX_W), 0)
    lf_ = lax.broadcasted_iota(jnp.int32, (FOX_HEADS, FOX_W), 1)
    own = (lf_ // FOX_DH) == rf
    qblk = jnp.where(own, jnp.broadcast_to(q_row, (FOX_HEADS, FOX_W)), 0.0)
    qblk_b = qblk.astype(BF16)

    qd_row = qd_ref[0]
    r8 = lax.broadcasted_iota(jnp.int32, (dj, hd), 0)
    l8 = lax.broadcasted_iota(jnp.int32, (dj, hd), 1)
    head_of_row = r8 % DIFF_HEADS
    in_map = (l8 // DIFF_DH) == (r8 // DIFF_HEADS)

    def rows_from(vec, offset):
        out = jnp.zeros((dj, hd), F32)
        for h in range(DIFF_HEADS):
            piece = jnp.broadcast_to(vec[:, offset + h * hd:offset + (h + 1) * hd], (dj, hd))
            out = jnp.where(head_of_row == h, piece, out)
        return out

    q8 = jnp.where(in_map, rows_from(qd_row, 0), 0.0)
    q8_h = [jnp.where(head_of_row == h, q8, 0.0).astype(BF16) for h in range(DIFF_HEADS)]

    @pl.when(j == 0)
    def _():
        kvn = nkf_ref[0]
        kn = jnp.broadcast_to(kvn[:, 0:FOX_W], (FOX_HEADS, FOX_W))
        mf_scr[...] = jnp.broadcast_to(jnp.sum(qblk * kn, axis=1, keepdims=True) * scale, mf_scr.shape)
        lf_scr[...] = jnp.ones_like(lf_scr)
        af_scr[...] = jnp.broadcast_to(kvn[:, FOX_W:], (FOX_HEADS, FOX_W))
        kvd = nkd_ref[0]
        md_scr[...] = jnp.broadcast_to(
            jnp.sum(q8 * rows_from(kvd, 0), axis=1, keepdims=True) * scale, md_scr.shape)
        ld_scr[...] = jnp.ones_like(ld_scr)
        ad_scr[...] = rows_from(kvd, DIFF_W)
        nl = nlf_ref[...]
        sl = lax.broadcasted_iota(jnp.int32, nl.shape, 1)
        carry_scr[...] = jnp.broadcast_to(
            jnp.sum(jnp.where(sl == b, nl, 0.0), axis=1, keepdims=True), carry_scr.shape)

    carry = carry_scr[:, 0:1]
    s_f = []
    for i in range(pps):
        kt = page_refs[3 * i][0, 0].astype(BF16)
        lt = page_refs[3 * i + 1][0]
        hi, mid, lo = _split3(lt)
        r3 = _dot(jnp.concatenate([hi, mid, lo], axis=0), sfx_ref[...])
        bias = carry + r3[0:8] + r3[8:16] + r3[16:24]
        s_f.append(_dot(qblk_b, kt) * scale + bias)
        carry = carry + jnp.sum(lt, axis=1, keepdims=True)
    carry_scr[...] = jnp.broadcast_to(carry, carry_scr.shape)
    m_old = mf_scr[:, 0:1]
    m_new = jnp.maximum(m_old, jnp.max(functools.reduce(jnp.maximum, s_f), axis=1, keepdims=True))
    alpha = jnp.exp(m_old - m_new)
    l_add = jnp.zeros((FOX_HEADS, 1), F32)
    pv = jnp.zeros((FOX_HEADS, FOX_W), F32)
    for i in range(pps):
        pr = jnp.exp(s_f[i] - m_new)
        l_add = l_add + jnp.sum(pr, axis=1, keepdims=True)
        vt = page_refs[3 * i][0, 1].astype(BF16)
        pv = pv + _dot_nt(pr.astype(BF16), vt)
    lf_scr[...] = alpha * lf_scr[...] + l_add
    af_scr[...] = alpha * af_scr[...] + pv
    mf_scr[...] = jnp.broadcast_to(m_new, mf_scr.shape)

    page = page_refs[2].shape[1] // dj
    s_d = []
    for i in range(pps):
        xd = page_refs[3 * i + 2]
        s = jnp.zeros((dj, page), F32)
        for h in range(DIFF_HEADS):
            k_h = xd[0, pl.ds(h, page, stride=dj), :].astype(BF16)
            s = s + _dot_nt(q8_h[h], k_h)
        s_d.append(s * scale)
    md_old = md_scr[:, 0:1]
    md_new = jnp.maximum(md_old, jnp.max(functools.reduce(jnp.maximum, s_d), axis=1, keepdims=True))
    alphad = jnp.exp(md_old - md_new)
    ld_add = jnp.zeros((dj, 1), F32)
    pvd = jnp.zeros((dj, hd), F32)
    for i in range(pps):
        xd = page_refs[3 * i + 2]
        pr = jnp.exp(s_d[i] - md_new)
        ld_add = ld_add + jnp.sum(pr, axis=1, keepdims=True)
        for h in range(DIFF_HEADS):
            v_h = xd[0, pl.ds(DIFF_HEADS + h, page, stride=dj), :].astype(BF16)
            pvd = pvd + _dot(jnp.where(head_of_row == h, pr, 0.0).astype(BF16), v_h)
    ld_scr[...] = alphad * ld_scr[...] + ld_add
    ad_scr[...] = alphad * ad_scr[...] + pvd
    md_scr[...] = jnp.broadcast_to(md_new, md_scr.shape)

    @pl.when(j == pl.num_programs(1) - 1)
    def _():
        o_all = af_scr[...] / lf_scr[:, 0:1]
        of_ref[0] = jnp.sum(jnp.where(own, o_all, 0.0), axis=0, keepdims=True)
        od_all = ad_scr[...] / ld_scr[:, 0:1]
        lp = lam_ref[...]
        lam = (jnp.exp(jnp.sum(lp[0:1] * lp[1:2], axis=1, keepdims=True))
               - jnp.exp(jnp.sum(lp[2:3] * lp[3:4], axis=1, keepdims=True)) + lam_init)
        o = od_all[0:DIFF_HEADS] - lam * od_all[DIFF_HEADS:]
        od_ref[0] = _rms(o, g_ref[...]) * (1.0 - lam_init)


def _decode(page_table, qf, qd, new_kvf, new_kvd, new_lft, lam_p, subln_g, cache_ft, cache_lt, cache_d, lam_init):
    nb, n_pages = page_table.shape
    page = cache_lt.shape[2]
    pps = math.gcd(_PAGES_PER_STEP, n_pages)
    dj, hd = 2 * DIFF_HEADS, 2 * DIFF_DH
    kk = lax.broadcasted_iota(jnp.int32, (page, page), 0)
    kc = lax.broadcasted_iota(jnp.int32, (page, page), 1)
    sfx = jnp.where(kk > kc, 1.0, 0.0).astype(BF16)

    def page_idx(i):
        return lambda s, j, pt: pt[s, n_pages - 1 - (j * pps + i)]

    page_specs = []
    page_args = []
    for i in range(pps):
        pick = page_idx(i)
        page_specs += [
            pl.BlockSpec((1, 2, FOX_W, page), lambda s, j, pt, pick=pick: (pick(s, j, pt), 0, 0, 0)),
            pl.BlockSpec((1, FOX_HEADS, page), lambda s, j, pt, pick=pick: (pick(s, j, pt), 0, 0)),
            pl.BlockSpec((1, page * dj, hd), lambda s, j, pt, pick=pick: (pick(s, j, pt), 0, 0)),
        ]
        page_args += [cache_ft, cache_lt, cache_d]
    samp = lambda s, j, pt: (s, 0, 0)
    const = lambda s, j, pt: (0, 0)
    grid_spec = pltpu.PrefetchScalarGridSpec(
        num_scalar_prefetch=1,
        grid=(nb, n_pages // pps),
        in_specs=[
            pl.BlockSpec((1, 1, FOX_W), samp),
            pl.BlockSpec((1, 1, DIFF_W), samp),
            pl.BlockSpec((1, 1, 2 * FOX_W), samp),
            pl.BlockSpec((1, 1, 2 * DIFF_W), samp),
            pl.BlockSpec(new_lft.shape, const),
            pl.BlockSpec(lam_p.shape, const),
            pl.BlockSpec(subln_g.shape, const),
            pl.BlockSpec(sfx.shape, const),
        ] + page_specs,
        out_specs=[
            pl.BlockSpec((1, 1, FOX_W), samp),
            pl.BlockSpec((1, DIFF_HEADS, hd), samp),
        ],
        scratch_shapes=[
            pltpu.VMEM((FOX_HEADS, LANES), F32), pltpu.VMEM((FOX_HEADS, LANES), F32),
            pltpu.VMEM((FOX_HEADS, FOX_W), F32),
            pltpu.VMEM((dj, LANES), F32), pltpu.VMEM((dj, LANES), F32),
            pltpu.VMEM((dj, hd), F32),
            pltpu.VMEM((FOX_HEADS, LANES), F32),
        ],
    )
    return pl.pallas_call(
        functools.partial(_decode_kernel, pps=pps, lam_init=lam_init),
        out_shape=[jax.ShapeDtypeStruct((nb, 1, FOX_W), F32),
                   jax.ShapeDtypeStruct((nb, DIFF_HEADS, hd), F32)],
        grid_spec=grid_spec,
        compiler_params=_params("parallel", "arbitrary"),
        name="decode_attn",
    )(page_table, qf, qd, new_kvf, new_kvd, new_lft, lam_p, subln_g, sfx, *page_args)


def _rec_proj_kernel(x_ref, g_ref, w_ref, wgt_ref, bg_ref, q_ref, k_ref, v_ref, og_ref, u_ref, gt_ref):
    h = _rms(x_ref[0], g_ref[...]).astype(BF16)
    c0 = 0
    q_ref[0] = _dot(h, w_ref[:, c0:c0 + MLSTM_QK]) * (MLSTM_DK ** -0.5)
    c0 += MLSTM_QK
    k_ref[0] = _dot(h, w_ref[:, c0:c0 + MLSTM_QK])
    c0 += MLSTM_QK
    v_ref[0] = _dot(h, w_ref[:, c0:c0 + MLSTM_V])
    c0 += MLSTM_V
    og_ref[0] = _dot(h, w_ref[:, c0:c0 + MLSTM_V])
    c0 += MLSTM_V
    ua = _dot(h, w_ref[:, c0:c0 + CONV_CH])
    c0 += CONV_CH
    ub = _dot(h, w_ref[:, c0:c0 + CONV_CH])
    u_ref[0] = ua * jax.nn.sigmoid(ub)
    z = _dot_nt(wgt_ref[...], h) + bg_ref[...]
    rowi = lax.broadcasted_iota(jnp.int32, z.shape, 0)
    gt_ref[0] = jnp.where(rowi < MLSTM_HEADS, z, _log_sigmoid(z))


def _rec_proj(x, g, w_main, wgt, bg):
    b, t, d = x.shape
    tm = min(512, t)
    row = lambda i, j: (i, j, 0)
    const = lambda i, j: (0, 0)
    widths = [MLSTM_QK, MLSTM_QK, MLSTM_V, MLSTM_V, CONV_CH]
    return pl.pallas_call(
        _rec_proj_kernel,
        out_shape=[jax.ShapeDtypeStruct((b, t, w), F32) for w in widths]
        + [jax.ShapeDtypeStruct((b, 2 * MLSTM_HEADS, t), F32)],
        grid=(b, t // tm),
        in_specs=[
            pl.BlockSpec((1, tm, d), row),
            pl.BlockSpec((1, d), const),
            pl.BlockSpec(w_main.shape, const),
            pl.BlockSpec(wgt.shape, const),
            pl.BlockSpec(bg.shape, const),
        ],
        out_specs=[pl.BlockSpec((1, tm, w), row) for w in widths]
        + [pl.BlockSpec((1, 2 * MLSTM_HEADS, tm), lambda i, j: (i, 0, j))],
        compiler_params=_params("parallel", "parallel"),
        name="rec_proj",
    )(x, g.reshape(1, d), w_main, wgt, bg)


def _mlstm_prompt_kernel(q_ref, k_ref, v_ref, og_ref, gt_ref, gc_ref, ng_ref,
                         hm_ref, c_ref, n_ref, m_ref, ct_scr, n_scr, m_scr, *, chunk):
    t = pl.program_id(1)
    L = chunk
    half = LANES // 2

    @pl.when(t == 0)
    def _():
        ct_scr[...] = jnp.zeros_like(ct_scr)
        n_scr[...] = jnp.zeros_like(n_scr)
        m_scr[...] = jnp.full(m_scr.shape, NEG, F32)

    gt = gt_ref[0]
    gc = gc_ref[0]
    cum_rows = _dot3_left(gt, _tri(L, upper=True))
    cum_cols = _dot3_right(_tri(L, upper=False), gc)
    rr = lax.broadcasted_iota(jnp.int32, (L, L), 0)
    cc = lax.broadcasted_iota(jnp.int32, (L, L), 1)
    tri = cc <= rr
    lane = lax.broadcasted_iota(jnp.int32, (L, LANES), 1)

    for pair in range(MLSTM_HEADS // 2):
        qp = q_ref[0, :, pair * LANES:(pair + 1) * LANES]
        kp = k_ref[0, :, pair * LANES:(pair + 1) * LANES]
        kp_b = kp.astype(BF16)
        kt_b = kp.T.astype(BF16)
        ct = ct_scr[pair]
        n_row = n_scr[pair]
        ct_new = []
        n_new = []
        for c in range(2):
            h = 2 * pair + c
            mine = (lane < half) if c == 0 else (lane >= half)
            qm = jnp.where(mine, qp, 0.0)
            qm_b = qm.astype(BF16)
            vh = v_ref[0, :, h * MLSTM_DV:(h + 1) * MLSTM_DV]
            a_col = cum_cols[:, MLSTM_HEADS + h:MLSTM_HEADS + h + 1]
            ig_col = gc[:, h:h + 1]
            b_row = gt[h:h + 1, :] - cum_rows[MLSTM_HEADS + h:MLSTM_HEADS + h + 1, :]
            m_prev = m_scr[h:h + 1, 0:1]
            d = jnp.where(tri, a_col + b_row, NEG)
            m_inter = a_col + m_prev
            m_t = jnp.maximum(jnp.max(d, axis=1, keepdims=True), m_inter)
            s = _dot_nt(qm_b, kp_b) * jnp.exp(d - m_t)
            w_inter = jnp.exp(m_inter - m_t)
            num = _dot(s.astype(BF16), vh.astype(BF16)) + w_inter * _dot(qm_b, ct.astype(BF16))
            den = jnp.sum(s, axis=1, keepdims=True) + w_inter * jnp.sum(qm * n_row, axis=1, keepdims=True)
            hout = num / jnp.maximum(jnp.abs(den), jnp.exp(-m_t))
            gh = ng_ref[:, h * MLSTM_DV:(h + 1) * MLSTM_DV]
            oh = og_ref[0, :, h * MLSTM_DV:(h + 1) * MLSTM_DV]
            hm_ref[0, :, h * MLSTM_DV:(h + 1) * MLSTM_DV] = _rms(hout, gh) * jax.nn.sigmoid(oh)
            m_new = m_t[L - 1:L, :]
            cum_last = a_col[L - 1:L, :]
            decay = jnp.exp(cum_last + m_prev - m_new)
            w_key = jnp.exp(cum_last - a_col + ig_col - m_new)
            upd = _dot(kt_b, (w_key * vh).astype(BF16))
            lo, hi_ = c * half, (c + 1) * half
            ct_new.append(decay * ct[lo:hi_] + upd[lo:hi_])
            n_new.append(decay * n_row + jnp.sum(w_key * kp, axis=0, keepdims=True))
            m_scr[h:h + 1, :] = jnp.broadcast_to(m_new, (1, LANES))
        ct_scr[pair] = jnp.concatenate(ct_new, axis=0)
        lane1 = lax.broadcasted_iota(jnp.int32, (1, LANES), 1)
        n_scr[pair] = jnp.where(lane1 < half, n_new[0], n_new[1])

    @pl.when(t == pl.num_programs(1) - 1)
    def _():
        for pair in range(MLSTM_HEADS // 2):
            c_pair = ct_scr[pair].T
            n_row = n_scr[pair]
            for c in range(2):
                h = 2 * pair + c
                c_ref[0, h] = c_pair[:, c * half:(c + 1) * half]
                n_ref[0, h:h + 1, :] = n_row[:, c * half:(c + 1) * half]
                m_ref[0, :, h:h + 1] = m_scr[h:h + 1, 0:1]


def _mlstm_prompt(q, k, v, og, gt, gc, norm_g):
    b, t, _ = q.shape
    chunk = min(256, t)
    row = lambda i, j: (i, j, 0)
    const = lambda i, j: (0, 0)
    return pl.pallas_call(
        functools.partial(_mlstm_prompt_kernel, chunk=chunk),
        out_shape=[
            jax.ShapeDtypeStruct((b, t, MLSTM_V), F32),
            jax.ShapeDtypeStruct((b, MLSTM_HEADS, MLSTM_DV, MLSTM_DK), F32),
            jax.ShapeDtypeStruct((b, MLSTM_HEADS, MLSTM_DK), F32),
            jax.ShapeDtypeStruct((b, 1, MLSTM_HEADS), F32),
        ],
        grid=(b, t // chunk),
        in_specs=[
            pl.BlockSpec((1, chunk, MLSTM_QK), row),
            pl.BlockSpec((1, chunk, MLSTM_QK), row),
            pl.BlockSpec((1, chunk, MLSTM_V), row),
            pl.BlockSpec((1, chunk, MLSTM_V), row),
            pl.BlockSpec((1, 2 * MLSTM_HEADS, chunk), lambda i, j: (i, 0, j)),
            pl.BlockSpec((1, chunk, 2 * MLSTM_HEADS), row),
            pl.BlockSpec((1, MLSTM_V), const),
        ],
        out_specs=[
            pl.BlockSpec((1, chunk, MLSTM_V), row),
            pl.BlockSpec((1, MLSTM_HEADS, MLSTM_DV, MLSTM_DK), lambda i, j: (i, 0, 0, 0)),
            pl.BlockSpec((1, MLSTM_HEADS, MLSTM_DK), lambda i, j: (i, 0, 0)),
            pl.BlockSpec((1, 1, MLSTM_HEADS), lambda i, j: (i, 0, 0)),
        ],
        scratch_shapes=[
            pltpu.VMEM((MLSTM_HEADS // 2, LANES, MLSTM_DV), F32),
            pltpu.VMEM((MLSTM_HEADS // 2, 1, LANES), F32),
            pltpu.VMEM((SUBLANES, LANES), F32),
        ],
        compiler_params=_params("parallel", "arbitrary"),
        name="mlstm_prompt",
    )(q, k, v, og, gt, gc, norm_g.reshape(1, MLSTM_V))


def _mlstm_sample_kernel(q_ref, k_ref, v_ref, og_ref, gc_ref, m_ref, kall_ref, vall_ref, ng_ref, c_ref, n_ref,
                         hm_ref, co_ref, no_ref, mo_ref, vt_scr, *, bs):
    i = pl.program_id(0)
    nb = kall_ref.shape[0]

    @pl.when(i == 0)
    def _():
        for h in range(MLSTM_HEADS):
            vt_scr[h] = vall_ref[:, h * MLSTM_DV:(h + 1) * MLSTM_DV].T

    rows = lax.broadcasted_iota(jnp.int32, (nb, MLSTM_DK), 0)

    def body(bl, _):
        b = i * bs + bl
        q_row = q_ref[bl]
        k_row = k_ref[bl]
        v_row = v_ref[bl]
        og_row = og_ref[bl]
        g_row = gc_ref[bl]
        m_row = m_ref[bl]
        for h in range(MLSTM_HEADS):
            qh = q_row[:, h * MLSTM_DK:(h + 1) * MLSTM_DK]
            kh = k_row[:, h * MLSTM_DK:(h + 1) * MLSTM_DK]
            vh = v_row[:, h * MLSTM_DV:(h + 1) * MLSTM_DV]
            it = g_row[:, h:h + 1]
            lf = g_row[:, MLSTM_HEADS + h:MLSTM_HEADS + h + 1]
            m_prev = m_row[:, h:h + 1]
            m_inter = lf + m_prev
            m_t = jnp.maximum(it, m_inter)
            e_i = jnp.exp(it - m_t)
            w_inter = jnp.exp(m_inter - m_t)
            s = jnp.sum(qh * kh, axis=1, keepdims=True) * e_i
            cm = c_ref[bl, h]
            nh = n_ref[bl, h:h + 1, :]
            num = s * vh + w_inter * _dot_nt(qh.astype(BF16), cm.astype(BF16))
            den = s + w_inter * jnp.sum(nh * qh, axis=1, keepdims=True)
            hrow = num / jnp.maximum(jnp.abs(den), jnp.exp(-m_t))
            gh = ng_ref[:, h * MLSTM_DV:(h + 1) * MLSTM_DV]
            oh = og_row[:, h * MLSTM_DV:(h + 1) * MLSTM_DV]
            hm_ref[bl, :, h * MLSTM_DV:(h + 1) * MLSTM_DV] = _rms(hrow, gh) * jax.nn.sigmoid(oh)
            k_all = kall_ref[:, h * MLSTM_DK:(h + 1) * MLSTM_DK]
            k_sel = jnp.where(rows == b, k_all, 0.0).astype(BF16)
            outer = _dot(vt_scr[h].astype(BF16), k_sel)
            co_ref[bl, h] = w_inter * cm + e_i * outer
            no_ref[bl, h:h + 1, :] = w_inter * nh + e_i * kh
            mo_ref[bl, :, h:h + 1] = m_t
        return 0

    lax.fori_loop(0, bs, body, 0)


def _mlstm_sample(q, k, v, og, gc, norm_g, c0, n0, m0):
    nb = q.shape[0]
    bs = min(16, nb)
    full = lambda i: (0, 0)
    rows = lambda i: (i, 0, 0)
    per_sample = [a.reshape(nb, 1, a.shape[1]) for a in (q, k, v, og, gc, m0)]
    hm, c1, n1, m1 = pl.pallas_call(
        functools.partial(_mlstm_sample_kernel, bs=bs),
        out_shape=[
            jax.ShapeDtypeStruct((nb, 1, MLSTM_V), F32),
            jax.ShapeDtypeStruct(c0.shape, F32),
            jax.ShapeDtypeStruct(n0.shape, F32),
            jax.ShapeDtypeStruct((nb, 1, MLSTM_HEADS), F32),
        ],
        grid=(nb // bs,),
        in_specs=[pl.BlockSpec((bs, 1, a.shape[2]), rows) for a in per_sample] + [
            pl.BlockSpec(k.shape, full),
            pl.BlockSpec(v.shape, full),
            pl.BlockSpec((1, MLSTM_V), full),
            pl.BlockSpec((bs, MLSTM_HEADS, MLSTM_DV, MLSTM_DK), lambda i: (i, 0, 0, 0)),
            pl.BlockSpec((bs, MLSTM_HEADS, MLSTM_DK), rows),
        ],
        out_specs=[
            pl.BlockSpec((bs, 1, MLSTM_V), rows),
            pl.BlockSpec((bs, MLSTM_HEADS, MLSTM_DV, MLSTM_DK), lambda i: (i, 0, 0, 0)),
            pl.BlockSpec((bs, MLSTM_HEADS, MLSTM_DK), rows),
            pl.BlockSpec((bs, 1, MLSTM_HEADS), rows),
        ],
        scratch_shapes=[pltpu.VMEM((MLSTM_HEADS, MLSTM_DV, nb), F32)],
        compiler_params=_params("arbitrary"),
        name="mlstm_sample",
    )(*per_sample, k, v, norm_g.reshape(1, MLSTM_V), c0, n0)
    return hm.reshape(nb, MLSTM_V), c1, n1, m1.reshape(nb, MLSTM_HEADS)


_CONV_HALO = 32
_CONV_ROWS = 32


def _ln_silu(y, g, b):
    yc = y - jnp.mean(y, axis=-1, keepdims=True)
    var = jnp.mean(yc * yc, axis=-1, keepdims=True)
    z = yc * lax.rsqrt(var + NORM_EPS) * g + b
    return z * jax.nn.sigmoid(z)


def _conv_prompt_kernel(u_ref, prev_ref, w_ref, b_ref, g_ref, be_ref, c_ref, st_ref, full_scr, shift_scr, *, tt):
    t = pl.program_id(1)
    pad = _CONV_HALO - (CONV_W - 1)

    @pl.when(t == 0)
    def _():
        full_scr[0:_CONV_HALO, :] = jnp.zeros((_CONV_HALO, CONV_CH), F32)
        full_scr[pad:_CONV_HALO, :] = prev_ref[0]

    full_scr[_CONV_HALO:, :] = u_ref[0]
    span = tt + _CONV_HALO - SUBLANES
    for r in range(1, SUBLANES):
        shift_scr[r, 0:span, :] = full_scr[r:r + span, :]
    for r0 in range(0, tt, _CONV_ROWS):
        acc = jnp.broadcast_to(b_ref[...], (_CONV_ROWS, CONV_CH))
        for k in range(CONV_W):
            off = r0 + pad + k
            r = off % SUBLANES
            if r == 0:
                rows = full_scr[off:off + _CONV_ROWS, :]
            else:
                rows = shift_scr[r, off - r:off - r + _CONV_ROWS, :]
            acc = acc + w_ref[k:k + 1, :] * rows
        c_ref[0, r0:r0 + _CONV_ROWS, :] = _ln_silu(acc, g_ref[...], be_ref[...])

    @pl.when(t == pl.num_programs(1) - 1)
    def _():
        st_ref[0] = full_scr[_CONV_HALO + tt - (CONV_W - 1):, :]

    full_scr[0:_CONV_HALO, :] = full_scr[tt:tt + _CONV_HALO, :]


def _conv_prompt(u, prev, w, bias, ln_g, ln_b):
    b, t, ch = u.shape
    tt = min(256, t)
    const = lambda i, j: (0, 0)
    return pl.pallas_call(
        functools.partial(_conv_prompt_kernel, tt=tt),
        out_shape=[jax.ShapeDtypeStruct((b, t, ch), F32),
                   jax.ShapeDtypeStruct((b, CONV_W - 1, ch), F32)],
        grid=(b, t // tt),
        in_specs=[
            pl.BlockSpec((1, tt, ch), lambda i, j: (i, j, 0)),
            pl.BlockSpec((1, CONV_W - 1, ch), lambda i, j: (i, 0, 0)),
            pl.BlockSpec((CONV_W, ch), const),
            pl.BlockSpec((1, ch), const),
            pl.BlockSpec((1, ch), const),
            pl.BlockSpec((1, ch), const),
        ],
        out_specs=[pl.BlockSpec((1, tt, ch), lambda i, j: (i, j, 0)),
                   pl.BlockSpec((1, CONV_W - 1, ch), lambda i, j: (i, 0, 0))],
        scratch_shapes=[pltpu.VMEM((_CONV_HALO + tt, ch), F32),
                        pltpu.VMEM((SUBLANES, _CONV_HALO + tt, ch), F32)],
        compiler_params=_params("parallel", "arbitrary"),
        name="conv_prompt",
    )(u, prev, w, bias.reshape(1, ch), ln_g.reshape(1, ch), ln_b.reshape(1, ch))


def _conv_sample_kernel(u_ref, prev_ref, w_ref, b_ref, g_ref, be_ref, c_ref, st_ref):
    hist = CONV_W - 1
    u = u_ref[...]
    y = w_ref[hist:hist + 1, :] * u + b_ref[...]
    for k in range(hist):
        row = prev_ref[k]
        y = y + w_ref[k:k + 1, :] * row
        if k > 0:
            st_ref[k - 1] = row
    st_ref[hist - 1] = u
    c_ref[...] = _ln_silu(y, g_ref[...], be_ref[...])


def _conv_sample(u, prev_t, w, bias, ln_g, ln_b):
    nb, ch = u.shape
    bs = min(32, nb)
    hist = CONV_W - 1
    const = lambda i: (0, 0)
    return pl.pallas_call(
        _conv_sample_kernel,
        out_shape=[jax.ShapeDtypeStruct((nb, ch), F32),
                   jax.ShapeDtypeStruct((hist, nb, ch), F32)],
        grid=(nb // bs,),
        in_specs=[
            pl.BlockSpec((bs, ch), lambda i: (i, 0)),
            pl.BlockSpec((hist, bs, ch), lambda i: (0, i, 0)),
            pl.BlockSpec((CONV_W, ch), const),
            pl.BlockSpec((1, ch), const),
            pl.BlockSpec((1, ch), const),
            pl.BlockSpec((1, ch), const),
        ],
        out_specs=[pl.BlockSpec((bs, ch), lambda i: (i, 0)),
                   pl.BlockSpec((hist, bs, ch), lambda i: (0, i, 0))],
        compiler_params=_params("parallel"),
        name="conv_sample",
    )(u, prev_t, w, bias.reshape(1, ch), ln_g.reshape(1, ch), ln_b.reshape(1, ch))


def kernel(x_prompt, x_sample, cache_fox_kv, cache_fox_logf, cache_diff_kv, state_mlstm_C, state_mlstm_n,
           state_mlstm_m, state_conv, page_table, norm_g, final_g, ffn_w_in, ffn_w_out, attn_w_in, attn_b_f,
           diff_lam, diff_subln_g, attn_w_out, rec_w_in, rec_b_i, rec_b_f, mlstm_norm_g, conv_w, conv_b,
           conv_ln_g, conv_ln_b, rec_w_out):
    bp, t, d = x_prompt.shape
    nb = x_sample.shape[0]
    depth = norm_g.shape[0]
    n_pool, page = cache_fox_kv.shape[1], cache_fox_kv.shape[2]
    xp = x_prompt.reshape(bp * t, d)
    xs = x_sample.reshape(nb, d)
    w_in_b = ffn_w_in.astype(BF16)
    w_out_b = ffn_w_out.astype(BF16)

    outs = {k: [] for k in ("fkv_p", "fkv_s", "flf_p", "flf_s", "dkv_p", "dkv_s",
                            "c_p", "c_s", "n_p", "n_s", "m_p", "m_s", "cv_p", "cv_s")}
    for l in range(depth):
        j = l // 2
        xp = _ffn(xp, norm_g[l, 0], w_in_b[l, 0], w_out_b[l, 0])
        xs = _ffn(xs, norm_g[l, 0], w_in_b[l, 0], w_out_b[l, 0])
        if l % 2 == 0:
            lam_init = 0.8 - 0.6 * math.exp(-0.3 * l)
            w = attn_w_in[j]
            c1, c2 = 3 * FOX_W, 3 * FOX_W + FOX_HEADS
            w_main = jnp.concatenate([w[:, :c1], w[:, c2:]], axis=1).astype(BF16)
            wft = w[:, c1:c2].T.astype(BF16)
            bf = attn_b_f[j].reshape(FOX_HEADS, 1)
            w_o = attn_w_out[j].astype(BF16)
            qf, kvf, qd, kvd, lft, cumt = _attn_proj(xp.reshape(bp, t, d), norm_g[l, 1], w_main, wft, bf)
            o_f = _attn_prompt(True, qf, kvf, jnp.swapaxes(cumt, 1, 2), cumt, lam_init)
            o_d = _attn_prompt(False, qd, kvd, diff_lam[j], diff_subln_g[j].reshape(1, -1), lam_init)
            xp = _merge(xp, o_f.reshape(bp * t, FOX_W), o_d.reshape(bp * t, DIFF_W), w_o)
            outs["fkv_p"].append(kvf.reshape(bp, t, 2, FOX_HEADS, FOX_DH))
            outs["flf_p"].append(jnp.swapaxes(lft, 1, 2))
            outs["dkv_p"].append(kvd.reshape(bp, t, 2, DIFF_HEADS, 2 * DIFF_DH))
            sqf, skvf, sqd, skvd, slft, _ = _attn_proj(xs.reshape(1, nb, d), norm_g[l, 1], w_main, wft, bf)
            so_f, so_d = _decode(
                page_table,
                sqf.reshape(nb, 1, FOX_W), sqd.reshape(nb, 1, DIFF_W),
                skvf.reshape(nb, 1, 2 * FOX_W), skvd.reshape(nb, 1, 2 * DIFF_W),
                slft[0], diff_lam[j], diff_subln_g[j].reshape(1, -1),
                jnp.transpose(cache_fox_kv[j], (0, 2, 3, 4, 1)).reshape(n_pool, 2, FOX_W, page),
                jnp.swapaxes(cache_fox_logf[j], 1, 2),
                cache_diff_kv[j].reshape(n_pool, page * 2 * DIFF_HEADS, 2 * DIFF_DH),
                lam_init)
            xs = _merge(xs, so_f.reshape(nb, FOX_W), so_d.reshape(nb, DIFF_W), w_o)
            outs["fkv_s"].append(skvf.reshape(nb, 1, 2, FOX_HEADS, FOX_DH))
            outs["flf_s"].append(jnp.swapaxes(slft, 1, 2).reshape(nb, 1, FOX_HEADS))
            outs["dkv_s"].append(skvd.reshape(nb, 1, 2, DIFF_HEADS, 2 * DIFF_DH))
        else:
            w = rec_w_in[j]
            c1 = 2 * MLSTM_QK + MLSTM_V
            c2 = c1 + 2 * MLSTM_HEADS
            w_main = jnp.concatenate([w[:, :c1], w[:, c2:]], axis=1).astype(BF16)
            wgt = w[:, c1:c2].T.astype(BF16)
            bg = jnp.concatenate([rec_b_i[j], rec_b_f[j]]).reshape(2 * MLSTM_HEADS, 1)
            w_o = rec_w_out[j].astype(BF16)
            q, k, v, og, u, gt = _rec_proj(xp.reshape(bp, t, d), norm_g[l, 1], w_main, wgt, bg)
            hm, c_p, n_p, m_p = _mlstm_prompt(q, k, v, og, gt, jnp.swapaxes(gt, 1, 2), mlstm_norm_g[j])
            cv, st_p = _conv_prompt(u, jnp.zeros((bp, CONV_W - 1, CONV_CH), F32), conv_w[j], conv_b[j],
                                    conv_ln_g[j], conv_ln_b[j])
            xp = _merge(xp, hm.reshape(bp * t, MLSTM_V), cv.reshape(bp * t, CONV_CH), w_o)
            outs["c_p"].append(c_p)
            outs["n_p"].append(n_p)
            outs["m_p"].append(m_p.reshape(bp, MLSTM_HEADS))
            outs["cv_p"].append(st_p)
            sq, sk, sv, sog, su, sgt = _rec_proj(xs.reshape(1, nb, d), norm_g[l, 1], w_main, wgt, bg)
            shm, c_s, n_s, m_s = _mlstm_sample(sq[0], sk[0], sv[0], sog[0], sgt[0].T, mlstm_norm_g[j],
                                               state_mlstm_C[j], state_mlstm_n[j], state_mlstm_m[j])
            scv, st_s = _conv_sample(su[0], jnp.swapaxes(state_conv[j], 0, 1), conv_w[j], conv_b[j],
                                     conv_ln_g[j], conv_ln_b[j])
            xs = _merge(xs, shm, scv, w_o)
            outs["c_s"].append(c_s)
            outs["n_s"].append(n_s)
            outs["m_s"].append(m_s)
            outs["cv_s"].append(jnp.swapaxes(st_s, 0, 1))
        last = l == depth - 1
        xp = _ffn(xp, norm_g[l, 2], w_in_b[l, 1], w_out_b[l, 1], final_g if last else None)
        xs = _ffn(xs, norm_g[l, 2], w_in_b[l, 1], w_out_b[l, 1], final_g if last else None)

    st = jnp.stack
    return (xp.reshape(bp, t, d), xs.reshape(nb, 1, d),
            st(outs["fkv_p"]), st(outs["fkv_s"]), st(outs["flf_p"]), st(outs["flf_s"]),
            st(outs["dkv_p"]), st(outs["dkv_s"]),
            st(outs["c_p"]), st(outs["c_s"]), st(outs["n_p"]), st(outs["n_s"]),
            st(outs["m_p"]), st(outs["m_s"]), st(outs["cv_p"]), st(outs["cv_s"]))
```

```python
import functools
import math

import jax
import jax.numpy as jnp
from jax import lax
from jax.experimental import pallas as pl
from jax.experimental.pallas import tpu as pltpu

F32 = jnp.float32
BF16 = jnp.bfloat16
NORM_EPS = 1e-6
NEG = -1e30
LOG2E = 1.4426950408889634

FOX_HEADS = 8
FOX_DH = 64
DIFF_HEADS = 4
DIFF_DH = 64
MLSTM_HEADS = 4
MLSTM_DK = 64
MLSTM_DV = 128
CONV_CH = 512
CONV_W = 31
FOX_W = FOX_HEADS * FOX_DH
DIFF_W = DIFF_HEADS * 2 * DIFF_DH
MLSTM_QK = MLSTM_HEADS * MLSTM_DK
MLSTM_V = MLSTM_HEADS * MLSTM_DV

LANES = 128
SUBLANES = 8
VMEM_LIMIT_BYTES = 56 * 1024 * 1024

_NT = (((1,), (1,)), ((), ()))


def _params(*sem):
    return pltpu.CompilerParams(dimension_semantics=sem, vmem_limit_bytes=VMEM_LIMIT_BYTES)


def _rms(x, g):
    return x * lax.rsqrt(jnp.mean(x * x, axis=-1, keepdims=True) + NORM_EPS) * g


def _log_sigmoid(z):
    return jnp.minimum(z, 0.0) - jnp.log(1.0 + jnp.exp(-jnp.abs(z)))


def _split3(x):
    hi = x.astype(BF16)
    r = x - hi.astype(F32)
    mid = r.astype(BF16)
    lo = (r - mid.astype(F32)).astype(BF16)
    return hi, mid, lo


def _dot(a, b):
    return jnp.dot(a, b, preferred_element_type=F32)


def _dot_nt(a, b):
    return lax.dot_general(a, b, _NT, preferred_element_type=F32)


def _dot3_left(x, t):
    hi, mid, lo = _split3(x)
    return _dot(hi, t) + _dot(mid, t) + _dot(lo, t)


def _dot3_right(t, x):
    hi, mid, lo = _split3(x)
    return _dot(t, hi) + _dot(t, mid) + _dot(t, lo)


def _tri(n, upper):
    r = lax.broadcasted_iota(jnp.int32, (n, n), 0)
    c = lax.broadcasted_iota(jnp.int32, (n, n), 1)
    keep = (r <= c) if upper else (r >= c)
    return jnp.where(keep, 1.0, 0.0).astype(BF16)


def _ffn_kernel(x_ref, g_ref, wa_ref, wb_ref, wo_ref, fg_ref, o_ref, h_scr, gated_scr, *, final_norm):
    j = pl.program_id(1)
    tf = wa_ref.shape[1]

    @pl.when(j == 0)
    def _():
        h_scr[...] = _rms(x_ref[...], g_ref[...]).astype(BF16)

    h = h_scr[...]
    a = _dot(h, wa_ref[...])
    b = _dot(h, wb_ref[...])
    gated_scr[:, pl.ds(pl.multiple_of(j * tf, tf), tf)] = (a * jax.nn.sigmoid(a) * b).astype(BF16)

    @pl.when(j == pl.num_programs(1) - 1)
    def _():
        y = x_ref[...] + 0.5 * _dot(gated_scr[...], wo_ref[...])
        if final_norm:
            y = _rms(y, fg_ref[...])
        o_ref[...] = y


def _ffn(x, g, w_in, w_out, final_g=None):
    m, d = x.shape
    f = w_out.shape[0]
    tm = min(1024, m)
    tf = 256
    nf = f // tf
    assert m % tm == 0 and f % tf == 0, (m, f)
    fg = jnp.ones((1, d), F32) if final_g is None else final_g.reshape(1, d)
    return pl.pallas_call(
        functools.partial(_ffn_kernel, final_norm=final_g is not None),
        out_shape=jax.ShapeDtypeStruct((m, d), F32),
        grid=(m // tm, nf),
        in_specs=[
            pl.BlockSpec((tm, d), lambda i, j: (i, 0)),
            pl.BlockSpec((1, d), lambda i, j: (0, 0)),
            pl.BlockSpec((d, tf), lambda i, j: (0, j)),
            pl.BlockSpec((d, tf), lambda i, j: (0, j + nf)),
            pl.BlockSpec((f, d), lambda i, j: (0, 0)),
            pl.BlockSpec((1, d), lambda i, j: (0, 0)),
        ],
        out_specs=pl.BlockSpec((tm, d), lambda i, j: (i, 0)),
        scratch_shapes=[pltpu.VMEM((tm, d), BF16), pltpu.VMEM((tm, f), BF16)],
        compiler_params=_params("parallel", "arbitrary"),
        name="ffn",
    )(x, g.reshape(1, d), w_in, w_in, w_out, fg)


def _merge_kernel(x_ref, oa_ref, ob_ref, w_ref, o_ref):
    wa = oa_ref.shape[-1]
    y = _dot(oa_ref[...].astype(BF16), w_ref[:wa, :])
    y += _dot(ob_ref[...].astype(BF16), w_ref[wa:, :])
    o_ref[...] = x_ref[...] + y


def _merge(x, oa, ob, w_out):
    m, d = x.shape
    tm = min(1024, m)
    wa, wb = oa.shape[1], ob.shape[1]
    return pl.pallas_call(
        _merge_kernel,
        out_shape=jax.ShapeDtypeStruct((m, d), F32),
        grid=(m // tm,),
        in_specs=[
            pl.BlockSpec((tm, d), lambda i: (i, 0)),
            pl.BlockSpec((tm, wa), lambda i: (i, 0)),
            pl.BlockSpec((tm, wb), lambda i: (i, 0)),
            pl.BlockSpec((wa + wb, d), lambda i: (0, 0)),
        ],
        out_specs=pl.BlockSpec((tm, d), lambda i: (i, 0)),
        compiler_params=_params("parallel"),
        name="merge",
    )(x, oa, ob, w_out)


def _attn_proj_kernel(x_ref, g_ref, w_ref, wft_ref, bf_ref, qf_ref, kvf_ref, qd_ref, kvd_ref,
                      lf_ref, cum_ref, carry_scr):
    t = pl.program_id(1)
    h = _rms(x_ref[0], g_ref[...]).astype(BF16)
    qf_ref[0] = _dot(h, w_ref[:, 0:FOX_W])
    kvf_ref[0] = _dot(h, w_ref[:, FOX_W:3 * FOX_W])
    qd_ref[0] = _dot(h, w_ref[:, 3 * FOX_W:3 * FOX_W + DIFF_W])
    kvd_ref[0] = _dot(h, w_ref[:, 3 * FOX_W + DIFF_W:])
    logf = _log_sigmoid(_dot_nt(wft_ref[...], h) + bf_ref[...])
    lf_ref[0] = logf

    @pl.when(t == 0)
    def _():
        carry_scr[...] = jnp.zeros_like(carry_scr)

    tm = logf.shape[1]
    cum = _dot3_left(logf, _tri(tm, upper=True)) + carry_scr[:, 0:1]
    cum_ref[0] = cum
    carry_scr[...] = jnp.broadcast_to(cum[:, tm - 1:tm], carry_scr.shape)


def _attn_proj(x, g, w_main, wft, bf):
    b, t, d = x.shape
    tm = min(512, t)
    n_main = w_main.shape[1]
    row = lambda i, j: (i, j, 0)
    col = lambda i, j: (i, 0, j)
    const = lambda i, j: (0, 0)
    return pl.pallas_call(
        _attn_proj_kernel,
        out_shape=[
            jax.ShapeDtypeStruct((b, t, FOX_W), F32),
            jax.ShapeDtypeStruct((b, t, 2 * FOX_W), F32),
            jax.ShapeDtypeStruct((b, t, DIFF_W), F32),
            jax.ShapeDtypeStruct((b, t, 2 * DIFF_W), F32),
            jax.ShapeDtypeStruct((b, FOX_HEADS, t), F32),
            jax.ShapeDtypeStruct((b, FOX_HEADS, t), F32),
        ],
        grid=(b, t // tm),
        in_specs=[
            pl.BlockSpec((1, tm, d), row),
            pl.BlockSpec((1, d), const),
            pl.BlockSpec((d, n_main), const),
            pl.BlockSpec((FOX_HEADS, d), const),
            pl.BlockSpec((FOX_HEADS, 1), const),
        ],
        out_specs=[
            pl.BlockSpec((1, tm, FOX_W), row),
            pl.BlockSpec((1, tm, 2 * FOX_W), row),
            pl.BlockSpec((1, tm, DIFF_W), row),
            pl.BlockSpec((1, tm, 2 * DIFF_W), row),
            pl.BlockSpec((1, FOX_HEADS, tm), col),
            pl.BlockSpec((1, FOX_HEADS, tm), col),
        ],
        scratch_shapes=[pltpu.VMEM((FOX_HEADS, LANES), F32)],
        compiler_params=_params("parallel", "arbitrary"),
        name="attn_proj",
    )(x, g.reshape(1, d), w_main, wft, bf)


_V_ROWS = 144
_ATTN_GROUP = 4


def _attn_prompt_kernel(*refs, fox, tq, lam_init):
    if fox:
        q_ref, k_ref, v_ref, ccol_ref, crow_ref, o_ref, ka_scr, vt_scr, m_scr, acc_scr = refs
    else:
        q_ref, k_ref, v_ref, lam_ref, g_ref, o_ref, ka_scr, vt_scr, m_scr, acc_scr = refs
    p = pl.program_id(1)
    qi = pl.program_id(2)
    half = LANES // 2
    t_all = k_ref.shape[1]

    @pl.when(qi == 0)
    def _():
        ka_scr[:, 0:LANES] = k_ref[0].astype(BF16)
        if fox:
            cc = ccol_ref[0]
            hl = lax.broadcasted_iota(jnp.int32, cc.shape, 1)
            lane = lax.broadcasted_iota(jnp.int32, (t_all, LANES), 1)
            extra = jnp.zeros((t_all, LANES), F32)
            for c in range(2):
                fk = jnp.sum(jnp.where(hl == 2 * p + c, cc, 0.0), axis=1, keepdims=True) * LOG2E
                for i, piece in enumerate(_split3(fk)):
                    extra = jnp.where(lane == 3 * c + i, piece.astype(F32), extra)
            ka_scr[:, LANES:] = extra.astype(BF16)
        vt_scr[0:LANES, :] = v_ref[0].T.astype(BF16)
        r = lax.broadcasted_iota(jnp.int32, (_V_ROWS - LANES, t_all), 0)
        vt_scr[LANES:, :] = jnp.where(r == 0, 1.0, 0.0).astype(BF16)

    q = q_ref[0] * (FOX_DH ** -0.5 * LOG2E)
    lane = lax.broadcasted_iota(jnp.int32, q.shape, 1)
    members = []
    for c in range(2):
        x = jnp.where((lane < half) if c == 0 else (lane >= half), q, 0.0)
        if fox:
            pick = (lane >= 3 * c) & (lane < 3 * c + 3)
            x = jnp.concatenate([x, jnp.where(pick, -1.0, 0.0)], axis=1)
        members.append(x)
    qa = jnp.concatenate(members, axis=0).astype(BF16)
    if fox:
        fq = jnp.concatenate([crow_ref[0, pl.ds(2 * p + c, 1), :] for c in range(2)], axis=1) * LOG2E
    m_scr[...] = jnp.full(m_scr.shape, NEG, F32)
    acc_scr[...] = jnp.zeros_like(acc_scr)

    def scores(kb, masked):
        start = pl.multiple_of(kb * tq, tq)
        s = _dot_nt(ka_scr[pl.ds(start, tq), :], qa)
        if masked:
            krow = lax.broadcasted_iota(jnp.int32, (tq, tq), 0)
            qcol = lax.broadcasted_iota(jnp.int32, (tq, tq), 1)
            keep = krow <= qcol
            s = jnp.where(jnp.concatenate([keep, keep], axis=1), s, NEG)
        col_max = jnp.max(s, axis=0, keepdims=True)
        if fox:
            col_max = col_max + fq
        return start, s, col_max

    def accumulate(start, s, col_max):
        m_old = m_scr[...]
        m_new = jnp.maximum(m_old, col_max)
        shift = (m_new - fq) if fox else m_new
        pr = jnp.exp2(s - shift).astype(BF16)
        alpha = jnp.exp2(m_old - m_new)
        acc_scr[...] = alpha * acc_scr[...] + _dot(vt_scr[:, pl.ds(start, tq)], pr)
        m_scr[...] = m_new

    def group(kb0, n, last_masked):
        parts = [scores(kb0 + i, last_masked and i == n - 1) for i in range(n)]
        for part in parts:
            accumulate(*part)

    def body(i, carry):
        group(_ATTN_GROUP * i, _ATTN_GROUP, False)
        return carry

    lax.fori_loop(0, qi // _ATTN_GROUP, body, 0)
    rest = qi % _ATTN_GROUP
    for n_full in range(_ATTN_GROUP):
        @pl.when(rest == n_full)
        def _(n_full=n_full):
            group(qi - n_full, n_full + 1, True)

    acc = acc_scr[...]
    outs_t = [acc[0:LANES, c * tq:(c + 1) * tq] / acc[LANES:LANES + 1, c * tq:(c + 1) * tq] for c in range(2)]
    if fox:
        o_ref[0] = jnp.concatenate([outs_t[0][:half], outs_t[1][half:]], axis=0).T
    else:
        lp = lam_ref[...]
        lam = (jnp.exp(jnp.sum(lp[0:1] * lp[1:2], axis=1, keepdims=True))
               - jnp.exp(jnp.sum(lp[2:3] * lp[3:4], axis=1, keepdims=True)) + lam_init)
        o = (outs_t[0] - lam * outs_t[1]).T
        o_ref[0] = _rms(o, g_ref[...]) * (1.0 - lam_init)


def _attn_prompt(fox, q, kv, extra_a, extra_b, lam_init):
    b, t, w = q.shape
    groups = w // LANES
    tq = min(512, t)
    kernel = functools.partial(_attn_prompt_kernel, fox=fox, tq=tq, lam_init=lam_init)
    if fox:
        extra_specs = [pl.BlockSpec((1, t, FOX_HEADS), lambda i, p, j: (i, 0, 0)),
                       pl.BlockSpec((1, FOX_HEADS, tq), lambda i, p, j: (i, 0, j))]
    else:
        extra_specs = [pl.BlockSpec(extra_a.shape, lambda i, p, j: (0, 0)),
                       pl.BlockSpec(extra_b.shape, lambda i, p, j: (0, 0))]
    return pl.pallas_call(
        kernel,
        out_shape=jax.ShapeDtypeStruct((b, t, w), F32),
        grid=(b, groups, t // tq),
        in_specs=[
            pl.BlockSpec((1, tq, LANES), lambda i, p, j: (i, j, p)),
            pl.BlockSpec((1, t, LANES), lambda i, p, j: (i, 0, p)),
            pl.BlockSpec((1, t, LANES), lambda i, p, j: (i, 0, groups + p)),
        ] + extra_specs,
        out_specs=pl.BlockSpec((1, tq, LANES), lambda i, p, j: (i, j, p)),
        scratch_shapes=[pltpu.VMEM((t, 2 * LANES if fox else LANES), BF16), pltpu.VMEM((_V_ROWS, t), BF16),
                        pltpu.VMEM((1, 2 * tq), F32), pltpu.VMEM((_V_ROWS, 2 * tq), F32)],
        compiler_params=_params("parallel", "parallel", "arbitrary"),
        name="fox_prompt" if fox else "diff_prompt",
    )(q, kv, kv, extra_a, extra_b)


_PAGES_PER_STEP = 8


def _decode_kernel(pt_ref, qf_ref, qd_ref, nkf_ref, nkd_ref, nlf_ref, lam_ref, g_ref, sfx_ref, *rest,
                   pps, lam_init):
    del pt_ref
    page_refs = rest[:3 * pps]
    of_ref, od_ref = rest[3 * pps:3 * pps + 2]
    mf_scr, lf_scr, af_scr, md_scr, ld_scr, ad_scr, carry_scr = rest[3 * pps + 2:]
    b = pl.program_id(0)
    j = pl.program_id(1)
    scale = FOX_DH ** -0.5
    dj = 2 * DIFF_HEADS
    hd = 2 * DIFF_DH

    q_row = qf_ref[0]
    rf = lax.broadcasted_iota(jnp.int32, (FOX_HEADS, FOX_W), 0)
    lf_ = lax.broadcasted_iota(jnp.int32, (FOX_HEADS, FOX_W), 1)
    own = (lf_ // FOX_DH) == rf
    qblk = jnp.where(own, jnp.broadcast_to(q_row, (FOX_HEADS, FOX_W)), 0.0)
    qblk_b = qblk.astype(BF16)

    qd_row = qd_ref[0]
    r8 = lax.broadcasted_iota(jnp.int32, (dj, hd), 0)
    l8 = lax.broadcasted_iota(jnp.int32, (dj, hd), 1)
    head_of_row = r8 % DIFF_HEADS
    in_map = (l8 // DIFF_DH) == (r8 // DIFF_HEADS)

    def rows_from(vec, offset):
        out = jnp.zeros((dj, hd), F32)
        for h in range(DIFF_HEADS):
            piece = jnp.broadcast_to(vec[:, offset + h * hd:offset + (h + 1) * hd], (dj, hd))
            out = jnp.where(head_of_row == h, piece, out)
        return out

    q8 = jnp.where(in_map, rows_from(qd_row, 0), 0.0)
    q8_h = [jnp.where(head_of_row == h, q8, 0.0).astype(BF16) for h in range(DIFF_HEADS)]

    @pl.when(j == 0)
    def _():
        kvn = nkf_ref[0]
        kn = jnp.broadcast_to(kvn[:, 0:FOX_W], (FOX_HEADS, FOX_W))
        mf_scr[...] = jnp.broadcast_to(jnp.sum(qblk * kn, axis=1, keepdims=True) * scale, mf_scr.shape)
        lf_scr[...] = jnp.ones_like(lf_scr)
        af_scr[...] = jnp.broadcast_to(kvn[:, FOX_W:], (FOX_HEADS, FOX_W))
        kvd = nkd_ref[0]
        md_scr[...] = jnp.broadcast_to(
            jnp.sum(q8 * rows_from(kvd, 0), axis=1, keepdims=True) * scale, md_scr.shape)
        ld_scr[...] = jnp.ones_like(ld_scr)
        ad_scr[...] = rows_from(kvd, DIFF_W)
        nl = nlf_ref[...]
        sl = lax.broadcasted_iota(jnp.int32, nl.shape, 1)
        carry_scr[...] = jnp.broadcast_to(
            jnp.sum(jnp.where(sl == b, nl, 0.0), axis=1, keepdims=True), carry_scr.shape)

    carry = carry_scr[:, 0:1]
    s_f = []
    for i in range(pps):
        kt = page_refs[3 * i][0, 0].astype(BF16)
        lt = page_refs[3 * i + 1][0]
        hi, mid, lo = _split3(lt)
        r3 = _dot(jnp.concatenate([hi, mid, lo], axis=0), sfx_ref[...])
        bias = carry + r3[0:8] + r3[8:16] + r3[16:24]
        s_f.append(_dot(qblk_b, kt) * scale + bias)
        carry = carry + jnp.sum(lt, axis=1, keepdims=True)
    carry_scr[...] = jnp.broadcast_to(carry, carry_scr.shape)

    page = page_refs[2].shape[1] // dj
    s_d = []
    for i in range(pps):
        xd = page_refs[3 * i + 2]
        s = jnp.zeros((dj, page), F32)
        for h in range(DIFF_HEADS):
            k_h = xd[0, pl.ds(h, page, stride=dj), :].astype(BF16)
            s = s + _dot_nt(q8_h[h], k_h)
        s_d.append(s * scale)

    m_old = mf_scr[:, 0:1]
    m_new = jnp.maximum(m_old, jnp.max(functools.reduce(jnp.maximum, s_f), axis=1, keepdims=True))
    alpha = jnp.exp(m_old - m_new)
    l_add = jnp.zeros((FOX_HEADS, 1), F32)
    pv = jnp.zeros((FOX_HEADS, FOX_W), F32)
    for i in range(pps):
        pr = jnp.exp(s_f[i] - m_new)
        l_add = l_add + jnp.sum(pr, axis=1, keepdims=True)
        vt = page_refs[3 * i][0, 1].astype(BF16)
        pv = pv + _dot_nt(pr.astype(BF16), vt)
    lf_scr[...] = alpha * lf_scr[...] + l_add
    af_scr[...] = alpha * af_scr[...] + pv
    mf_scr[...] = jnp.broadcast_to(m_new, mf_scr.shape)

    md_old = md_scr[:, 0:1]
    md_new = jnp.maximum(md_old, jnp.max(functools.reduce(jnp.maximum, s_d), axis=1, keepdims=True))
    alphad = jnp.exp(md_old - md_new)
    ld_add = jnp.zeros((dj, 1), F32)
    pvd = jnp.zeros((dj, hd), F32)
    for i in range(pps):
        xd = page_refs[3 * i + 2]
        pr = jnp.exp(s_d[i] - md_new)
        ld_add = ld_add + jnp.sum(pr, axis=1, keepdims=True)
        for h in range(DIFF_HEADS):
            v_h = xd[0, pl.ds(DIFF_HEADS + h, page, stride=dj), :].astype(BF16)
            pvd = pvd + _dot(jnp.where(head_of_row == h, pr, 0.0).astype(BF16), v_h)
    ld_scr[...] = alphad * ld_scr[...] + ld_add
    ad_scr[...] = alphad * ad_scr[...] + pvd
    md_scr[...] = jnp.broadcast_to(md_new, md_scr.shape)

    @pl.when(j == pl.num_programs(1) - 1)
    def _():
        o_all = af_scr[...] / lf_scr[:, 0:1]
        of_ref[0] = jnp.sum(jnp.where(own, o_all, 0.0), axis=0, keepdims=True)
        od_all = ad_scr[...] / ld_scr[:, 0:1]
        lp = lam_ref[...]
        lam = (jnp.exp(jnp.sum(lp[0:1] * lp[1:2], axis=1, keepdims=True))
               - jnp.exp(jnp.sum(lp[2:3] * lp[3:4], axis=1, keepdims=True)) + lam_init)
        o = od_all[0:DIFF_HEADS] - lam * od_all[DIFF_HEADS:]
        od_ref[0] = _rms(o, g_ref[...]) * (1.0 - lam_init)


def _decode(page_table, qf, qd, new_kvf, new_kvd, new_lft, lam_p, subln_g, cache_ft, cache_lt, cache_d, lam_init):
    nb, n_pages = page_table.shape
    page = cache_lt.shape[2]
    pps = math.gcd(_PAGES_PER_STEP, n_pages)
    dj, hd = 2 * DIFF_HEADS, 2 * DIFF_DH
    kk = lax.broadcasted_iota(jnp.int32, (page, page), 0)
    kc = lax.broadcasted_iota(jnp.int32, (page, page), 1)
    sfx = jnp.where(kk > kc, 1.0, 0.0).astype(BF16)

    def page_idx(i):
        return lambda s, j, pt: pt[s, n_pages - 1 - (j * pps + i)]

    page_specs = []
    page_args = []
    for i in range(pps):
        pick = page_idx(i)
        page_specs += [
            pl.BlockSpec((1, 2, FOX_W, page), lambda s, j, pt, pick=pick: (pick(s, j, pt), 0, 0, 0)),
            pl.BlockSpec((1, FOX_HEADS, page), lambda s, j, pt, pick=pick: (pick(s, j, pt), 0, 0)),
            pl.BlockSpec((1, page * dj, hd), lambda s, j, pt, pick=pick: (pick(s, j, pt), 0, 0)),
        ]
        page_args += [cache_ft, cache_lt, cache_d]
    samp = lambda s, j, pt: (s, 0, 0)
    const = lambda s, j, pt: (0, 0)
    grid_spec = pltpu.PrefetchScalarGridSpec(
        num_scalar_prefetch=1,
        grid=(nb, n_pages // pps),
        in_specs=[
            pl.BlockSpec((1, 1, FOX_W), samp),
            pl.BlockSpec((1, 1, DIFF_W), samp),
            pl.BlockSpec((1, 1, 2 * FOX_W), samp),
            pl.BlockSpec((1, 1, 2 * DIFF_W), samp),
            pl.BlockSpec(new_lft.shape, const),
            pl.BlockSpec(lam_p.shape, const),
            pl.BlockSpec(subln_g.shape, const),
            pl.BlockSpec(sfx.shape, const),
        ] + page_specs,
        out_specs=[
            pl.BlockSpec((1, 1, FOX_W), samp),
            pl.BlockSpec((1, DIFF_HEADS, hd), samp),
        ],
        scratch_shapes=[
            pltpu.VMEM((FOX_HEADS, LANES), F32), pltpu.VMEM((FOX_HEADS, LANES), F32),
            pltpu.VMEM((FOX_HEADS, FOX_W), F32),
            pltpu.VMEM((dj, LANES), F32), pltpu.VMEM((dj, LANES), F32),
            pltpu.VMEM((dj, hd), F32),
            pltpu.VMEM((FOX_HEADS, LANES), F32),
        ],
    )
    return pl.pallas_call(
        functools.partial(_decode_kernel, pps=pps, lam_init=lam_init),
        out_shape=[jax.ShapeDtypeStruct((nb, 1, FOX_W), F32),
                   jax.ShapeDtypeStruct((nb, DIFF_HEADS, hd), F32)],
        grid_spec=grid_spec,
        compiler_params=_params("parallel", "arbitrary"),
        name="decode_attn",
    )(page_table, qf, qd, new_kvf, new_kvd, new_lft, lam_p, subln_g, sfx, *page_args)


def _rec_proj_kernel(x_ref, g_ref, w_ref, wgt_ref, bg_ref, q_ref, k_ref, v_ref, og_ref, u_ref, gt_ref):
    h = _rms(x_ref[0], g_ref[...]).astype(BF16)
    c0 = 0
    q_ref[0] = _dot(h, w_ref[:, c0:c0 + MLSTM_QK]) * (MLSTM_DK ** -0.5)
    c0 += MLSTM_QK
    k_ref[0] = _dot(h, w_ref[:, c0:c0 + MLSTM_QK])
    c0 += MLSTM_QK
    v_ref[0] = _dot(h, w_ref[:, c0:c0 + MLSTM_V])
    c0 += MLSTM_V
    og_ref[0] = _dot(h, w_ref[:, c0:c0 + MLSTM_V])
    c0 += MLSTM_V
    ua = _dot(h, w_ref[:, c0:c0 + CONV_CH])
    c0 += CONV_CH
    ub = _dot(h, w_ref[:, c0:c0 + CONV_CH])
    u_ref[0] = ua * jax.nn.sigmoid(ub)
    z = _dot_nt(wgt_ref[...], h) + bg_ref[...]
    rowi = lax.broadcasted_iota(jnp.int32, z.shape, 0)
    gt_ref[0] = jnp.where(rowi < MLSTM_HEADS, z, _log_sigmoid(z))


def _rec_proj(x, g, w_main, wgt, bg):
    b, t, d = x.shape
    tm = min(512, t)
    row = lambda i, j: (i, j, 0)
    const = lambda i, j: (0, 0)
    widths = [MLSTM_QK, MLSTM_QK, MLSTM_V, MLSTM_V, CONV_CH]
    return pl.pallas_call(
        _rec_proj_kernel,
        out_shape=[jax.ShapeDtypeStruct((b, t, w), F32) for w in widths]
        + [jax.ShapeDtypeStruct((b, 2 * MLSTM_HEADS, t), F32)],
        grid=(b, t // tm),
        in_specs=[
            pl.BlockSpec((1, tm, d), row),
            pl.BlockSpec((1, d), const),
            pl.BlockSpec(w_main.shape, const),
            pl.BlockSpec(wgt.shape, const),
            pl.BlockSpec(bg.shape, const),
        ],
        out_specs=[pl.BlockSpec((1, tm, w), row) for w in widths]
        + [pl.BlockSpec((1, 2 * MLSTM_HEADS, tm), lambda i, j: (i, 0, j))],
        compiler_params=_params("parallel", "parallel"),
        name="rec_proj",
    )(x, g.reshape(1, d), w_main, wgt, bg)


def _mlstm_prompt_kernel(q_ref, k_ref, v_ref, og_ref, gt_ref, gc_ref, ng_ref,
                         hm_ref, c_ref, n_ref, m_ref, ct_scr, n_scr, m_scr, *, chunk):
    t = pl.program_id(1)
    L = chunk
    half = LANES // 2

    @pl.when(t == 0)
    def _():
        ct_scr[...] = jnp.zeros_like(ct_scr)
        n_scr[...] = jnp.zeros_like(n_scr)
        m_scr[...] = jnp.full(m_scr.shape, NEG, F32)

    gt = gt_ref[0]
    gc = gc_ref[0]
    cum_rows = _dot3_left(gt, _tri(L, upper=True))
    cum_cols = _dot3_right(_tri(L, upper=False), gc)
    rr = lax.broadcasted_iota(jnp.int32, (L, L), 0)
    cc = lax.broadcasted_iota(jnp.int32, (L, L), 1)
    tri = cc <= rr
    lane = lax.broadcasted_iota(jnp.int32, (L, LANES), 1)

    heads = []
    for h in range(MLSTM_HEADS):
        pair, c = divmod(h, 2)
        qp = q_ref[0, :, pair * LANES:(pair + 1) * LANES]
        kp = k_ref[0, :, pair * LANES:(pair + 1) * LANES]
        mine = (lane < half) if c == 0 else (lane >= half)
        qm = jnp.where(mine, qp, 0.0)
        qm_b = qm.astype(BF16)
        qk = _dot_nt(qm_b, kp.astype(BF16))
        qc = _dot(qm_b, ct_scr[pair].astype(BF16))
        heads.append(dict(pair=pair, c=c, kp=kp, qm=qm, qk=qk, qc=qc))

    for h, hd_ in enumerate(heads):
        a_col = cum_cols[:, MLSTM_HEADS + h:MLSTM_HEADS + h + 1]
        ig_col = gc[:, h:h + 1]
        b_row = gt[h:h + 1, :] - cum_rows[MLSTM_HEADS + h:MLSTM_HEADS + h + 1, :]
        m_prev = m_scr[h:h + 1, 0:1]
        d = jnp.where(tri, a_col + b_row, NEG)
        m_inter = a_col + m_prev
        m_t = jnp.maximum(jnp.max(d, axis=1, keepdims=True), m_inter)
        s = hd_["qk"] * jnp.exp(d - m_t)
        w_inter = jnp.exp(m_inter - m_t)
        m_new = m_t[L - 1:L, :]
        cum_last = a_col[L - 1:L, :]
        decay = jnp.exp(cum_last + m_prev - m_new)
        w_key = jnp.exp(cum_last - a_col + ig_col - m_new)
        vh = v_ref[0, :, h * MLSTM_DV:(h + 1) * MLSTM_DV]
        hd_.update(s=s, w_inter=w_inter, m_t=m_t, m_new=m_new, decay=decay, w_key=w_key, vh=vh)

    for hd_ in heads:
        hd_["sv"] = _dot(hd_["s"].astype(BF16), hd_["vh"].astype(BF16))
        kt_b = hd_["kp"].T.astype(BF16)
        hd_["upd"] = _dot(kt_b, (hd_["w_key"] * hd_["vh"]).astype(BF16))

    lane1 = lax.broadcasted_iota(jnp.int32, (1, LANES), 1)
    for pair in range(MLSTM_HEADS // 2):
        ct = ct_scr[pair]
        n_row = n_scr[pair]
        ct_new = []
        n_new = []
        for c in range(2):
            h = 2 * pair + c
            hd_ = heads[h]
            num = hd_["sv"] + hd_["w_inter"] * hd_["qc"]
            den = (jnp.sum(hd_["s"], axis=1, keepdims=True)
                   + hd_["w_inter"] * jnp.sum(hd_["qm"] * n_row, axis=1, keepdims=True))
            hout = num / jnp.maximum(jnp.abs(den), jnp.exp(-hd_["m_t"]))
            gh = ng_ref[:, h * MLSTM_DV:(h + 1) * MLSTM_DV]
            oh = og_ref[0, :, h * MLSTM_DV:(h + 1) * MLSTM_DV]
            hm_ref[0, :, h * MLSTM_DV:(h + 1) * MLSTM_DV] = _rms(hout, gh) * jax.nn.sigmoid(oh)
            lo, hi_ = c * half, (c + 1) * half
            ct_new.append(hd_["decay"] * ct[lo:hi_] + hd_["upd"][lo:hi_])
            n_new.append(hd_["decay"] * n_row + jnp.sum(hd_["w_key"] * hd_["kp"], axis=0, keepdims=True))
            m_scr[h:h + 1, :] = jnp.broadcast_to(hd_["m_new"], (1, LANES))
        ct_scr[pair] = jnp.concatenate(ct_new, axis=0)
        n_scr[pair] = jnp.where(lane1 < half, n_new[0], n_new[1])

    @pl.when(t == pl.num_programs(1) - 1)
    def _():
        for pair in range(MLSTM_HEADS // 2):
            c_pair = ct_scr[pair].T
            n_row = n_scr[pair]
            for c in range(2):
                h = 2 * pair + c
                c_ref[0, h] = c_pair[:, c * half:(c + 1) * half]
                n_ref[0, h:h + 1, :] = n_row[:, c * half:(c + 1) * half]
                m_ref[0, :, h:h + 1] = m_scr[h:h + 1, 0:1]


def _mlstm_prompt(q, k, v, og, gt, gc, norm_g):
    b, t, _ = q.shape
    chunk = min(256, t)
    row = lambda i, j: (i, j, 0)
    const = lambda i, j: (0, 0)
    return pl.pallas_call(
        functools.partial(_mlstm_prompt_kernel, chunk=chunk),
        out_shape=[
            jax.ShapeDtypeStruct((b, t, MLSTM_V), F32),
            jax.ShapeDtypeStruct((b, MLSTM_HEADS, MLSTM_DV, MLSTM_DK), F32),
            jax.ShapeDtypeStruct((b, MLSTM_HEADS, MLSTM_DK), F32),
            jax.ShapeDtypeStruct((b, 1, MLSTM_HEADS), F32),
        ],
        grid=(b, t // chunk),
        in_specs=[
            pl.BlockSpec((1, chunk, MLSTM_QK), row),
            pl.BlockSpec((1, chunk, MLSTM_QK), row),
            pl.BlockSpec((1, chunk, MLSTM_V), row),
            pl.BlockSpec((1, chunk, MLSTM_V), row),
            pl.BlockSpec((1, 2 * MLSTM_HEADS, chunk), lambda i, j: (i, 0, j)),
            pl.BlockSpec((1, chunk, 2 * MLSTM_HEADS), row),
            pl.BlockSpec((1, MLSTM_V), const),
        ],
        out_specs=[
            pl.BlockSpec((1, chunk, MLSTM_V), row),
            pl.BlockSpec((1, MLSTM_HEADS, MLSTM_DV, MLSTM_DK), lambda i, j: (i, 0, 0, 0)),
            pl.BlockSpec((1, MLSTM_HEADS, MLSTM_DK), lambda i, j: (i, 0, 0)),
            pl.BlockSpec((1, 1, MLSTM_HEADS), lambda i, j: (i, 0, 0)),
        ],
        scratch_shapes=[
            pltpu.VMEM((MLSTM_HEADS // 2, LANES, MLSTM_DV), F32),
            pltpu.VMEM((MLSTM_HEADS // 2, 1, LANES), F32),
            pltpu.VMEM((SUBLANES, LANES), F32),
        ],
        compiler_params=_params("parallel", "arbitrary"),
        name="mlstm_prompt",
    )(q, k, v, og, gt, gc, norm_g.reshape(1, MLSTM_V))


def _mlstm_sample_kernel(q_ref, k_ref, v_ref, og_ref, gc_ref, m_ref, kall_ref, vall_ref, ng_ref, c_ref, n_ref,
                         hm_ref, co_ref, no_ref, mo_ref, vt_scr, *, bs):
    i = pl.program_id(0)
    nb = kall_ref.shape[0]

    @pl.when(i == 0)
    def _():
        for h in range(MLSTM_HEADS):
            vt_scr[h] = vall_ref[:, h * MLSTM_DV:(h + 1) * MLSTM_DV].T

    rows = lax.broadcasted_iota(jnp.int32, (nb, MLSTM_DK), 0)

    def body(bl, _):
        b = i * bs + bl
        q_row = q_ref[bl]
        k_row = k_ref[bl]
        v_row = v_ref[bl]
        og_row = og_ref[bl]
        g_row = gc_ref[bl]
        m_row = m_ref[bl]
        for h in range(MLSTM_HEADS):
            qh = q_row[:, h * MLSTM_DK:(h + 1) * MLSTM_DK]
            kh = k_row[:, h * MLSTM_DK:(h + 1) * MLSTM_DK]
            vh = v_row[:, h * MLSTM_DV:(h + 1) * MLSTM_DV]
            it = g_row[:, h:h + 1]
            lf = g_row[:, MLSTM_HEADS + h:MLSTM_HEADS + h + 1]
            m_prev = m_row[:, h:h + 1]
            m_inter = lf + m_prev
            m_t = jnp.maximum(it, m_inter)
            e_i = jnp.exp(it - m_t)
            w_inter = jnp.exp(m_inter - m_t)
            s = jnp.sum(qh * kh, axis=1, keepdims=True) * e_i
            cm = c_ref[bl, h]
            nh = n_ref[bl, h:h + 1, :]
            num = s * vh + w_inter * _dot_nt(qh.astype(BF16), cm.astype(BF16))
            den = s + w_inter * jnp.sum(nh * qh, axis=1, keepdims=True)
            hrow = num / jnp.maximum(jnp.abs(den), jnp.exp(-m_t))
            gh = ng_ref[:, h * MLSTM_DV:(h + 1) * MLSTM_DV]
            oh = og_row[:, h * MLSTM_DV:(h + 1) * MLSTM_DV]
            hm_ref[bl, :, h * MLSTM_DV:(h + 1) * MLSTM_DV] = _rms(hrow, gh) * jax.nn.sigmoid(oh)
            k_all = kall_ref[:, h * MLSTM_DK:(h + 1) * MLSTM_DK]
            k_sel = jnp.where(rows == b, k_all, 0.0).astype(BF16)
            outer = _dot(vt_scr[h].astype(BF16), k_sel)
            co_ref[bl, h] = w_inter * cm + e_i * outer
            no_ref[bl, h:h + 1, :] = w_inter * nh + e_i * kh
            mo_ref[bl, :, h:h + 1] = m_t
        return 0

    lax.fori_loop(0, bs, body, 0)


def _mlstm_sample(q, k, v, og, gc, norm_g, c0, n0, m0):
    nb = q.shape[0]
    bs = min(16, nb)
    full = lambda i: (0, 0)
    rows = lambda i: (i, 0, 0)
    per_sample = [a.reshape(nb, 1, a.shape[1]) for a in (q, k, v, og, gc, m0)]
    hm, c1, n1, m1 = pl.pallas_call(
        functools.partial(_mlstm_sample_kernel, bs=bs),
        out_shape=[
            jax.ShapeDtypeStruct((nb, 1, MLSTM_V), F32),
            jax.ShapeDtypeStruct(c0.shape, F32),
            jax.ShapeDtypeStruct(n0.shape, F32),
            jax.ShapeDtypeStruct((nb, 1, MLSTM_HEADS), F32),
        ],
        grid=(nb // bs,),
        in_specs=[pl.BlockSpec((bs, 1, a.shape[2]), rows) for a in per_sample] + [
            pl.BlockSpec(k.shape, full),
            pl.BlockSpec(v.shape, full),
            pl.BlockSpec((1, MLSTM_V), full),
            pl.BlockSpec((bs, MLSTM_HEADS, MLSTM_DV, MLSTM_DK), lambda i: (i, 0, 0, 0)),
            pl.BlockSpec((bs, MLSTM_HEADS, MLSTM_DK), rows),
        ],
        out_specs=[
            pl.BlockSpec((bs, 1, MLSTM_V), rows),
            pl.BlockSpec((bs, MLSTM_HEADS, MLSTM_DV, MLSTM_DK), lambda i: (i, 0, 0, 0)),
            pl.BlockSpec((bs, MLSTM_HEADS, MLSTM_DK), rows),
            pl.BlockSpec((bs, 1, MLSTM_HEADS), rows),
        ],
        scratch_shapes=[pltpu.VMEM((MLSTM_HEADS, MLSTM_DV, nb), F32)],
        compiler_params=_params("arbitrary"),
        name="mlstm_sample",
    )(*per_sample, k, v, norm_g.reshape(1, MLSTM_V), c0, n0)
    return hm.reshape(nb, MLSTM_V), c1, n1, m1.reshape(nb, MLSTM_HEADS)


_CONV_HALO = 32
_CONV_ROWS = 32


def _ln_silu(y, g, b):
    yc = y - jnp.mean(y, axis=-1, keepdims=True)
    var = jnp.mean(yc * yc, axis=-1, keepdims=True)
    z = yc * lax.rsqrt(var + NORM_EPS) * g + b
    return z * jax.nn.sigmoid(z)


def _conv_prompt_kernel(u_ref, prev_ref, w_ref, b_ref, g_ref, be_ref, c_ref, st_ref, full_scr, shift_scr, *, tt):
    t = pl.program_id(1)
    pad = _CONV_HALO - (CONV_W - 1)

    @pl.when(t == 0)
    def _():
        full_scr[0:_CONV_HALO, :] = jnp.zeros((_CONV_HALO, CONV_CH), F32)
        full_scr[pad:_CONV_HALO, :] = prev_ref[0]

    full_scr[_CONV_HALO:, :] = u_ref[0]
    span = tt + _CONV_HALO - SUBLANES
    for r in range(1, SUBLANES):
        shift_scr[r, 0:span, :] = full_scr[r:r + span, :]
    for r0 in range(0, tt, _CONV_ROWS):
        acc = jnp.broadcast_to(b_ref[...], (_CONV_ROWS, CONV_CH))
        for k in range(CONV_W):
            off = r0 + pad + k
            r = off % SUBLANES
            if r == 0:
                rows = full_scr[off:off + _CONV_ROWS, :]
            else:
                rows = shift_scr[r, off - r:off - r + _CONV_ROWS, :]
            acc = acc + w_ref[k:k + 1, :] * rows
        c_ref[0, r0:r0 + _CONV_ROWS, :] = _ln_silu(acc, g_ref[...], be_ref[...])

    @pl.when(t == pl.num_programs(1) - 1)
    def _():
        st_ref[0] = full_scr[_CONV_HALO + tt - (CONV_W - 1):, :]

    full_scr[0:_CONV_HALO, :] = full_scr[tt:tt + _CONV_HALO, :]


def _conv_prompt(u, prev, w, bias, ln_g, ln_b):
    b, t, ch = u.shape
    tt = min(256, t)
    const = lambda i, j: (0, 0)
    return pl.pallas_call(
        functools.partial(_conv_prompt_kernel, tt=tt),
        out_shape=[jax.ShapeDtypeStruct((b, t, ch), F32),
                   jax.ShapeDtypeStruct((b, CONV_W - 1, ch), F32)],
        grid=(b, t // tt),
        in_specs=[
            pl.BlockSpec((1, tt, ch), lambda i, j: (i, j, 0)),
            pl.BlockSpec((1, CONV_W - 1, ch), lambda i, j: (i, 0, 0)),
            pl.BlockSpec((CONV_W, ch), const),
            pl.BlockSpec((1, ch), const),
            pl.BlockSpec((1, ch), const),
            pl.BlockSpec((1, ch), const),
        ],
        out_specs=[pl.BlockSpec((1, tt, ch), lambda i, j: (i, j, 0)),
                   pl.BlockSpec((1, CONV_W - 1, ch), lambda i, j: (i, 0, 0))],
        scratch_shapes=[pltpu.VMEM((_CONV_HALO + tt, ch), F32),
                        pltpu.VMEM((SUBLANES, _CONV_HALO + tt, ch), F32)],
        compiler_params=_params("parallel", "arbitrary"),
        name="conv_prompt",
    )(u, prev, w, bias.reshape(1, ch), ln_g.reshape(1, ch), ln_b.reshape(1, ch))


def _conv_sample_kernel(u_ref, prev_ref, w_ref, b_ref, g_ref, be_ref, c_ref, st_ref):
    hist = CONV_W - 1
    u = u_ref[...]
    y = w_ref[hist:hist + 1, :] * u + b_ref[...]
    for k in range(hist):
        row = prev_ref[k]
        y = y + w_ref[k:k + 1, :] * row
        if k > 0:
            st_ref[k - 1] = row
    st_ref[hist - 1] = u
    c_ref[...] = _ln_silu(y, g_ref[...], be_ref[...])


def _conv_sample(u, prev_t, w, bias, ln_g, ln_b):
    nb, ch = u.shape
    bs = min(32, nb)
    hist = CONV_W - 1
    const = lambda i: (0, 0)
    return pl.pallas_call(
        _conv_sample_kernel,
        out_shape=[jax.ShapeDtypeStruct((nb, ch), F32),
                   jax.ShapeDtypeStruct((hist, nb, ch), F32)],
        grid=(nb // bs,),
        in_specs=[
            pl.BlockSpec((bs, ch), lambda i: (i, 0)),
            pl.BlockSpec((hist, bs, ch), lambda i: (0, i, 0)),
            pl.BlockSpec((CONV_W, ch), const),
            pl.BlockSpec((1, ch), const),
            pl.BlockSpec((1, ch), const),
            pl.BlockSpec((1, ch), const),
        ],
        out_specs=[pl.BlockSpec((bs, ch), lambda i: (i, 0)),
                   pl.BlockSpec((hist, bs, ch), lambda i: (0, i, 0))],
        compiler_params=_params("parallel"),
        name="conv_sample",
    )(u, prev_t, w, bias.reshape(1, ch), ln_g.reshape(1, ch), ln_b.reshape(1, ch))


def kernel(x_prompt, x_sample, cache_fox_kv, cache_fox_logf, cache_diff_kv, state_mlstm_C, state_mlstm_n,
           state_mlstm_m, state_conv, page_table, norm_g, final_g, ffn_w_in, ffn_w_out, attn_w_in, attn_b_f,
           diff_lam, diff_subln_g, attn_w_out, rec_w_in, rec_b_i, rec_b_f, mlstm_norm_g, conv_w, conv_b,
           conv_ln_g, conv_ln_b, rec_w_out):
    bp, t, d = x_prompt.shape
    nb = x_sample.shape[0]
    depth = norm_g.shape[0]
    n_pool, page = cache_fox_kv.shape[1], cache_fox_kv.shape[2]
    xp = x_prompt.reshape(bp * t, d)
    xs = x_sample.reshape(nb, d)

    outs = {k: [] for k in ("fkv_p", "fkv_s", "flf_p", "flf_s", "dkv_p", "dkv_s",
                            "c_p", "c_s", "n_p", "n_s", "m_p", "m_s", "cv_p", "cv_s")}
    for l in range(depth):
        j = l // 2
        w_in_a, w_out_a = ffn_w_in[l, 0].astype(BF16), ffn_w_out[l, 0].astype(BF16)
        w_in_b, w_out_b = ffn_w_in[l, 1].astype(BF16), ffn_w_out[l, 1].astype(BF16)
        xp = _ffn(xp, norm_g[l, 0], w_in_a, w_out_a)
        xs = _ffn(xs, norm_g[l, 0], w_in_a, w_out_a)
        if l % 2 == 0:
            lam_init = 0.8 - 0.6 * math.exp(-0.3 * l)
            w = attn_w_in[j]
            c1, c2 = 3 * FOX_W, 3 * FOX_W + FOX_HEADS
            w_main = jnp.concatenate([w[:, :c1], w[:, c2:]], axis=1).astype(BF16)
            wft = w[:, c1:c2].T.astype(BF16)
            bf = attn_b_f[j].reshape(FOX_HEADS, 1)
            w_o = attn_w_out[j].astype(BF16)
            qf, kvf, qd, kvd, lft, cumt = _attn_proj(xp.reshape(bp, t, d), norm_g[l, 1], w_main, wft, bf)
            o_f = _attn_prompt(True, qf, kvf, jnp.swapaxes(cumt, 1, 2), cumt, lam_init)
            o_d = _attn_prompt(False, qd, kvd, diff_lam[j], diff_subln_g[j].reshape(1, -1), lam_init)
            xp = _merge(xp, o_f.reshape(bp * t, FOX_W), o_d.reshape(bp * t, DIFF_W), w_o)
            outs["fkv_p"].append(kvf.reshape(bp, t, 2, FOX_HEADS, FOX_DH))
            outs["flf_p"].append(jnp.swapaxes(lft, 1, 2))
            outs["dkv_p"].append(kvd.reshape(bp, t, 2, DIFF_HEADS, 2 * DIFF_DH))
            sqf, skvf, sqd, skvd, slft, _ = _attn_proj(xs.reshape(1, nb, d), norm_g[l, 1], w_main, wft, bf)
            so_f, so_d = _decode(
                page_table,
                sqf.reshape(nb, 1, FOX_W), sqd.reshape(nb, 1, DIFF_W),
                skvf.reshape(nb, 1, 2 * FOX_W), skvd.reshape(nb, 1, 2 * DIFF_W),
                slft[0], diff_lam[j], diff_subln_g[j].reshape(1, -1),
                jnp.transpose(cache_fox_kv[j], (0, 2, 3, 4, 1)).reshape(n_pool, 2, FOX_W, page),
                jnp.swapaxes(cache_fox_logf[j], 1, 2),
                cache_diff_kv[j].reshape(n_pool, page * 2 * DIFF_HEADS, 2 * DIFF_DH),
                lam_init)
            xs = _merge(xs, so_f.reshape(nb, FOX_W), so_d.reshape(nb, DIFF_W), w_o)
            outs["fkv_s"].append(skvf.reshape(nb, 1, 2, FOX_HEADS, FOX_DH))
            outs["flf_s"].append(jnp.swapaxes(slft, 1, 2).reshape(nb, 1, FOX_HEADS))
            outs["dkv_s"].append(skvd.reshape(nb, 1, 2, DIFF_HEADS, 2 * DIFF_DH))
        else:
            w = rec_w_in[j]
            c1 = 2 * MLSTM_QK + MLSTM_V
            c2 = c1 + 2 * MLSTM_HEADS
            w_main = jnp.concatenate([w[:, :c1], w[:, c2:]], axis=1).astype(BF16)
            wgt = w[:, c1:c2].T.astype(BF16)
            bg = jnp.concatenate([rec_b_i[j], rec_b_f[j]]).reshape(2 * MLSTM_HEADS, 1)
            w_o = rec_w_out[j].astype(BF16)
            q, k, v, og, u, gt = _rec_proj(xp.reshape(bp, t, d), norm_g[l, 1], w_main, wgt, bg)
            hm, c_p, n_p, m_p = _mlstm_prompt(q, k, v, og, gt, jnp.swapaxes(gt, 1, 2), mlstm_norm_g[j])
            cv, st_p = _conv_prompt(u, jnp.zeros((bp, CONV_W - 1, CONV_CH), F32), conv_w[j], conv_b[j],
                                    conv_ln_g[j], conv_ln_b[j])
            xp = _merge(xp, hm.reshape(bp * t, MLSTM_V), cv.reshape(bp * t, CONV_CH), w_o)
            outs["c_p"].append(c_p)
            outs["n_p"].append(n_p)
            outs["m_p"].append(m_p.reshape(bp, MLSTM_HEADS))
            outs["cv_p"].append(st_p)
            sq, sk, sv, sog, su, sgt = _rec_proj(xs.reshape(1, nb, d), norm_g[l, 1], w_main, wgt, bg)
            shm, c_s, n_s, m_s = _mlstm_sample(sq[0], sk[0], sv[0], sog[0], sgt[0].T, mlstm_norm_g[j],
                                               state_mlstm_C[j], state_mlstm_n[j], state_mlstm_m[j])
            scv, st_s = _conv_sample(su[0], jnp.swapaxes(state_conv[j], 0, 1), conv_w[j], conv_b[j],
                                     conv_ln_g[j], conv_ln_b[j])
            xs = _merge(xs, shm, scv, w_o)
            outs["c_s"].append(c_s)
            outs["n_s"].append(n_s)
            outs["m_s"].append(m_s)
            outs["cv_s"].append(jnp.swapaxes(st_s, 0, 1))
        last = l == depth - 1
        xp = _ffn(xp, norm_g[l, 2], w_in_b, w_out_b, final_g if last else None)
        xs = _ffn(xs, norm_g[l, 2], w_in_b, w_out_b, final_g if last else None)

    st = jnp.stack
    return (xp.reshape(bp, t, d), xs.reshape(nb, 1, d),
            st(outs["fkv_p"]), st(outs["fkv_s"]), st(outs["flf_p"]), st(outs["flf_s"]),
            st(outs["dkv_p"]), st(outs["dkv_s"]),
            st(outs["c_p"]), st(outs["c_s"]), st(outs["n_p"]), st(outs["n_s"]),
            st(outs["m_p"]), st(outs["m_s"]), st(outs["cv_p"]), st(outs["cv_s"]))
```

```python
import functools
import math

import jax
import jax.numpy as jnp
from jax import lax
from jax.experimental import pallas as pl
from jax.experimental.pallas import tpu as pltpu

F32 = jnp.float32
BF16 = jnp.bfloat16
NORM_EPS = 1e-6
NEG = -1e30
LOG2E = 1.4426950408889634

FOX_HEADS = 8
FOX_DH = 64
DIFF_HEADS = 4
DIFF_DH = 64
MLSTM_HEADS = 4
MLSTM_DK = 64
MLSTM_DV = 128
CONV_CH = 512
CONV_W = 31
FOX_W = FOX_HEADS * FOX_DH
DIFF_W = DIFF_HEADS * 2 * DIFF_DH
MLSTM_QK = MLSTM_HEADS * MLSTM_DK
MLSTM_V = MLSTM_HEADS * MLSTM_DV

LANES = 128
SUBLANES = 8
VMEM_LIMIT_BYTES = 56 * 1024 * 1024

_NT = (((1,), (1,)), ((), ()))


def _params(*sem):
    return pltpu.CompilerParams(dimension_semantics=sem, vmem_limit_bytes=VMEM_LIMIT_BYTES)


def _rms(x, g):
    return x * lax.rsqrt(jnp.mean(x * x, axis=-1, keepdims=True) + NORM_EPS) * g


def _log_sigmoid(z):
    return jnp.minimum(z, 0.0) - jnp.log(1.0 + jnp.exp(-jnp.abs(z)))


def _split3(x):
    hi = x.astype(BF16)
    r = x - hi.astype(F32)
    mid = r.astype(BF16)
    lo = (r - mid.astype(F32)).astype(BF16)
    return hi, mid, lo


def _dot(a, b):
    return jnp.dot(a, b, preferred_element_type=F32)


def _dot_nt(a, b):
    return lax.dot_general(a, b, _NT, preferred_element_type=F32)


def _dot3_left(x, t):
    hi, mid, lo = _split3(x)
    return _dot(hi, t) + _dot(mid, t) + _dot(lo, t)


def _dot3_right(t, x):
    hi, mid, lo = _split3(x)
    return _dot(t, hi) + _dot(t, mid) + _dot(t, lo)


def _tri(n, upper):
    r = lax.broadcasted_iota(jnp.int32, (n, n), 0)
    c = lax.broadcasted_iota(jnp.int32, (n, n), 1)
    keep = (r <= c) if upper else (r >= c)
    return jnp.where(keep, 1.0, 0.0).astype(BF16)


def _ffn_kernel(x_ref, g_ref, wa_ref, wb_ref, wo_ref, fg_ref, o_ref, h_scr, gated_scr, *, final_norm):
    j = pl.program_id(1)
    tf = wa_ref.shape[1]

    @pl.when(j == 0)
    def _():
        h_scr[...] = _rms(x_ref[...], g_ref[...]).astype(BF16)

    h = h_scr[...]
    a = _dot(h, wa_ref[...])
    b = _dot(h, wb_ref[...])
    gated_scr[:, pl.ds(pl.multiple_of(j * tf, tf), tf)] = (a * jax.nn.sigmoid(a) * b).astype(BF16)

    @pl.when(j == pl.num_programs(1) - 1)
    def _():
        y = x_ref[...] + 0.5 * _dot(gated_scr[...], wo_ref[...])
        if final_norm:
            y = _rms(y, fg_ref[...])
        o_ref[...] = y


def _ffn(x, g, w_in, w_out, layer, which, final_g=None):
    m, d = x.shape
    f = w_out.shape[2]
    tm = min(1024, m)
    tf = 256 if m > 256 else f // 2
    nf = f // tf
    assert m % tm == 0 and f % tf == 0 and tf % LANES == 0, (m, f)
    fg = jnp.ones((1, d), F32) if final_g is None else final_g.reshape(1, d)
    return pl.pallas_call(
        functools.partial(_ffn_kernel, final_norm=final_g is not None),
        out_shape=jax.ShapeDtypeStruct((m, d), F32),
        grid=(m // tm, nf),
        in_specs=[
            pl.BlockSpec((tm, d), lambda i, j: (i, 0)),
            pl.BlockSpec((1, d), lambda i, j: (0, 0)),
            pl.BlockSpec((None, None, d, tf), lambda i, j: (layer, which, 0, j)),
            pl.BlockSpec((None, None, d, tf), lambda i, j: (layer, which, 0, j + nf)),
            pl.BlockSpec((None, None, f, d), lambda i, j: (layer, which, 0, 0)),
            pl.BlockSpec((1, d), lambda i, j: (0, 0)),
        ],
        out_specs=pl.BlockSpec((tm, d), lambda i, j: (i, 0)),
        scratch_shapes=[pltpu.VMEM((tm, d), BF16), pltpu.VMEM((tm, f), BF16)],
        compiler_params=_params("parallel", "arbitrary"),
        name="ffn",
    )(x, g.reshape(1, d), w_in, w_in, w_out, fg)


def _merge_kernel(x_ref, oa_ref, ob_ref, w_ref, o_ref):
    wa = oa_ref.shape[-1]
    y = _dot(oa_ref[...].astype(BF16), w_ref[:wa, :])
    y += _dot(ob_ref[...].astype(BF16), w_ref[wa:, :])
    o_ref[...] = x_ref[...] + y


def _merge(x, oa, ob, w_out):
    m, d = x.shape
    tm = min(1024, m)
    wa, wb = oa.shape[1], ob.shape[1]
    return pl.pallas_call(
        _merge_kernel,
        out_shape=jax.ShapeDtypeStruct((m, d), F32),
        grid=(m // tm,),
        in_specs=[
            pl.BlockSpec((tm, d), lambda i: (i, 0)),
            pl.BlockSpec((tm, wa), lambda i: (i, 0)),
            pl.BlockSpec((tm, wb), lambda i: (i, 0)),
            pl.BlockSpec((wa + wb, d), lambda i: (0, 0)),
        ],
        out_specs=pl.BlockSpec((tm, d), lambda i: (i, 0)),
        compiler_params=_params("parallel"),
        name="merge",
    )(x, oa, ob, w_out)


def _attn_proj_kernel(x_ref, g_ref, w_ref, wft_ref, bf_ref, qf_ref, kvf_ref, qd_ref, kvd_ref,
                      lf_ref, cum_ref, kvdr_ref, carry_scr):
    t = pl.program_id(1)
    h = _rms(x_ref[0], g_ref[...]).astype(BF16)
    qf_ref[0] = _dot(h, w_ref[:, 0:FOX_W])
    kvf_ref[0] = _dot(h, w_ref[:, FOX_W:3 * FOX_W])
    qd_ref[0] = _dot(h, w_ref[:, 3 * FOX_W:3 * FOX_W + DIFF_W])
    kvd = _dot(h, w_ref[:, 3 * FOX_W + DIFF_W:])
    kvd_ref[0] = kvd
    groups = kvd.shape[1] // LANES
    for gi in range(groups):
        kvdr_ref[0, pl.ds(gi, kvd.shape[0], stride=groups), :] = kvd[:, gi * LANES:(gi + 1) * LANES]
    logf = _log_sigmoid(_dot_nt(wft_ref[...], h) + bf_ref[...])
    lf_ref[0] = logf

    @pl.when(t == 0)
    def _():
        carry_scr[...] = jnp.zeros_like(carry_scr)

    tm = logf.shape[1]
    cum = _dot3_left(logf, _tri(tm, upper=True)) + carry_scr[:, 0:1]
    cum_ref[0] = cum
    carry_scr[...] = jnp.broadcast_to(cum[:, tm - 1:tm], carry_scr.shape)


def _attn_proj(x, g, w_main, wft, bf):
    b, t, d = x.shape
    tm = min(512, t)
    n_main = w_main.shape[1]
    row = lambda i, j: (i, j, 0)
    col = lambda i, j: (i, 0, j)
    const = lambda i, j: (0, 0)
    return pl.pallas_call(
        _attn_proj_kernel,
        out_shape=[
            jax.ShapeDtypeStruct((b, t, FOX_W), F32),
            jax.ShapeDtypeStruct((b, t, 2 * FOX_W), F32),
            jax.ShapeDtypeStruct((b, t, DIFF_W), F32),
            jax.ShapeDtypeStruct((b, t, 2 * DIFF_W), F32),
            jax.ShapeDtypeStruct((b, FOX_HEADS, t), F32),
            jax.ShapeDtypeStruct((b, FOX_HEADS, t), F32),
            jax.ShapeDtypeStruct((b, t * (2 * DIFF_W // LANES), LANES), F32),
        ],
        grid=(b, t // tm),
        in_specs=[
            pl.BlockSpec((1, tm, d), row),
            pl.BlockSpec((1, d), const),
            pl.BlockSpec((d, n_main), const),
            pl.BlockSpec((FOX_HEADS, d), const),
            pl.BlockSpec((FOX_HEADS, 1), const),
        ],
        out_specs=[
            pl.BlockSpec((1, tm, FOX_W), row),
            pl.BlockSpec((1, tm, 2 * FOX_W), row),
            pl.BlockSpec((1, tm, DIFF_W), row),
            pl.BlockSpec((1, tm, 2 * DIFF_W), row),
            pl.BlockSpec((1, FOX_HEADS, tm), col),
            pl.BlockSpec((1, FOX_HEADS, tm), col),
            pl.BlockSpec((1, tm * (2 * DIFF_W // LANES), LANES), row),
        ],
        scratch_shapes=[pltpu.VMEM((FOX_HEADS, LANES), F32)],
        compiler_params=_params("parallel", "arbitrary"),
        name="attn_proj",
    )(x, g.reshape(1, d), w_main, wft, bf)


_V_ROWS = 144
_ATTN_GROUP = 4


def _attn_prompt_kernel(*refs, fox, tq, lam_init):
    if fox:
        q_ref, k_ref, v_ref, ccol_ref, crow_ref, o_ref, ka_scr, vt_scr, m_scr, acc_scr = refs
    else:
        q_ref, k_ref, v_ref, lam_ref, g_ref, o_ref, ka_scr, vt_scr, m_scr, acc_scr = refs
    p = pl.program_id(1)
    qi = pl.program_id(2)
    half = LANES // 2
    t_all = k_ref.shape[1]

    @pl.when(qi == 0)
    def _():
        ka_scr[:, 0:LANES] = k_ref[0].astype(BF16)
        if fox:
            cc = ccol_ref[0]
            hl = lax.broadcasted_iota(jnp.int32, cc.shape, 1)
            lane = lax.broadcasted_iota(jnp.int32, (t_all, LANES), 1)
            extra = jnp.zeros((t_all, LANES), F32)
            for c in range(2):
                fk = jnp.sum(jnp.where(hl == 2 * p + c, cc, 0.0), axis=1, keepdims=True) * LOG2E
                for i, piece in enumerate(_split3(fk)):
                    extra = jnp.where(lane == 3 * c + i, piece.astype(F32), extra)
            ka_scr[:, LANES:] = extra.astype(BF16)
        vt_scr[0:LANES, :] = v_ref[0].T.astype(BF16)
        r = lax.broadcasted_iota(jnp.int32, (_V_ROWS - LANES, t_all), 0)
        vt_scr[LANES:, :] = jnp.where(r == 0, 1.0, 0.0).astype(BF16)

    q = q_ref[0] * (FOX_DH ** -0.5 * LOG2E)
    lane = lax.broadcasted_iota(jnp.int32, q.shape, 1)
    members = []
    for c in range(2):
        x = jnp.where((lane < half) if c == 0 else (lane >= half), q, 0.0)
        if fox:
            pick = (lane >= 3 * c) & (lane < 3 * c + 3)
            x = jnp.concatenate([x, jnp.where(pick, -1.0, 0.0)], axis=1)
        members.append(x)
    qa = jnp.concatenate(members, axis=0).astype(BF16)
    if fox:
        fq = jnp.concatenate([crow_ref[0, pl.ds(2 * p + c, 1), :] for c in range(2)], axis=1) * LOG2E
    m_scr[...] = jnp.full(m_scr.shape, NEG, F32)
    acc_scr[...] = jnp.zeros_like(acc_scr)

    def scores(kb, masked):
        start = pl.multiple_of(kb * tq, tq)
        s = _dot_nt(ka_scr[pl.ds(start, tq), :], qa)
        if masked:
            krow = lax.broadcasted_iota(jnp.int32, (tq, tq), 0)
            qcol = lax.broadcasted_iota(jnp.int32, (tq, tq), 1)
            keep = krow <= qcol
            s = jnp.where(jnp.concatenate([keep, keep], axis=1), s, NEG)
        col_max = jnp.max(s, axis=0, keepdims=True)
        if fox:
            col_max = col_max + fq
        return start, s, col_max

    def accumulate(start, s, col_max):
        m_old = m_scr[...]
        m_new = jnp.maximum(m_old, col_max)
        shift = (m_new - fq) if fox else m_new
        pr = jnp.exp2(s - shift).astype(BF16)
        alpha = jnp.exp2(m_old - m_new)
        acc_scr[...] = alpha * acc_scr[...] + _dot(vt_scr[:, pl.ds(start, tq)], pr)
        m_scr[...] = m_new

    def group(kb0, n, last_masked):
        parts = [scores(kb0 + i, last_masked and i == n - 1) for i in range(n)]
        for part in parts:
            accumulate(*part)

    def body(i, carry):
        group(_ATTN_GROUP * i, _ATTN_GROUP, False)
        return carry

    lax.fori_loop(0, qi // _ATTN_GROUP, body, 0)
    rest = qi % _ATTN_GROUP
    for n_full in range(_ATTN_GROUP):
        @pl.when(rest == n_full)
        def _(n_full=n_full):
            group(qi - n_full, n_full + 1, True)

    acc = acc_scr[...]
    outs_t = [acc[0:LANES, c * tq:(c + 1) * tq] / acc[LANES:LANES + 1, c * tq:(c + 1) * tq] for c in range(2)]
    if fox:
        o_ref[0] = jnp.concatenate([outs_t[0][:half], outs_t[1][half:]], axis=0).T
    else:
        lp = lam_ref[...]
        lam = (jnp.exp(jnp.sum(lp[0:1] * lp[1:2], axis=1, keepdims=True))
               - jnp.exp(jnp.sum(lp[2:3] * lp[3:4], axis=1, keepdims=True)) + lam_init)
        o = (outs_t[0] - lam * outs_t[1]).T
        o_ref[0] = _rms(o, g_ref[...]) * (1.0 - lam_init)


def _attn_prompt(fox, q, kv, extra_a, extra_b, lam_init):
    b, t, w = q.shape
    groups = w // LANES
    tq = min(512, t)
    kernel = functools.partial(_attn_prompt_kernel, fox=fox, tq=tq, lam_init=lam_init)
    if fox:
        extra_specs = [pl.BlockSpec((1, t, FOX_HEADS), lambda i, p, j: (i, 0, 0)),
                       pl.BlockSpec((1, FOX_HEADS, tq), lambda i, p, j: (i, 0, j))]
    else:
        extra_specs = [pl.BlockSpec(extra_a.shape, lambda i, p, j: (0, 0)),
                       pl.BlockSpec(extra_b.shape, lambda i, p, j: (0, 0))]
    return pl.pallas_call(
        kernel,
        out_shape=jax.ShapeDtypeStruct((b, t, w), F32),
        grid=(b, groups, t // tq),
        in_specs=[
            pl.BlockSpec((1, tq, LANES), lambda i, p, j: (i, j, p)),
            pl.BlockSpec((1, t, LANES), lambda i, p, j: (i, 0, p)),
            pl.BlockSpec((1, t, LANES), lambda i, p, j: (i, 0, groups + p)),
        ] + extra_specs,
        out_specs=pl.BlockSpec((1, tq, LANES), lambda i, p, j: (i, j, p)),
        scratch_shapes=[pltpu.VMEM((t, 2 * LANES if fox else LANES), BF16), pltpu.VMEM((_V_ROWS, t), BF16),
                        pltpu.VMEM((1, 2 * tq), F32), pltpu.VMEM((_V_ROWS, 2 * tq), F32)],
        compiler_params=_params("parallel", "parallel", "arbitrary"),
        name="fox_prompt" if fox else "diff_prompt",
    )(q, kv, kv, extra_a, extra_b)


_PAGES_PER_STEP = 8


def _decode_kernel(pt_ref, qf_ref, qd_ref, nkf_ref, nkd_ref, nlf_ref, lam_ref, g_ref, sfx_ref, *rest,
                   pps, lam_init):
    del pt_ref
    page_refs = rest[:3 * pps]
    of_ref, od_ref = rest[3 * pps:3 * pps + 2]
    mf_scr, lf_scr, af_scr, md_scr, ld_scr, ad_scr, carry_scr = rest[3 * pps + 2:]
    b = pl.program_id(0)
    j = pl.program_id(1)
    scale = FOX_DH ** -0.5
    dj = 2 * DIFF_HEADS
    hd = 2 * DIFF_DH

    q_row = qf_ref[0]
    rf = lax.broadcasted_iota(jnp.int32, (FOX_HEADS, FOX_W), 0)
    lf_ = lax.broadcasted_iota(jnp.int32, (FOX_HEADS, FOX_W), 1)
    own = (lf_ // FOX_DH) == rf
    qblk = jnp.where(own, jnp.broadcast_to(q_row, (FOX_HEADS, FOX_W)), 0.0)
    qblk_b = qblk.astype(BF16)

    qd_row = qd_ref[0]
    r8 = lax.broadcasted_iota(jnp.int32, (dj, hd), 0)
    l8 = lax.broadcasted_iota(jnp.int32, (dj, hd), 1)
    head_of_row = r8 % DIFF_HEADS
    in_map = (l8 // DIFF_DH) == (r8 // DIFF_HEADS)

    def rows_from(vec, offset):
        out = jnp.zeros((dj, hd), F32)
        for h in range(DIFF_HEADS):
            piece = jnp.broadcast_to(vec[:, offset + h * hd:offset + (h + 1) * hd], (dj, hd))
            out = jnp.where(head_of_row == h, piece, out)
        return out

    q8 = jnp.where(in_map, rows_from(qd_row, 0), 0.0)
    q8_h = [jnp.where(head_of_row == h, q8, 0.0).astype(BF16) for h in range(DIFF_HEADS)]

    @pl.when(j == 0)
    def _():
        kvn = nkf_ref[0]
        kn = jnp.broadcast_to(kvn[:, 0:FOX_W], (FOX_HEADS, FOX_W))
        mf_scr[...] = jnp.broadcast_to(jnp.sum(qblk * kn, axis=1, keepdims=True) * scale, mf_scr.shape)
        lf_scr[...] = jnp.ones_like(lf_scr)
        af_scr[...] = jnp.broadcast_to(kvn[:, FOX_W:], (FOX_HEADS, FOX_W))
        kvd = nkd_ref[0]
        md_scr[...] = jnp.broadcast_to(
            jnp.sum(q8 * rows_from(kvd, 0), axis=1, keepdims=True) * scale, md_scr.shape)
        ld_scr[...] = jnp.ones_like(ld_scr)
        ad_scr[...] = rows_from(kvd, DIFF_W)
        nl = nlf_ref[...]
        sl = lax.broadcasted_iota(jnp.int32, nl.shape, 1)
        carry_scr[...] = jnp.broadcast_to(
            jnp.sum(jnp.where(sl == b, nl, 0.0), axis=1, keepdims=True), carry_scr.shape)

    carry = carry_scr[:, 0:1]
    s_f = []
    for i in range(pps):
        kt = page_refs[3 * i][0, 0].astype(BF16)
        lt = page_refs[3 * i + 1][0]
        hi, mid, lo = _split3(lt)
        r3 = _dot(jnp.concatenate([hi, mid, lo], axis=0), sfx_ref[...])
        bias = carry + r3[0:8] + r3[8:16] + r3[16:24]
        s_f.append(_dot(qblk_b, kt) * scale + bias)
        carry = carry + jnp.sum(lt, axis=1, keepdims=True)
    carry_scr[...] = jnp.broadcast_to(carry, carry_scr.shape)

    page = page_refs[2].shape[1] // dj
    s_d = []
    for i in range(pps):
        xd = page_refs[3 * i + 2]
        s = jnp.zeros((dj, page), F32)
        for h in range(DIFF_HEADS):
            k_h = xd[0, pl.ds(h, page, stride=dj), :].astype(BF16)
            s = s + _dot_nt(q8_h[h], k_h)
        s_d.append(s * scale)

    m_old = mf_scr[:, 0:1]
    m_new = jnp.maximum(m_old, jnp.max(functools.reduce(jnp.maximum, s_f), axis=1, keepdims=True))
    alpha = jnp.exp(m_old - m_new)
    l_add = jnp.zeros((FOX_HEADS, 1), F32)
    pv = jnp.zeros((FOX_HEADS, FOX_W), F32)
    for i in range(pps):
        pr = jnp.exp(s_f[i] - m_new)
        l_add = l_add + jnp.sum(pr, axis=1, keepdims=True)
        vt = page_refs[3 * i][0, 1].astype(BF16)
        pv = pv + _dot_nt(pr.astype(BF16), vt)
    lf_scr[...] = alpha * lf_scr[...] + l_add
    af_scr[...] = alpha * af_scr[...] + pv
    mf_scr[...] = jnp.broadcast_to(m_new, mf_scr.shape)

    md_old = md_scr[:, 0:1]
    md_new = jnp.maximum(md_old, jnp.max(functools.reduce(jnp.maximum, s_d), axis=1, keepdims=True))
    alphad = jnp.exp(md_old - md_new)
    ld_add = jnp.zeros((dj, 1), F32)
    pvd = jnp.zeros((dj, hd), F32)
    for i in range(pps):
        xd = page_refs[3 * i + 2]
        pr = jnp.exp(s_d[i] - md_new)
        ld_add = ld_add + jnp.sum(pr, axis=1, keepdims=True)
        for h in range(DIFF_HEADS):
            v_h = xd[0, pl.ds(DIFF_HEADS + h, page, stride=dj), :].astype(BF16)
            pvd = pvd + _dot(jnp.where(head_of_row == h, pr, 0.0).astype(BF16), v_h)
    ld_scr[...] = alphad * ld_scr[...] + ld_add
    ad_scr[...] = alphad * ad_scr[...] + pvd
    md_scr[...] = jnp.broadcast_to(md_new, md_scr.shape)

    @pl.when(j == pl.num_programs(1) - 1)
    def _():
        o_all = af_scr[...] / lf_scr[:, 0:1]
        of_ref[0] = jnp.sum(jnp.where(own, o_all, 0.0), axis=0, keepdims=True)
        od_all = ad_scr[...] / ld_scr[:, 0:1]
        lp = lam_ref[...]
        lam = (jnp.exp(jnp.sum(lp[0:1] * lp[1:2], axis=1, keepdims=True))
               - jnp.exp(jnp.sum(lp[2:3] * lp[3:4], axis=1, keepdims=True)) + lam_init)
        o = od_all[0:DIFF_HEADS] - lam * od_all[DIFF_HEADS:]
        od_ref[0] = _rms(o, g_ref[...]) * (1.0 - lam_init)


def _decode(page_table, qf, qd, new_kvf, new_kvd, new_lft, lam_p, subln_g, cache_ft, cache_lt, cache_d, lam_init):
    nb, n_pages = page_table.shape
    page = cache_lt.shape[2]
    pps = math.gcd(_PAGES_PER_STEP, n_pages)
    dj, hd = 2 * DIFF_HEADS, 2 * DIFF_DH
    kk = lax.broadcasted_iota(jnp.int32, (page, page), 0)
    kc = lax.broadcasted_iota(jnp.int32, (page, page), 1)
    sfx = jnp.where(kk > kc, 1.0, 0.0).astype(BF16)

    def page_idx(i):
        return lambda s, j, pt: pt[s, n_pages - 1 - (j * pps + i)]

    page_specs = []
    page_args = []
    for i in range(pps):
        pick = page_idx(i)
        page_specs += [
            pl.BlockSpec((1, 2, FOX_W, page), lambda s, j, pt, pick=pick: (pick(s, j, pt), 0, 0, 0)),
            pl.BlockSpec((1, FOX_HEADS, page), lambda s, j, pt, pick=pick: (pick(s, j, pt), 0, 0)),
            pl.BlockSpec((1, page * dj, hd), lambda s, j, pt, pick=pick: (pick(s, j, pt), 0, 0)),
        ]
        page_args += [cache_ft, cache_lt, cache_d]
    samp = lambda s, j, pt: (s, 0, 0)
    const = lambda s, j, pt: (0, 0)
    grid_spec = pltpu.PrefetchScalarGridSpec(
        num_scalar_prefetch=1,
        grid=(nb, n_pages // pps),
        in_specs=[
            pl.BlockSpec((1, 1, FOX_W), samp),
            pl.BlockSpec((1, 1, DIFF_W), samp),
            pl.BlockSpec((1, 1, 2 * FOX_W), samp),
            pl.BlockSpec((1, 1, 2 * DIFF_W), samp),
            pl.BlockSpec(new_lft.shape, const),
            pl.BlockSpec(lam_p.shape, const),
            pl.BlockSpec(subln_g.shape, const),
            pl.BlockSpec(sfx.shape, const),
        ] + page_specs,
        out_specs=[
            pl.BlockSpec((1, 1, FOX_W), samp),
            pl.BlockSpec((1, DIFF_HEADS, hd), samp),
        ],
        scratch_shapes=[
            pltpu.VMEM((FOX_HEADS, LANES), F32), pltpu.VMEM((FOX_HEADS, LANES), F32),
            pltpu.VMEM((FOX_HEADS, FOX_W), F32),
            pltpu.VMEM((dj, LANES), F32), pltpu.VMEM((dj, LANES), F32),
            pltpu.VMEM((dj, hd), F32),
            pltpu.VMEM((FOX_HEADS, LANES), F32),
        ],
    )
    return pl.pallas_call(
        functools.partial(_decode_kernel, pps=pps, lam_init=lam_init),
        out_shape=[jax.ShapeDtypeStruct((nb, 1, FOX_W), F32),
                   jax.ShapeDtypeStruct((nb, DIFF_HEADS, hd), F32)],
        grid_spec=grid_spec,
        compiler_params=_params("parallel", "arbitrary"),
        name="decode_attn",
    )(page_table, qf, qd, new_kvf, new_kvd, new_lft, lam_p, subln_g, sfx, *page_args)


def _rec_proj_kernel(x_ref, g_ref, w_ref, wgt_ref, bg_ref, q_ref, k_ref, v_ref, og_ref, u_ref, gt_ref):
    h = _rms(x_ref[0], g_ref[...]).astype(BF16)
    c0 = 0
    q_ref[0] = _dot(h, w_ref[:, c0:c0 + MLSTM_QK]) * (MLSTM_DK ** -0.5)
    c0 += MLSTM_QK
    k_ref[0] = _dot(h, w_ref[:, c0:c0 + MLSTM_QK])
    c0 += MLSTM_QK
    v_ref[0] = _dot(h, w_ref[:, c0:c0 + MLSTM_V])
    c0 += MLSTM_V
    og_ref[0] = _dot(h, w_ref[:, c0:c0 + MLSTM_V])
    c0 += MLSTM_V
    ua = _dot(h, w_ref[:, c0:c0 + CONV_CH])
    c0 += CONV_CH
    ub = _dot(h, w_ref[:, c0:c0 + CONV_CH])
    u_ref[0] = ua * jax.nn.sigmoid(ub)
    z = _dot_nt(wgt_ref[...], h) + bg_ref[...]
    rowi = lax.broadcasted_iota(jnp.int32, z.shape, 0)
    gt_ref[0] = jnp.where(rowi < MLSTM_HEADS, z, _log_sigmoid(z))


def _rec_proj(x, g, w_main, wgt, bg):
    b, t, d = x.shape
    tm = min(512, t)
    row = lambda i, j: (i, j, 0)
    const = lambda i, j: (0, 0)
    widths = [MLSTM_QK, MLSTM_QK, MLSTM_V, MLSTM_V, CONV_CH]
    return pl.pallas_call(
        _rec_proj_kernel,
        out_shape=[jax.ShapeDtypeStruct((b, t, w), F32) for w in widths]
        + [jax.ShapeDtypeStruct((b, 2 * MLSTM_HEADS, t), F32)],
        grid=(b, t // tm),
        in_specs=[
            pl.BlockSpec((1, tm, d), row),
            pl.BlockSpec((1, d), const),
            pl.BlockSpec(w_main.shape, const),
            pl.BlockSpec(wgt.shape, const),
            pl.BlockSpec(bg.shape, const),
        ],
        out_specs=[pl.BlockSpec((1, tm, w), row) for w in widths]
        + [pl.BlockSpec((1, 2 * MLSTM_HEADS, tm), lambda i, j: (i, 0, j))],
        compiler_params=_params("parallel", "parallel"),
        name="rec_proj",
    )(x, g.reshape(1, d), w_main, wgt, bg)


def _mlstm_prompt_kernel(q_ref, k_ref, v_ref, og_ref, gt_ref, gc_ref, ng_ref,
                         hm_ref, c_ref, n_ref, m_ref, ct_scr, n_scr, m_scr, *, chunk):
    t = pl.program_id(1)
    L = chunk
    half = LANES // 2

    @pl.when(t == 0)
    def _():
        ct_scr[...] = jnp.zeros_like(ct_scr)
        n_scr[...] = jnp.zeros_like(n_scr)
        m_scr[...] = jnp.full(m_scr.shape, NEG, F32)

    gt = gt_ref[0]
    gc = gc_ref[0]
    cum_rows = _dot3_left(gt, _tri(L, upper=True))
    cum_cols = _dot3_right(_tri(L, upper=False), gc)
    rr = lax.broadcasted_iota(jnp.int32, (L, L), 0)
    cc = lax.broadcasted_iota(jnp.int32, (L, L), 1)
    tri = cc <= rr
    lane = lax.broadcasted_iota(jnp.int32, (L, LANES), 1)

    heads = []
    for h in range(MLSTM_HEADS):
        pair, c = divmod(h, 2)
        qp = q_ref[0, :, pair * LANES:(pair + 1) * LANES]
        kp = k_ref[0, :, pair * LANES:(pair + 1) * LANES]
        mine = (lane < half) if c == 0 else (lane >= half)
        qm = jnp.where(mine, qp, 0.0)
        qm_b = qm.astype(BF16)
        qk = _dot_nt(qm_b, kp.astype(BF16))
        qc = _dot(qm_b, ct_scr[pair].astype(BF16))
        heads.append(dict(pair=pair, c=c, kp=kp, qm=qm, qk=qk, qc=qc))

    for h, hd_ in enumerate(heads):
        a_col = cum_cols[:, MLSTM_HEADS + h:MLSTM_HEADS + h + 1]
        ig_col = gc[:, h:h + 1]
        b_row = gt[h:h + 1, :] - cum_rows[MLSTM_HEADS + h:MLSTM_HEADS + h + 1, :]
        m_prev = m_scr[h:h + 1, 0:1]
        d = jnp.where(tri, a_col + b_row, NEG)
        m_inter = a_col + m_prev
        m_t = jnp.maximum(jnp.max(d, axis=1, keepdims=True), m_inter)
        s = hd_["qk"] * jnp.exp(d - m_t)
        w_inter = jnp.exp(m_inter - m_t)
        m_new = m_t[L - 1:L, :]
        cum_last = a_col[L - 1:L, :]
        decay = jnp.exp(cum_last + m_prev - m_new)
        w_key = jnp.exp(cum_last - a_col + ig_col - m_new)
        vh = v_ref[0, :, h * MLSTM_DV:(h + 1) * MLSTM_DV]
        hd_.update(s=s, w_inter=w_inter, m_t=m_t, m_new=m_new, decay=decay, w_key=w_key, vh=vh)

    for hd_ in heads:
        hd_["sv"] = _dot(hd_["s"].astype(BF16), hd_["vh"].astype(BF16))
        kt_b = hd_["kp"].T.astype(BF16)
        hd_["upd"] = _dot(kt_b, (hd_["w_key"] * hd_["vh"]).astype(BF16))

    lane1 = lax.broadcasted_iota(jnp.int32, (1, LANES), 1)
    for pair in range(MLSTM_HEADS // 2):
        ct = ct_scr[pair]
        n_row = n_scr[pair]
        ct_new = []
        n_new = []
        for c in range(2):
            h = 2 * pair + c
            hd_ = heads[h]
            num = hd_["sv"] + hd_["w_inter"] * hd_["qc"]
            den = (jnp.sum(hd_["s"], axis=1, keepdims=True)
                   + hd_["w_inter"] * jnp.sum(hd_["qm"] * n_row, axis=1, keepdims=True))
            hout = num / jnp.maximum(jnp.abs(den), jnp.exp(-hd_["m_t"]))
            gh = ng_ref[:, h * MLSTM_DV:(h + 1) * MLSTM_DV]
            oh = og_ref[0, :, h * MLSTM_DV:(h + 1) * MLSTM_DV]
            hm_ref[0, :, h * MLSTM_DV:(h + 1) * MLSTM_DV] = _rms(hout, gh) * jax.nn.sigmoid(oh)
            lo, hi_ = c * half, (c + 1) * half
            ct_new.append(hd_["decay"] * ct[lo:hi_] + hd_["upd"][lo:hi_])
            n_new.append(hd_["decay"] * n_row + jnp.sum(hd_["w_key"] * hd_["kp"], axis=0, keepdims=True))
            m_scr[h:h + 1, :] = jnp.broadcast_to(hd_["m_new"], (1, LANES))
        ct_scr[pair] = jnp.concatenate(ct_new, axis=0)
        n_scr[pair] = jnp.where(lane1 < half, n_new[0], n_new[1])

    @pl.when(t == pl.num_programs(1) - 1)
    def _():
        for pair in range(MLSTM_HEADS // 2):
            c_pair = ct_scr[pair].T
            n_row = n_scr[pair]
            for c in range(2):
                h = 2 * pair + c
                c_ref[0, h] = c_pair[:, c * half:(c + 1) * half]
                n_ref[0, h:h + 1, :] = n_row[:, c * half:(c + 1) * half]
                m_ref[0, :, h:h + 1] = m_scr[h:h + 1, 0:1]


def _mlstm_prompt(q, k, v, og, gt, gc, norm_g):
    b, t, _ = q.shape
    chunk = min(256, t)
    row = lambda i, j: (i, j, 0)
    const = lambda i, j: (0, 0)
    return pl.pallas_call(
        functools.partial(_mlstm_prompt_kernel, chunk=chunk),
        out_shape=[
            jax.ShapeDtypeStruct((b, t, MLSTM_V), F32),
            jax.ShapeDtypeStruct((b, MLSTM_HEADS, MLSTM_DV, MLSTM_DK), F32),
            jax.ShapeDtypeStruct((b, MLSTM_HEADS, MLSTM_DK), F32),
            jax.ShapeDtypeStruct((b, 1, MLSTM_HEADS), F32),
        ],
        grid=(b, t // chunk),
        in_specs=[
            pl.BlockSpec((1, chunk, MLSTM_QK), row),
            pl.BlockSpec((1, chunk, MLSTM_QK), row),
            pl.BlockSpec((1, chunk, MLSTM_V), row),
            pl.BlockSpec((1, chunk, MLSTM_V), row),
            pl.BlockSpec((1, 2 * MLSTM_HEADS, chunk), lambda i, j: (i, 0, j)),
            pl.BlockSpec((1, chunk, 2 * MLSTM_HEADS), row),
            pl.BlockSpec((1, MLSTM_V), const),
        ],
        out_specs=[
            pl.BlockSpec((1, chunk, MLSTM_V), row),
            pl.BlockSpec((1, MLSTM_HEADS, MLSTM_DV, MLSTM_DK), lambda i, j: (i, 0, 0, 0)),
            pl.BlockSpec((1, MLSTM_HEADS, MLSTM_DK), lambda i, j: (i, 0, 0)),
            pl.BlockSpec((1, 1, MLSTM_HEADS), lambda i, j: (i, 0, 0)),
        ],
        scratch_shapes=[
            pltpu.VMEM((MLSTM_HEADS // 2, LANES, MLSTM_DV), F32),
            pltpu.VMEM((MLSTM_HEADS // 2, 1, LANES), F32),
            pltpu.VMEM((SUBLANES, LANES), F32),
        ],
        compiler_params=_params("parallel", "arbitrary"),
        name="mlstm_prompt",
    )(q, k, v, og, gt, gc, norm_g.reshape(1, MLSTM_V))


def _mlstm_sample_kernel(q_ref, k_ref, v_ref, og_ref, gc_ref, m_ref, kall_ref, vall_ref, ng_ref, c_ref, n_ref,
                         hm_ref, co_ref, no_ref, mo_ref, vt_scr, *, bs):
    i = pl.program_id(0)
    nb = kall_ref.shape[0]

    @pl.when(i == 0)
    def _():
        for h in range(MLSTM_HEADS):
            vt_scr[h] = vall_ref[:, h * MLSTM_DV:(h + 1) * MLSTM_DV].T

    rows = lax.broadcasted_iota(jnp.int32, (nb, MLSTM_DK), 0)

    def body(bl, _):
        b = i * bs + bl
        q_row = q_ref[bl]
        k_row = k_ref[bl]
        v_row = v_ref[bl]
        og_row = og_ref[bl]
        g_row = gc_ref[bl]
        m_row = m_ref[bl]
        for h in range(MLSTM_HEADS):
            qh = q_row[:, h * MLSTM_DK:(h + 1) * MLSTM_DK]
            kh = k_row[:, h * MLSTM_DK:(h + 1) * MLSTM_DK]
            vh = v_row[:, h * MLSTM_DV:(h + 1) * MLSTM_DV]
            it = g_row[:, h:h + 1]
            lf = g_row[:, MLSTM_HEADS + h:MLSTM_HEADS + h + 1]
            m_prev = m_row[:, h:h + 1]
            m_inter = lf + m_prev
            m_t = jnp.maximum(it, m_inter)
            e_i = jnp.exp(it - m_t)
            w_inter = jnp.exp(m_inter - m_t)
            s = jnp.sum(qh * kh, axis=1, keepdims=True) * e_i
            cm = c_ref[bl, h]
            nh = n_ref[bl, h:h + 1, :]
            num = s * vh + w_inter * _dot_nt(qh.astype(BF16), cm.astype(BF16))
            den = s + w_inter * jnp.sum(nh * qh, axis=1, keepdims=True)
            hrow = num / jnp.maximum(jnp.abs(den), jnp.exp(-m_t))
            gh = ng_ref[:, h * MLSTM_DV:(h + 1) * MLSTM_DV]
            oh = og_row[:, h * MLSTM_DV:(h + 1) * MLSTM_DV]
            hm_ref[bl, :, h * MLSTM_DV:(h + 1) * MLSTM_DV] = _rms(hrow, gh) * jax.nn.sigmoid(oh)
            k_all = kall_ref[:, h * MLSTM_DK:(h + 1) * MLSTM_DK]
            k_sel = jnp.where(rows == b, k_all, 0.0).astype(BF16)
            outer = _dot(vt_scr[h].astype(BF16), k_sel)
            co_ref[bl, h] = w_inter * cm + e_i * outer
            no_ref[bl, h:h + 1, :] = w_inter * nh + e_i * kh
            mo_ref[bl, :, h:h + 1] = m_t
        return 0

    lax.fori_loop(0, bs, body, 0)


def _mlstm_sample(q, k, v, og, gc, norm_g, c0, n0, m0):
    nb = q.shape[0]
    bs = min(16, nb)
    full = lambda i: (0, 0)
    rows = lambda i: (i, 0, 0)
    per_sample = [a.reshape(nb, 1, a.shape[1]) for a in (q, k, v, og, gc, m0)]
    hm, c1, n1, m1 = pl.pallas_call(
        functools.partial(_mlstm_sample_kernel, bs=bs),
        out_shape=[
            jax.ShapeDtypeStruct((nb, 1, MLSTM_V), F32),
            jax.ShapeDtypeStruct(c0.shape, F32),
            jax.ShapeDtypeStruct(n0.shape, F32),
            jax.ShapeDtypeStruct((nb, 1, MLSTM_HEADS), F32),
        ],
        grid=(nb // bs,),
        in_specs=[pl.BlockSpec((bs, 1, a.shape[2]), rows) for a in per_sample] + [
            pl.BlockSpec(k.shape, full),
            pl.BlockSpec(v.shape, full),
            pl.BlockSpec((1, MLSTM_V), full),
            pl.BlockSpec((bs, MLSTM_HEADS, MLSTM_DV, MLSTM_DK), lambda i: (i, 0, 0, 0)),
            pl.BlockSpec((bs, MLSTM_HEADS, MLSTM_DK), rows),
        ],
        out_specs=[
            pl.BlockSpec((bs, 1, MLSTM_V), rows),
            pl.BlockSpec((bs, MLSTM_HEADS, MLSTM_DV, MLSTM_DK), lambda i: (i, 0, 0, 0)),
            pl.BlockSpec((bs, MLSTM_HEADS, MLSTM_DK), rows),
            pl.BlockSpec((bs, 1, MLSTM_HEADS), rows),
        ],
        scratch_shapes=[pltpu.VMEM((MLSTM_HEADS, MLSTM_DV, nb), F32)],
        compiler_params=_params("arbitrary"),
        name="mlstm_sample",
    )(*per_sample, k, v, norm_g.reshape(1, MLSTM_V), c0, n0)
    return hm.reshape(nb, MLSTM_V), c1, n1, m1.reshape(nb, MLSTM_HEADS)


_CONV_HALO = 32
_CONV_ROWS = 32


def _ln_silu(y, g, b):
    yc = y - jnp.mean(y, axis=-1, keepdims=True)
    var = jnp.mean(yc * yc, axis=-1, keepdims=True)
    z = yc * lax.rsqrt(var + NORM_EPS) * g + b
    return z * jax.nn.sigmoid(z)


def _conv_prompt_kernel(u_ref, prev_ref, w_ref, b_ref, g_ref, be_ref, c_ref, st_ref, full_scr, shift_scr, *, tt):
    t = pl.program_id(1)
    pad = _CONV_HALO - (CONV_W - 1)

    @pl.when(t == 0)
    def _():
        full_scr[0:_CONV_HALO, :] = jnp.zeros((_CONV_HALO, CONV_CH), F32)
        full_scr[pad:_CONV_HALO, :] = prev_ref[0]

    full_scr[_CONV_HALO:, :] = u_ref[0]
    span = tt + _CONV_HALO - SUBLANES
    for r in range(1, SUBLANES):
        shift_scr[r, 0:span, :] = full_scr[r:r + span, :]
    for r0 in range(0, tt, _CONV_ROWS):
        acc = jnp.broadcast_to(b_ref[...], (_CONV_ROWS, CONV_CH))
        for k in range(CONV_W):
            off = r0 + pad + k
            r = off % SUBLANES
            if r == 0:
                rows = full_scr[off:off + _CONV_ROWS, :]
            else:
                rows = shift_scr[r, off - r:off - r + _CONV_ROWS, :]
            acc = acc + w_ref[k:k + 1, :] * rows
        c_ref[0, r0:r0 + _CONV_ROWS, :] = _ln_silu(acc, g_ref[...], be_ref[...])

    @pl.when(t == pl.num_programs(1) - 1)
    def _():
        st_ref[0] = full_scr[_CONV_HALO + tt - (CONV_W - 1):, :]

    full_scr[0:_CONV_HALO, :] = full_scr[tt:tt + _CONV_HALO, :]


def _conv_prompt(u, prev, w, bias, ln_g, ln_b):
    b, t, ch = u.shape
    tt = min(256, t)
    const = lambda i, j: (0, 0)
    return pl.pallas_call(
        functools.partial(_conv_prompt_kernel, tt=tt),
        out_shape=[jax.ShapeDtypeStruct((b, t, ch), F32),
                   jax.ShapeDtypeStruct((b, CONV_W - 1, ch), F32)],
        grid=(b, t // tt),
        in_specs=[
            pl.BlockSpec((1, tt, ch), lambda i, j: (i, j, 0)),
            pl.BlockSpec((1, CONV_W - 1, ch), lambda i, j: (i, 0, 0)),
            pl.BlockSpec((CONV_W, ch), const),
            pl.BlockSpec((1, ch), const),
            pl.BlockSpec((1, ch), const),
            pl.BlockSpec((1, ch), const),
        ],
        out_specs=[pl.BlockSpec((1, tt, ch), lambda i, j: (i, j, 0)),
                   pl.BlockSpec((1, CONV_W - 1, ch), lambda i, j: (i, 0, 0))],
        scratch_shapes=[pltpu.VMEM((_CONV_HALO + tt, ch), F32),
                        pltpu.VMEM((SUBLANES, _CONV_HALO + tt, ch), F32)],
        compiler_params=_params("parallel", "arbitrary"),
        name="conv_prompt",
    )(u, prev, w, bias.reshape(1, ch), ln_g.reshape(1, ch), ln_b.reshape(1, ch))


def _conv_sample_kernel(u_ref, prev_ref, w_ref, b_ref, g_ref, be_ref, c_ref, st_ref):
    hist = CONV_W - 1
    u = u_ref[...]
    y = w_ref[hist:hist + 1, :] * u + b_ref[...]
    for k in range(hist):
        row = prev_ref[k]
        y = y + w_ref[k:k + 1, :] * row
        if k > 0:
            st_ref[k - 1] = row
    st_ref[hist - 1] = u
    c_ref[...] = _ln_silu(y, g_ref[...], be_ref[...])


def _conv_sample(u, prev_t, w, bias, ln_g, ln_b):
    nb, ch = u.shape
    bs = min(32, nb)
    hist = CONV_W - 1
    const = lambda i: (0, 0)
    return pl.pallas_call(
        _conv_sample_kernel,
        out_shape=[jax.ShapeDtypeStruct((nb, ch), F32),
                   jax.ShapeDtypeStruct((hist, nb, ch), F32)],
        grid=(nb // bs,),
        in_specs=[
            pl.BlockSpec((bs, ch), lambda i: (i, 0)),
            pl.BlockSpec((hist, bs, ch), lambda i: (0, i, 0)),
            pl.BlockSpec((CONV_W, ch), const),
            pl.BlockSpec((1, ch), const),
            pl.BlockSpec((1, ch), const),
            pl.BlockSpec((1, ch), const),
        ],
        out_specs=[pl.BlockSpec((bs, ch), lambda i: (i, 0)),
                   pl.BlockSpec((hist, bs, ch), lambda i: (0, i, 0))],
        compiler_params=_params("parallel"),
        name="conv_sample",
    )(u, prev_t, w, bias.reshape(1, ch), ln_g.reshape(1, ch), ln_b.reshape(1, ch))


def kernel(x_prompt, x_sample, cache_fox_kv, cache_fox_logf, cache_diff_kv, state_mlstm_C, state_mlstm_n,
           state_mlstm_m, state_conv, page_table, norm_g, final_g, ffn_w_in, ffn_w_out, attn_w_in, attn_b_f,
           diff_lam, diff_subln_g, attn_w_out, rec_w_in, rec_b_i, rec_b_f, mlstm_norm_g, conv_w, conv_b,
           conv_ln_g, conv_ln_b, rec_w_out):
    bp, t, d = x_prompt.shape
    nb = x_sample.shape[0]
    depth = norm_g.shape[0]
    n_pool, page = cache_fox_kv.shape[1], cache_fox_kv.shape[2]
    xp = x_prompt.reshape(bp * t, d)
    xs = x_sample.reshape(nb, d)
    w_in_b = ffn_w_in.astype(BF16)
    w_out_b = ffn_w_out.astype(BF16)

    outs = {k: [] for k in ("fkv_p", "fkv_s", "flf_p", "flf_s", "dkv_p", "dkv_s",
                            "c_p", "c_s", "n_p", "n_s", "m_p", "m_s", "cv_p", "cv_s")}
    for l in range(depth):
        j = l // 2
        xp = _ffn(xp, norm_g[l, 0], w_in_b, w_out_b, l, 0)
        xs = _ffn(xs, norm_g[l, 0], w_in_b, w_out_b, l, 0)
        if l % 2 == 0:
            lam_init = 0.8 - 0.6 * math.exp(-0.3 * l)
            w = attn_w_in[j]
            c1, c2 = 3 * FOX_W, 3 * FOX_W + FOX_HEADS
            w_main = jnp.concatenate([w[:, :c1], w[:, c2:]], axis=1).astype(BF16)
            wft = w[:, c1:c2].T.astype(BF16)
            bf = attn_b_f[j].reshape(FOX_HEADS, 1)
            w_o = attn_w_out[j].astype(BF16)
            qf, kvf, qd, kvd, lft, cumt, kvd_rows = _attn_proj(xp.reshape(bp, t, d), norm_g[l, 1], w_main, wft, bf)
            o_f = _attn_prompt(True, qf, kvf, jnp.swapaxes(cumt, 1, 2), cumt, lam_init)
            o_d = _attn_prompt(False, qd, kvd, diff_lam[j], diff_subln_g[j].reshape(1, -1), lam_init)
            xp = _merge(xp, o_f.reshape(bp * t, FOX_W), o_d.reshape(bp * t, DIFF_W), w_o)
            outs["fkv_p"].append(kvf.reshape(bp, t, 2, FOX_HEADS, FOX_DH))
            outs["flf_p"].append(jnp.swapaxes(lft, 1, 2))
            outs["dkv_p"].append(kvd_rows.reshape(bp, t, 2, DIFF_HEADS, 2 * DIFF_DH))
            sqf, skvf, sqd, skvd, slft, _, skvd_rows = _attn_proj(xs.reshape(1, nb, d), norm_g[l, 1], w_main, wft, bf)
            so_f, so_d = _decode(
                page_table,
                sqf.reshape(nb, 1, FOX_W), sqd.reshape(nb, 1, DIFF_W),
                skvf.reshape(nb, 1, 2 * FOX_W), skvd.reshape(nb, 1, 2 * DIFF_W),
                slft[0], diff_lam[j], diff_subln_g[j].reshape(1, -1),
                jnp.transpose(cache_fox_kv[j], (0, 2, 3, 4, 1)).reshape(n_pool, 2, FOX_W, page),
                jnp.swapaxes(cache_fox_logf[j], 1, 2),
                cache_diff_kv[j].reshape(n_pool, page * 2 * DIFF_HEADS, 2 * DIFF_DH),
                lam_init)
            xs = _merge(xs, so_f.reshape(nb, FOX_W), so_d.reshape(nb, DIFF_W), w_o)
            outs["fkv_s"].append(skvf.reshape(nb, 1, 2, FOX_HEADS, FOX_DH))
            outs["flf_s"].append(jnp.swapaxes(slft, 1, 2).reshape(nb, 1, FOX_HEADS))
            outs["dkv_s"].append(skvd_rows.reshape(nb, 1, 2, DIFF_HEADS, 2 * DIFF_DH))
        else:
            w = rec_w_in[j]
            c1 = 2 * MLSTM_QK + MLSTM_V
            c2 = c1 + 2 * MLSTM_HEADS
            w_main = jnp.concatenate([w[:, :c1], w[:, c2:]], axis=1).astype(BF16)
            wgt = w[:, c1:c2].T.astype(BF16)
            bg = jnp.concatenate([rec_b_i[j], rec_b_f[j]]).reshape(2 * MLSTM_HEADS, 1)
            w_o = rec_w_out[j].astype(BF16)
            q, k, v, og, u, gt = _rec_proj(xp.reshape(bp, t, d), norm_g[l, 1], w_main, wgt, bg)
            hm, c_p, n_p, m_p = _mlstm_prompt(q, k, v, og, gt, jnp.swapaxes(gt, 1, 2), mlstm_norm_g[j])
            cv, st_p = _conv_prompt(u, jnp.zeros((bp, CONV_W - 1, CONV_CH), F32), conv_w[j], conv_b[j],
                                    conv_ln_g[j], conv_ln_b[j])
            xp = _merge(xp, hm.reshape(bp * t, MLSTM_V), cv.reshape(bp * t, CONV_CH), w_o)
            outs["c_p"].append(c_p)
            outs["n_p"].append(n_p)
            outs["m_p"].append(m_p.reshape(bp, MLSTM_HEADS))
            outs["cv_p"].append(st_p)
            sq, sk, sv, sog, su, sgt = _rec_proj(xs.reshape(1, nb, d), norm_g[l, 1], w_main, wgt, bg)
            shm, c_s, n_s, m_s = _mlstm_sample(sq[0], sk[0], sv[0], sog[0], sgt[0].T, mlstm_norm_g[j],
                                               state_mlstm_C[j], state_mlstm_n[j], state_mlstm_m[j])
            scv, st_s = _conv_sample(su[0], jnp.swapaxes(state_conv[j], 0, 1), conv_w[j], conv_b[j],
                                     conv_ln_g[j], conv_ln_b[j])
            xs = _merge(xs, shm, scv, w_o)
            outs["c_s"].append(c_s)
            outs["n_s"].append(n_s)
            outs["m_s"].append(m_s)
            outs["cv_s"].append(jnp.swapaxes(st_s, 0, 1))
        last = l == depth - 1
        xp = _ffn(xp, norm_g[l, 2], w_in_b, w_out_b, l, 1, final_g if last else None)
        xs = _ffn(xs, norm_g[l, 2], w_in_b, w_out_b, l, 1, final_g if last else None)

    st = jnp.stack
    return (xp.reshape(bp, t, d), xs.reshape(nb, 1, d),
            st(outs["fkv_p"]), st(outs["fkv_s"]), st(outs["flf_p"]), st(outs["flf_s"]),
            st(outs["dkv_p"]), st(outs["dkv_s"]),
            st(outs["c_p"]), st(outs["c_s"]), st(outs["n_p"]), st(outs["n_s"]),
            st(outs["m_p"]), st(outs["m_s"]), st(outs["cv_p"]), st(outs["cv_s"]))
```

```python
import functools
import math

import jax
import jax.numpy as jnp
from jax import lax
from jax.experimental import pallas as pl
from jax.experimental.pallas import tpu as pltpu

F32 = jnp.float32
BF16 = jnp.bfloat16
NORM_EPS = 1e-6
NEG = -1e30
LOG2E = 1.4426950408889634

FOX_HEADS = 8
FOX_DH = 64
DIFF_HEADS = 4
DIFF_DH = 64
MLSTM_HEADS = 4
MLSTM_DK = 64
MLSTM_DV = 128
CONV_CH = 512
CONV_W = 31
FOX_W = FOX_HEADS * FOX_DH
DIFF_W = DIFF_HEADS * 2 * DIFF_DH
MLSTM_QK = MLSTM_HEADS * MLSTM_DK
MLSTM_V = MLSTM_HEADS * MLSTM_DV

LANES = 128
SUBLANES = 8
VMEM_LIMIT_BYTES = 56 * 1024 * 1024

_NT = (((1,), (1,)), ((), ()))


def _params(*sem):
    return pltpu.CompilerParams(dimension_semantics=sem, vmem_limit_bytes=VMEM_LIMIT_BYTES)


def _rms(x, g):
    return x * lax.rsqrt(jnp.mean(x * x, axis=-1, keepdims=True) + NORM_EPS) * g


def _log_sigmoid(z):
    return jnp.minimum(z, 0.0) - jnp.log(1.0 + jnp.exp(-jnp.abs(z)))


def _split3(x):
    hi = x.astype(BF16)
    r = x - hi.astype(F32)
    mid = r.astype(BF16)
    lo = (r - mid.astype(F32)).astype(BF16)
    return hi, mid, lo


def _dot(a, b):
    return jnp.dot(a, b, preferred_element_type=F32)


def _dot_nt(a, b):
    return lax.dot_general(a, b, _NT, preferred_element_type=F32)


def _dot3_left(x, t):
    hi, mid, lo = _split3(x)
    return _dot(hi, t) + _dot(mid, t) + _dot(lo, t)


def _dot3_right(t, x):
    hi, mid, lo = _split3(x)
    return _dot(t, hi) + _dot(t, mid) + _dot(t, lo)


def _tri(n, upper):
    r = lax.broadcasted_iota(jnp.int32, (n, n), 0)
    c = lax.broadcasted_iota(jnp.int32, (n, n), 1)
    keep = (r <= c) if upper else (r >= c)
    return jnp.where(keep, 1.0, 0.0).astype(BF16)


def _ffn_kernel(x_ref, g_ref, wa_ref, wb_ref, wo_ref, fg_ref, o_ref, h_scr, gated_scr, *, final_norm):
    j = pl.program_id(1)
    tf = wa_ref.shape[1]

    @pl.when(j == 0)
    def _():
        h_scr[...] = _rms(x_ref[...], g_ref[...]).astype(BF16)

    h = h_scr[...]
    a = _dot(h, wa_ref[...].astype(BF16))
    b = _dot(h, wb_ref[...].astype(BF16))
    gated_scr[:, pl.ds(pl.multiple_of(j * tf, tf), tf)] = (a * jax.nn.sigmoid(a) * b).astype(BF16)

    @pl.when(j == pl.num_programs(1) - 1)
    def _():
        y = x_ref[...] + 0.5 * _dot(gated_scr[...], wo_ref[...])
        if final_norm:
            y = _rms(y, fg_ref[...])
        o_ref[...] = y


def _ffn(x, g, w_in, w_out, layer, which, final_g=None):
    m, d = x.shape
    f = w_out.shape[2]
    tm = min(1024, m)
    tf = 256 if m > 256 else f // 2
    nf = f // tf
    assert m % tm == 0 and f % tf == 0 and tf % LANES == 0, (m, f)
    fg = jnp.ones((1, d), F32) if final_g is None else final_g.reshape(1, d)
    return pl.pallas_call(
        functools.partial(_ffn_kernel, final_norm=final_g is not None),
        out_shape=jax.ShapeDtypeStruct((m, d), F32),
        grid=(m // tm, nf),
        in_specs=[
            pl.BlockSpec((tm, d), lambda i, j: (i, 0)),
            pl.BlockSpec((1, d), lambda i, j: (0, 0)),
            pl.BlockSpec((None, None, d, tf), lambda i, j: (layer, which, 0, j)),
            pl.BlockSpec((None, None, d, tf), lambda i, j: (layer, which, 0, j + nf)),
            pl.BlockSpec((None, None, f, d), lambda i, j: (layer, which, 0, 0)),
            pl.BlockSpec((1, d), lambda i, j: (0, 0)),
        ],
        out_specs=pl.BlockSpec((tm, d), lambda i, j: (i, 0)),
        scratch_shapes=[pltpu.VMEM((tm, d), BF16), pltpu.VMEM((tm, f), BF16)],
        compiler_params=_params("parallel", "arbitrary"),
        name="ffn",
    )(x, g.reshape(1, d), w_in, w_in, w_out, fg)


def _merge_kernel(x_ref, oa_ref, ob_ref, w_ref, o_ref):
    wa = oa_ref.shape[-1]
    y = _dot(oa_ref[...].astype(BF16), w_ref[:wa, :])
    y += _dot(ob_ref[...].astype(BF16), w_ref[wa:, :])
    o_ref[...] = x_ref[...] + y


def _merge(x, oa, ob, w_out):
    m, d = x.shape
    tm = min(1024, m)
    wa, wb = oa.shape[1], ob.shape[1]
    return pl.pallas_call(
        _merge_kernel,
        out_shape=jax.ShapeDtypeStruct((m, d), F32),
        grid=(m // tm,),
        in_specs=[
            pl.BlockSpec((tm, d), lambda i: (i, 0)),
            pl.BlockSpec((tm, wa), lambda i: (i, 0)),
            pl.BlockSpec((tm, wb), lambda i: (i, 0)),
            pl.BlockSpec((wa + wb, d), lambda i: (0, 0)),
        ],
        out_specs=pl.BlockSpec((tm, d), lambda i: (i, 0)),
        compiler_params=_params("parallel"),
        name="merge",
    )(x, oa, ob, w_out)


def _attn_proj_kernel(x_ref, g_ref, w_ref, wft_ref, bf_ref, qf_ref, kvf_ref, qd_ref, kvd_ref,
                      lf_ref, cum_ref, kvdr_ref, carry_scr):
    t = pl.program_id(1)
    h = _rms(x_ref[0], g_ref[...]).astype(BF16)
    qf_ref[0] = _dot(h, w_ref[:, 0:FOX_W])
    kvf_ref[0] = _dot(h, w_ref[:, FOX_W:3 * FOX_W])
    qd_ref[0] = _dot(h, w_ref[:, 3 * FOX_W:3 * FOX_W + DIFF_W])
    kvd = _dot(h, w_ref[:, 3 * FOX_W + DIFF_W:])
    kvd_ref[0] = kvd
    groups = kvd.shape[1] // LANES
    for gi in range(groups):
        kvdr_ref[0, pl.ds(gi, kvd.shape[0], stride=groups), :] = kvd[:, gi * LANES:(gi + 1) * LANES]
    logf = _log_sigmoid(_dot_nt(wft_ref[...], h) + bf_ref[...])
    lf_ref[0] = logf

    @pl.when(t == 0)
    def _():
        carry_scr[...] = jnp.zeros_like(carry_scr)

    tm = logf.shape[1]
    cum = _dot3_left(logf, _tri(tm, upper=True)) + carry_scr[:, 0:1]
    cum_ref[0] = cum
    carry_scr[...] = jnp.broadcast_to(cum[:, tm - 1:tm], carry_scr.shape)


def _attn_proj(x, g, w_main, wft, bf):
    b, t, d = x.shape
    tm = min(512, t)
    n_main = w_main.shape[1]
    row = lambda i, j: (i, j, 0)
    col = lambda i, j: (i, 0, j)
    const = lambda i, j: (0, 0)
    return pl.pallas_call(
        _attn_proj_kernel,
        out_shape=[
            jax.ShapeDtypeStruct((b, t, FOX_W), F32),
            jax.ShapeDtypeStruct((b, t, 2 * FOX_W), F32),
            jax.ShapeDtypeStruct((b, t, DIFF_W), F32),
            jax.ShapeDtypeStruct((b, t, 2 * DIFF_W), F32),
            jax.ShapeDtypeStruct((b, FOX_HEADS, t), F32),
            jax.ShapeDtypeStruct((b, FOX_HEADS, t), F32),
            jax.ShapeDtypeStruct((b, t * (2 * DIFF_W // LANES), LANES), F32),
        ],
        grid=(b, t // tm),
        in_specs=[
            pl.BlockSpec((1, tm, d), row),
            pl.BlockSpec((1, d), const),
            pl.BlockSpec((d, n_main), const),
            pl.BlockSpec((FOX_HEADS, d), const),
            pl.BlockSpec((FOX_HEADS, 1), const),
        ],
        out_specs=[
            pl.BlockSpec((1, tm, FOX_W), row),
            pl.BlockSpec((1, tm, 2 * FOX_W), row),
            pl.BlockSpec((1, tm, DIFF_W), row),
            pl.BlockSpec((1, tm, 2 * DIFF_W), row),
            pl.BlockSpec((1, FOX_HEADS, tm), col),
            pl.BlockSpec((1, FOX_HEADS, tm), col),
            pl.BlockSpec((1, tm * (2 * DIFF_W // LANES), LANES), row),
        ],
        scratch_shapes=[pltpu.VMEM((FOX_HEADS, LANES), F32)],
        compiler_params=_params("parallel", "arbitrary"),
        name="attn_proj",
    )(x, g.reshape(1, d), w_main, wft, bf)


_V_ROWS = 144
_ATTN_GROUP = 4


def _attn_prompt_kernel(*refs, fox, tq, lam_init):
    if fox:
        q_ref, k_ref, v_ref, ccol_ref, crow_ref, o_ref, ka_scr, vt_scr, m_scr, acc_scr = refs
    else:
        q_ref, k_ref, v_ref, lam_ref, g_ref, o_ref, ka_scr, vt_scr, m_scr, acc_scr = refs
    p = pl.program_id(1)
    qi = pl.program_id(2)
    half = LANES // 2
    t_all = k_ref.shape[1]

    @pl.when(qi == 0)
    def _():
        ka_scr[:, 0:LANES] = k_ref[0].astype(BF16)
        if fox:
            cc = ccol_ref[0]
            hl = lax.broadcasted_iota(jnp.int32, cc.shape, 1)
            lane = lax.broadcasted_iota(jnp.int32, (t_all, LANES), 1)
            extra = jnp.zeros((t_all, LANES), F32)
            for c in range(2):
                fk = jnp.sum(jnp.where(hl == 2 * p + c, cc, 0.0), axis=1, keepdims=True) * LOG2E
                for i, piece in enumerate(_split3(fk)):
                    extra = jnp.where(lane == 3 * c + i, piece.astype(F32), extra)
            ka_scr[:, LANES:] = extra.astype(BF16)
        vt_scr[0:LANES, :] = v_ref[0].T.astype(BF16)
        r = lax.broadcasted_iota(jnp.int32, (_V_ROWS - LANES, t_all), 0)
        vt_scr[LANES:, :] = jnp.where(r == 0, 1.0, 0.0).astype(BF16)

    q = q_ref[0] * (FOX_DH ** -0.5 * LOG2E)
    lane = lax.broadcasted_iota(jnp.int32, q.shape, 1)
    members = []
    for c in range(2):
        x = jnp.where((lane < half) if c == 0 else (lane >= half), q, 0.0)
        if fox:
            pick = (lane >= 3 * c) & (lane < 3 * c + 3)
            x = jnp.concatenate([x, jnp.where(pick, -1.0, 0.0)], axis=1)
        members.append(x)
    qa = jnp.concatenate(members, axis=0).astype(BF16)
    if fox:
        fq = jnp.concatenate([crow_ref[0, pl.ds(2 * p + c, 1), :] for c in range(2)], axis=1) * LOG2E
    m_scr[...] = jnp.full(m_scr.shape, NEG, F32)
    acc_scr[...] = jnp.zeros_like(acc_scr)

    def scores(kb, masked):
        start = pl.multiple_of(kb * tq, tq)
        s = _dot_nt(ka_scr[pl.ds(start, tq), :], qa)
        if masked:
            krow = lax.broadcasted_iota(jnp.int32, (tq, tq), 0)
            qcol = lax.broadcasted_iota(jnp.int32, (tq, tq), 1)
            keep = krow <= qcol
            s = jnp.where(jnp.concatenate([keep, keep], axis=1), s, NEG)
        col_max = jnp.max(s, axis=0, keepdims=True)
        if fox:
            col_max = col_max + fq
        return start, s, col_max

    def accumulate(start, s, col_max):
        m_old = m_scr[...]
        m_new = jnp.maximum(m_old, col_max)
        shift = (m_new - fq) if fox else m_new
        pr = jnp.exp2(s - shift).astype(BF16)
        alpha = jnp.exp2(m_old - m_new)
        acc_scr[...] = alpha * acc_scr[...] + _dot(vt_scr[:, pl.ds(start, tq)], pr)
        m_scr[...] = m_new

    def group(kb0, n, last_masked):
        parts = [scores(kb0 + i, last_masked and i == n - 1) for i in range(n)]
        for part in parts:
            accumulate(*part)

    def body(i, carry):
        group(_ATTN_GROUP * i, _ATTN_GROUP, False)
        return carry

    lax.fori_loop(0, qi // _ATTN_GROUP, body, 0)
    rest = qi % _ATTN_GROUP
    for n_full in range(_ATTN_GROUP):
        @pl.when(rest == n_full)
        def _(n_full=n_full):
            group(qi - n_full, n_full + 1, True)

    acc = acc_scr[...]
    outs_t = [acc[0:LANES, c * tq:(c + 1) * tq] / acc[LANES:LANES + 1, c * tq:(c + 1) * tq] for c in range(2)]
    if fox:
        o_ref[0] = jnp.concatenate([outs_t[0][:half], outs_t[1][half:]], axis=0).T
    else:
        lp = lam_ref[...]
        lam = (jnp.exp(jnp.sum(lp[0:1] * lp[1:2], axis=1, keepdims=True))
               - jnp.exp(jnp.sum(lp[2:3] * lp[3:4], axis=1, keepdims=True)) + lam_init)
        o = (outs_t[0] - lam * outs_t[1]).T
        o_ref[0] = _rms(o, g_ref[...]) * (1.0 - lam_init)


def _attn_prompt(fox, q, kv, extra_a, extra_b, lam_init):
    b, t, w = q.shape
    groups = w // LANES
    tq = min(512, t)
    kernel = functools.partial(_attn_prompt_kernel, fox=fox, tq=tq, lam_init=lam_init)
    if fox:
        extra_specs = [pl.BlockSpec((1, t, FOX_HEADS), lambda i, p, j: (i, 0, 0)),
                       pl.BlockSpec((1, FOX_HEADS, tq), lambda i, p, j: (i, 0, j))]
    else:
        extra_specs = [pl.BlockSpec(extra_a.shape, lambda i, p, j: (0, 0)),
                       pl.BlockSpec(extra_b.shape, lambda i, p, j: (0, 0))]
    return pl.pallas_call(
        kernel,
        out_shape=jax.ShapeDtypeStruct((b, t, w), F32),
        grid=(b, groups, t // tq),
        in_specs=[
            pl.BlockSpec((1, tq, LANES), lambda i, p, j: (i, j, p)),
            pl.BlockSpec((1, t, LANES), lambda i, p, j: (i, 0, p)),
            pl.BlockSpec((1, t, LANES), lambda i, p, j: (i, 0, groups + p)),
        ] + extra_specs,
        out_specs=pl.BlockSpec((1, tq, LANES), lambda i, p, j: (i, j, p)),
        scratch_shapes=[pltpu.VMEM((t, 2 * LANES if fox else LANES), BF16), pltpu.VMEM((_V_ROWS, t), BF16),
                        pltpu.VMEM((1, 2 * tq), F32), pltpu.VMEM((_V_ROWS, 2 * tq), F32)],
        compiler_params=_params("parallel", "parallel", "arbitrary"),
        name="fox_prompt" if fox else "diff_prompt",
    )(q, kv, kv, extra_a, extra_b)


_PAGES_PER_STEP = 8


def _decode_kernel(pt_ref, qf_ref, qd_ref, nkf_ref, nkd_ref, nlf_ref, lam_ref, g_ref, sfx_ref, *rest,
                   pps, lam_init):
    del pt_ref
    page_refs = rest[:3 * pps]
    of_ref, od_ref = rest[3 * pps:3 * pps + 2]
    mf_scr, lf_scr, af_scr, md_scr, ld_scr, ad_scr, carry_scr = rest[3 * pps + 2:]
    b = pl.program_id(0)
    j = pl.program_id(1)
    scale = FOX_DH ** -0.5
    dj = 2 * DIFF_HEADS
    hd = 2 * DIFF_DH

    q_row = qf_ref[0]
    rf = lax.broadcasted_iota(jnp.int32, (FOX_HEADS, FOX_W), 0)
    lf_ = lax.broadcasted_iota(jnp.int32, (FOX_HEADS, FOX_W), 1)
    own = (lf_ // FOX_DH) == rf
    qblk = jnp.where(own, jnp.broadcast_to(q_row, (FOX_HEADS, FOX_W)), 0.0)
    qblk_b = qblk.astype(BF16)

    qd_row = qd_ref[0]
    r8 = lax.broadcasted_iota(jnp.int32, (dj, hd), 0)
    l8 = lax.broadcasted_iota(jnp.int32, (dj, hd), 1)
    head_of_row = r8 % DIFF_HEADS
    in_map = (l8 // DIFF_DH) == (r8 // DIFF_HEADS)

    def rows_from(vec, offset):
        out = jnp.zeros((dj, hd), F32)
        for h in range(DIFF_HEADS):
            piece = jnp.broadcast_to(vec[:, offset + h * hd:offset + (h + 1) * hd], (dj, hd))
            out = jnp.where(head_of_row == h, piece, out)
        return out

    q8 = jnp.where(in_map, rows_from(qd_row, 0), 0.0)
    q8_h = [jnp.where(head_of_row == h, q8, 0.0).astype(BF16) for h in range(DIFF_HEADS)]

    @pl.when(j == 0)
    def _():
        kvn = nkf_ref[0]
        kn = jnp.broadcast_to(kvn[:, 0:FOX_W], (FOX_HEADS, FOX_W))
        mf_scr[...] = jnp.broadcast_to(jnp.sum(qblk * kn, axis=1, keepdims=True) * scale, mf_scr.shape)
        lf_scr[...] = jnp.ones_like(lf_scr)
        af_scr[...] = jnp.broadcast_to(kvn[:, FOX_W:], (FOX_HEADS, FOX_W))
        kvd = nkd_ref[0]
        md_scr[...] = jnp.broadcast_to(
            jnp.sum(q8 * rows_from(kvd, 0), axis=1, keepdims=True) * scale, md_scr.shape)
        ld_scr[...] = jnp.ones_like(ld_scr)
        ad_scr[...] = rows_from(kvd, DIFF_W)
        nl = nlf_ref[...]
        sl = lax.broadcasted_iota(jnp.int32, nl.shape, 1)
        carry_scr[...] = jnp.broadcast_to(
            jnp.sum(jnp.where(sl == b, nl, 0.0), axis=1, keepdims=True), carry_scr.shape)

    carry = carry_scr[:, 0:1]
    s_f = []
    for i in range(pps):
        kt = page_refs[3 * i][0, 0].astype(BF16)
        lt = page_refs[3 * i + 1][0]
        hi, mid, lo = _split3(lt)
        r3 = _dot(jnp.concatenate([hi, mid, lo], axis=0), sfx_ref[...])
        bias = carry + r3[0:8] + r3[8:16] + r3[16:24]
        s_f.append(_dot(qblk_b, kt) * scale + bias)
        carry = carry + jnp.sum(lt, axis=1, keepdims=True)
    carry_scr[...] = jnp.broadcast_to(carry, carry_scr.shape)

    page = page_refs[2].shape[1] // dj
    s_d = []
    for i in range(pps):
        xd = page_refs[3 * i + 2]
        s = jnp.zeros((dj, page), F32)
        for h in range(DIFF_HEADS):
            k_h = xd[0, pl.ds(h, page, stride=dj), :].astype(BF16)
            s = s + _dot_nt(q8_h[h], k_h)
        s_d.append(s * scale)

    m_old = mf_scr[:, 0:1]
    m_new = jnp.maximum(m_old, jnp.max(functools.reduce(jnp.maximum, s_f), axis=1, keepdims=True))
    alpha = jnp.exp(m_old - m_new)
    l_add = jnp.zeros((FOX_HEADS, 1), F32)
    pv = jnp.zeros((FOX_HEADS, FOX_W), F32)
    for i in range(pps):
        pr = jnp.exp(s_f[i] - m_new)
        l_add = l_add + jnp.sum(pr, axis=1, keepdims=True)
        vt = page_refs[3 * i][0, 1].astype(BF16)
        pv = pv + _dot_nt(pr.astype(BF16), vt)
    lf_scr[...] = alpha * lf_scr[...] + l_add
    af_scr[...] = alpha * af_scr[...] + pv
    mf_scr[...] = jnp.broadcast_to(m_new, mf_scr.shape)

    md_old = md_scr[:, 0:1]
    md_new = jnp.maximum(md_old, jnp.max(functools.reduce(jnp.maximum, s_d), axis=1, keepdims=True))
    alphad = jnp.exp(md_old - md_new)
    ld_add = jnp.zeros((dj, 1), F32)
    pvd = jnp.zeros((dj, hd), F32)
    for i in range(pps):
        xd = page_refs[3 * i + 2]
        pr = jnp.exp(s_d[i] - md_new)
        ld_add = ld_add + jnp.sum(pr, axis=1, keepdims=True)
        for h in range(DIFF_HEADS):
            v_h = xd[0, pl.ds(DIFF_HEADS + h, page, stride=dj), :].astype(BF16)
            pvd = pvd + _dot(jnp.where(head_of_row == h, pr, 0.0).astype(BF16), v_h)
    ld_scr[...] = alphad * ld_scr[...] + ld_add
    ad_scr[...] = alphad * ad_scr[...] + pvd
    md_scr[...] = jnp.broadcast_to(md_new, md_scr.shape)

    @pl.when(j == pl.num_programs(1) - 1)
    def _():
        o_all = af_scr[...] / lf_scr[:, 0:1]
        of_ref[0] = jnp.sum(jnp.where(own, o_all, 0.0), axis=0, keepdims=True)
        od_all = ad_scr[...] / ld_scr[:, 0:1]
        lp = lam_ref[...]
        lam = (jnp.exp(jnp.sum(lp[0:1] * lp[1:2], axis=1, keepdims=True))
               - jnp.exp(jnp.sum(lp[2:3] * lp[3:4], axis=1, keepdims=True)) + lam_init)
        o = od_all[0:DIFF_HEADS] - lam * od_all[DIFF_HEADS:]
        od_ref[0] = _rms(o, g_ref[...]) * (1.0 - lam_init)


def _decode(page_table, qf, qd, new_kvf, new_kvd, new_lft, lam_p, subln_g, cache_ft, cache_lt, cache_d, lam_init):
    nb, n_pages = page_table.shape
    page = cache_lt.shape[2]
    pps = math.gcd(_PAGES_PER_STEP, n_pages)
    dj, hd = 2 * DIFF_HEADS, 2 * DIFF_DH
    kk = lax.broadcasted_iota(jnp.int32, (page, page), 0)
    kc = lax.broadcasted_iota(jnp.int32, (page, page), 1)
    sfx = jnp.where(kk > kc, 1.0, 0.0).astype(BF16)

    def page_idx(i):
        return lambda s, j, pt: pt[s, n_pages - 1 - (j * pps + i)]

    page_specs = []
    page_args = []
    for i in range(pps):
        pick = page_idx(i)
        page_specs += [
            pl.BlockSpec((1, 2, FOX_W, page), lambda s, j, pt, pick=pick: (pick(s, j, pt), 0, 0, 0)),
            pl.BlockSpec((1, FOX_HEADS, page), lambda s, j, pt, pick=pick: (pick(s, j, pt), 0, 0)),
            pl.BlockSpec((1, page * dj, hd), lambda s, j, pt, pick=pick: (pick(s, j, pt), 0, 0)),
        ]
        page_args += [cache_ft, cache_lt, cache_d]
    samp = lambda s, j, pt: (s, 0, 0)
    const = lambda s, j, pt: (0, 0)
    grid_spec = pltpu.PrefetchScalarGridSpec(
        num_scalar_prefetch=1,
        grid=(nb, n_pages // pps),
        in_specs=[
            pl.BlockSpec((1, 1, FOX_W), samp),
            pl.BlockSpec((1, 1, DIFF_W), samp),
            pl.BlockSpec((1, 1, 2 * FOX_W), samp),
            pl.BlockSpec((1, 1, 2 * DIFF_W), samp),
            pl.BlockSpec(new_lft.shape, const),
            pl.BlockSpec(lam_p.shape, const),
            pl.BlockSpec(subln_g.shape, const),
            pl.BlockSpec(sfx.shape, const),
        ] + page_specs,
        out_specs=[
            pl.BlockSpec((1, 1, FOX_W), samp),
            pl.BlockSpec((1, DIFF_HEADS, hd), samp),
        ],
        scratch_shapes=[
            pltpu.VMEM((FOX_HEADS, LANES), F32), pltpu.VMEM((FOX_HEADS, LANES), F32),
            pltpu.VMEM((FOX_HEADS, FOX_W), F32),
            pltpu.VMEM((dj, LANES), F32), pltpu.VMEM((dj, LANES), F32),
            pltpu.VMEM((dj, hd), F32),
            pltpu.VMEM((FOX_HEADS, LANES), F32),
        ],
    )
    return pl.pallas_call(
        functools.partial(_decode_kernel, pps=pps, lam_init=lam_init),
        out_shape=[jax.ShapeDtypeStruct((nb, 1, FOX_W), F32),
                   jax.ShapeDtypeStruct((nb, DIFF_HEADS, hd), F32)],
        grid_spec=grid_spec,
        compiler_params=_params("parallel", "arbitrary"),
        name="decode_attn",
    )(page_table, qf, qd, new_kvf, new_kvd, new_lft, lam_p, subln_g, sfx, *page_args)


def _rec_proj_kernel(x_ref, g_ref, w_ref, wgt_ref, bg_ref, q_ref, k_ref, v_ref, og_ref, u_ref, gt_ref):
    h = _rms(x_ref[0], g_ref[...]).astype(BF16)
    c0 = 0
    q_ref[0] = _dot(h, w_ref[:, c0:c0 + MLSTM_QK]) * (MLSTM_DK ** -0.5)
    c0 += MLSTM_QK
    k_ref[0] = _dot(h, w_ref[:, c0:c0 + MLSTM_QK])
    c0 += MLSTM_QK
    v_ref[0] = _dot(h, w_ref[:, c0:c0 + MLSTM_V])
    c0 += MLSTM_V
    og_ref[0] = _dot(h, w_ref[:, c0:c0 + MLSTM_V])
    c0 += MLSTM_V
    ua = _dot(h, w_ref[:, c0:c0 + CONV_CH])
    c0 += CONV_CH
    ub = _dot(h, w_ref[:, c0:c0 + CONV_CH])
    u_ref[0] = ua * jax.nn.sigmoid(ub)
    z = _dot_nt(wgt_ref[...], h) + bg_ref[...]
    rowi = lax.broadcasted_iota(jnp.int32, z.shape, 0)
    gt_ref[0] = jnp.where(rowi < MLSTM_HEADS, z, _log_sigmoid(z))


def _rec_proj(x, g, w_main, wgt, bg):
    b, t, d = x.shape
    tm = min(512, t)
    row = lambda i, j: (i, j, 0)
    const = lambda i, j: (0, 0)
    widths = [MLSTM_QK, MLSTM_QK, MLSTM_V, MLSTM_V, CONV_CH]
    return pl.pallas_call(
        _rec_proj_kernel,
        out_shape=[jax.ShapeDtypeStruct((b, t, w), F32) for w in widths]
        + [jax.ShapeDtypeStruct((b, 2 * MLSTM_HEADS, t), F32)],
        grid=(b, t // tm),
        in_specs=[
            pl.BlockSpec((1, tm, d), row),
            pl.BlockSpec((1, d), const),
            pl.BlockSpec(w_main.shape, const),
            pl.BlockSpec(wgt.shape, const),
            pl.BlockSpec(bg.shape, const),
        ],
        out_specs=[pl.BlockSpec((1, tm, w), row) for w in widths]
        + [pl.BlockSpec((1, 2 * MLSTM_HEADS, tm), lambda i, j: (i, 0, j))],
        compiler_params=_params("parallel", "parallel"),
        name="rec_proj",
    )(x, g.reshape(1, d), w_main, wgt, bg)


def _mlstm_prompt_kernel(q_ref, k_ref, v_ref, og_ref, gt_ref, gc_ref, ng_ref,
                         hm_ref, c_ref, n_ref, m_ref, ct_scr, n_scr, m_scr, *, chunk):
    t = pl.program_id(1)
    L = chunk
    half = LANES // 2

    @pl.when(t == 0)
    def _():
        ct_scr[...] = jnp.zeros_like(ct_scr)
        n_scr[...] = jnp.zeros_like(n_scr)
        m_scr[...] = jnp.full(m_scr.shape, NEG, F32)

    gt = gt_ref[0]
    gc = gc_ref[0]
    cum_rows = _dot3_left(gt, _tri(L, upper=True))
    cum_cols = _dot3_right(_tri(L, upper=False), gc)
    rr = lax.broadcasted_iota(jnp.int32, (L, L), 0)
    cc = lax.broadcasted_iota(jnp.int32, (L, L), 1)
    tri = cc <= rr
    lane = lax.broadcasted_iota(jnp.int32, (L, LANES), 1)

    heads = []
    for h in range(MLSTM_HEADS):
        pair, c = divmod(h, 2)
        qp = q_ref[0, :, pair * LANES:(pair + 1) * LANES]
        kp = k_ref[0, :, pair * LANES:(pair + 1) * LANES]
        mine = (lane < half) if c == 0 else (lane >= half)
        qm = jnp.where(mine, qp, 0.0)
        qm_b = qm.astype(BF16)
        qk = _dot_nt(qm_b, kp.astype(BF16))
        qc = _dot(qm_b, ct_scr[pair].astype(BF16))
        heads.append(dict(pair=pair, c=c, kp=kp, qm=qm, qk=qk, qc=qc))

    for h, hd_ in enumerate(heads):
        a_col = cum_cols[:, MLSTM_HEADS + h:MLSTM_HEADS + h + 1]
        ig_col = gc[:, h:h + 1]
        b_row = gt[h:h + 1, :] - cum_rows[MLSTM_HEADS + h:MLSTM_HEADS + h + 1, :]
        m_prev = m_scr[h:h + 1, 0:1]
        d = jnp.where(tri, a_col + b_row, NEG)
        m_inter = a_col + m_prev
        m_t = jnp.maximum(jnp.max(d, axis=1, keepdims=True), m_inter)
        s = hd_["qk"] * jnp.exp(d - m_t)
        w_inter = jnp.exp(m_inter - m_t)
        m_new = m_t[L - 1:L, :]
        cum_last = a_col[L - 1:L, :]
        decay = jnp.exp(cum_last + m_prev - m_new)
        w_key = jnp.exp(cum_last - a_col + ig_col - m_new)
        vh = v_ref[0, :, h * MLSTM_DV:(h + 1) * MLSTM_DV]
        hd_.update(s=s, w_inter=w_inter, m_t=m_t, m_new=m_new, decay=decay, w_key=w_key, vh=vh)

    for hd_ in heads:
        hd_["sv"] = _dot(hd_["s"].astype(BF16), hd_["vh"].astype(BF16))
        kt_b = hd_["kp"].T.astype(BF16)
        hd_["upd"] = _dot(kt_b, (hd_["w_key"] * hd_["vh"]).astype(BF16))

    lane1 = lax.broadcasted_iota(jnp.int32, (1, LANES), 1)
    for pair in range(MLSTM_HEADS // 2):
        ct = ct_scr[pair]
        n_row = n_scr[pair]
        ct_new = []
        n_new = []
        for c in range(2):
            h = 2 * pair + c
            hd_ = heads[h]
            num = hd_["sv"] + hd_["w_inter"] * hd_["qc"]
            den = (jnp.sum(hd_["s"], axis=1, keepdims=True)
                   + hd_["w_inter"] * jnp.sum(hd_["qm"] * n_row, axis=1, keepdims=True))
            hout = num / jnp.maximum(jnp.abs(den), jnp.exp(-hd_["m_t"]))
            gh = ng_ref[:, h * MLSTM_DV:(h + 1) * MLSTM_DV]
            oh = og_ref[0, :, h * MLSTM_DV:(h + 1) * MLSTM_DV]
            hm_ref[0, :, h * MLSTM_DV:(h + 1) * MLSTM_DV] = _rms(hout, gh) * jax.nn.sigmoid(oh)
            lo, hi_ = c * half, (c + 1) * half
            ct_new.append(hd_["decay"] * ct[lo:hi_] + hd_["upd"][lo:hi_])
            n_new.append(hd_["decay"] * n_row + jnp.sum(hd_["w_key"] * hd_["kp"], axis=0, keepdims=True))
            m_scr[h:h + 1, :] = jnp.broadcast_to(hd_["m_new"], (1, LANES))
        ct_scr[pair] = jnp.concatenate(ct_new, axis=0)
        n_scr[pair] = jnp.where(lane1 < half, n_new[0], n_new[1])

    @pl.when(t == pl.num_programs(1) - 1)
    def _():
        for pair in range(MLSTM_HEADS // 2):
            c_pair = ct_scr[pair].T
            n_row = n_scr[pair]
            for c in range(2):
                h = 2 * pair + c
                c_ref[0, h] = c_pair[:, c * half:(c + 1) * half]
                n_ref[0, h:h + 1, :] = n_row[:, c * half:(c + 1) * half]
                m_ref[0, :, h:h + 1] = m_scr[h:h + 1, 0:1]


def _mlstm_prompt(q, k, v, og, gt, gc, norm_g):
    b, t, _ = q.shape
    chunk = min(256, t)
    row = lambda i, j: (i, j, 0)
    const = lambda i, j: (0, 0)
    return pl.pallas_call(
        functools.partial(_mlstm_prompt_kernel, chunk=chunk),
        out_shape=[
            jax.ShapeDtypeStruct((b, t, MLSTM_V), F32),
            jax.ShapeDtypeStruct((b, MLSTM_HEADS, MLSTM_DV, MLSTM_DK), F32),
            jax.ShapeDtypeStruct((b, MLSTM_HEADS, MLSTM_DK), F32),
            jax.ShapeDtypeStruct((b, 1, MLSTM_HEADS), F32),
        ],
        grid=(b, t // chunk),
        in_specs=[
            pl.BlockSpec((1, chunk, MLSTM_QK), row),
            pl.BlockSpec((1, chunk, MLSTM_QK), row),
            pl.BlockSpec((1, chunk, MLSTM_V), row),
            pl.BlockSpec((1, chunk, MLSTM_V), row),
            pl.BlockSpec((1, 2 * MLSTM_HEADS, chunk), lambda i, j: (i, 0, j)),
            pl.BlockSpec((1, chunk, 2 * MLSTM_HEADS), row),
            pl.BlockSpec((1, MLSTM_V), const),
        ],
        out_specs=[
            pl.BlockSpec((1, chunk, MLSTM_V), row),
            pl.BlockSpec((1, MLSTM_HEADS, MLSTM_DV, MLSTM_DK), lambda i, j: (i, 0, 0, 0)),
            pl.BlockSpec((1, MLSTM_HEADS, MLSTM_DK), lambda i, j: (i, 0, 0)),
            pl.BlockSpec((1, 1, MLSTM_HEADS), lambda i, j: (i, 0, 0)),
        ],
        scratch_shapes=[
            pltpu.VMEM((MLSTM_HEADS // 2, LANES, MLSTM_DV), F32),
            pltpu.VMEM((MLSTM_HEADS // 2, 1, LANES), F32),
            pltpu.VMEM((SUBLANES, LANES), F32),
        ],
        compiler_params=_params("parallel", "arbitrary"),
        name="mlstm_prompt",
    )(q, k, v, og, gt, gc, norm_g.reshape(1, MLSTM_V))


def _mlstm_sample_kernel(q_ref, k_ref, v_ref, og_ref, gc_ref, m_ref, kall_ref, vall_ref, ng_ref, c_ref, n_ref,
                         hm_ref, co_ref, no_ref, mo_ref, kt_scr, *, bs):
    i = pl.program_id(0)
    nb = kall_ref.shape[0]
    half = LANES // 2

    @pl.when(i == 0)
    def _():
        for pair in range(MLSTM_HEADS // 2):
            kt_scr[pair] = kall_ref[:, pair * LANES:(pair + 1) * LANES].T

    rows = lax.broadcasted_iota(jnp.int32, (nb, MLSTM_DV), 0)

    def body(bl, _):
        b = i * bs + bl
        q_row = q_ref[bl]
        k_row = k_ref[bl]
        v_row = v_ref[bl]
        og_row = og_ref[bl]
        g_row = gc_ref[bl]
        m_row = m_ref[bl]
        cms, cqs, outers = [], [], []
        for h in range(MLSTM_HEADS):
            qh = q_row[:, h * MLSTM_DK:(h + 1) * MLSTM_DK]
            cm = c_ref[bl, h]
            v_all = vall_ref[:, h * MLSTM_DV:(h + 1) * MLSTM_DV]
            v_sel = jnp.where(rows == b, v_all, 0.0).astype(BF16)
            pair, c = divmod(h, 2)
            kt_h = kt_scr[pair, c * half:(c + 1) * half, :]
            cms.append(cm)
            cqs.append(_dot(qh.astype(BF16), cm.astype(BF16)))
            outers.append(_dot(kt_h.astype(BF16), v_sel))
        for h in range(MLSTM_HEADS):
            qh = q_row[:, h * MLSTM_DK:(h + 1) * MLSTM_DK]
            kh = k_row[:, h * MLSTM_DK:(h + 1) * MLSTM_DK]
            vh = v_row[:, h * MLSTM_DV:(h + 1) * MLSTM_DV]
            it = g_row[:, h:h + 1]
            lf = g_row[:, MLSTM_HEADS + h:MLSTM_HEADS + h + 1]
            m_prev = m_row[:, h:h + 1]
            m_inter = lf + m_prev
            m_t = jnp.maximum(it, m_inter)
            e_i = jnp.exp(it - m_t)
            w_inter = jnp.exp(m_inter - m_t)
            s = jnp.sum(qh * kh, axis=1, keepdims=True) * e_i
            nh = n_ref[bl, h:h + 1, :]
            num = s * vh + w_inter * cqs[h]
            den = s + w_inter * jnp.sum(nh * qh, axis=1, keepdims=True)
            hrow = num / jnp.maximum(jnp.abs(den), jnp.exp(-m_t))
            gh = ng_ref[:, h * MLSTM_DV:(h + 1) * MLSTM_DV]
            oh = og_row[:, h * MLSTM_DV:(h + 1) * MLSTM_DV]
            hm_ref[bl, :, h * MLSTM_DV:(h + 1) * MLSTM_DV] = _rms(hrow, gh) * jax.nn.sigmoid(oh)
            co_ref[bl, h] = w_inter * cms[h] + e_i * outers[h]
            no_ref[bl, h:h + 1, :] = w_inter * nh + e_i * kh
            mo_ref[bl, :, h:h + 1] = m_t
        return 0

    lax.fori_loop(0, bs, body, 0)


def _mlstm_sample(q, k, v, og, gc, norm_g, c0_t, n0, m0):
    nb = q.shape[0]
    bs = min(16, nb)
    full = lambda i: (0, 0)
    rows = lambda i: (i, 0, 0)
    per_sample = [a.reshape(nb, 1, a.shape[1]) for a in (q, k, v, og, gc, m0)]
    hm, c1, n1, m1 = pl.pallas_call(
        functools.partial(_mlstm_sample_kernel, bs=bs),
        out_shape=[
            jax.ShapeDtypeStruct((nb, 1, MLSTM_V), F32),
            jax.ShapeDtypeStruct(c0_t.shape, F32),
            jax.ShapeDtypeStruct(n0.shape, F32),
            jax.ShapeDtypeStruct((nb, 1, MLSTM_HEADS), F32),
        ],
        grid=(nb // bs,),
        in_specs=[pl.BlockSpec((bs, 1, a.shape[2]), rows) for a in per_sample] + [
            pl.BlockSpec(k.shape, full),
            pl.BlockSpec(v.shape, full),
            pl.BlockSpec((1, MLSTM_V), full),
            pl.BlockSpec((bs, MLSTM_HEADS, MLSTM_DK, MLSTM_DV), lambda i: (i, 0, 0, 0)),
            pl.BlockSpec((bs, MLSTM_HEADS, MLSTM_DK), rows),
        ],
        out_specs=[
            pl.BlockSpec((bs, 1, MLSTM_V), rows),
            pl.BlockSpec((bs, MLSTM_HEADS, MLSTM_DK, MLSTM_DV), lambda i: (i, 0, 0, 0)),
            pl.BlockSpec((bs, MLSTM_HEADS, MLSTM_DK), rows),
            pl.BlockSpec((bs, 1, MLSTM_HEADS), rows),
        ],
        scratch_shapes=[pltpu.VMEM((MLSTM_HEADS // 2, LANES, nb), F32)],
        compiler_params=_params("arbitrary"),
        name="mlstm_sample",
    )(*per_sample, k, v, norm_g.reshape(1, MLSTM_V), c0_t, n0)
    return hm.reshape(nb, MLSTM_V), c1, n1, m1.reshape(nb, MLSTM_HEADS)


_CONV_HALO = 32
_CONV_ROWS = 32


def _ln_silu(y, g, b):
    yc = y - jnp.mean(y, axis=-1, keepdims=True)
    var = jnp.mean(yc * yc, axis=-1, keepdims=True)
    z = yc * lax.rsqrt(var + NORM_EPS) * g + b
    return z * jax.nn.sigmoid(z)


def _conv_prompt_kernel(u_ref, prev_ref, w_ref, b_ref, g_ref, be_ref, c_ref, st_ref, full_scr, shift_scr, *, tt):
    t = pl.program_id(1)
    pad = _CONV_HALO - (CONV_W - 1)

    @pl.when(t == 0)
    def _():
        full_scr[0:_CONV_HALO, :] = jnp.zeros((_CONV_HALO, CONV_CH), F32)
        full_scr[pad:_CONV_HALO, :] = prev_ref[0]

    full_scr[_CONV_HALO:, :] = u_ref[0]
    span = tt + _CONV_HALO - SUBLANES
    for r in range(1, SUBLANES):
        shift_scr[r, 0:span, :] = full_scr[r:r + span, :]
    for r0 in range(0, tt, _CONV_ROWS):
        acc = jnp.broadcast_to(b_ref[...], (_CONV_ROWS, CONV_CH))
        for k in range(CONV_W):
            off = r0 + pad + k
            r = off % SUBLANES
            if r == 0:
                rows = full_scr[off:off + _CONV_ROWS, :]
            else:
                rows = shift_scr[r, off - r:off - r + _CONV_ROWS, :]
            acc = acc + w_ref[k:k + 1, :] * rows
        c_ref[0, r0:r0 + _CONV_ROWS, :] = _ln_silu(acc, g_ref[...], be_ref[...])

    @pl.when(t == pl.num_programs(1) - 1)
    def _():
        st_ref[0] = full_scr[_CONV_HALO + tt - (CONV_W - 1):, :]

    full_scr[0:_CONV_HALO, :] = full_scr[tt:tt + _CONV_HALO, :]


def _conv_prompt(u, prev, w, bias, ln_g, ln_b):
    b, t, ch = u.shape
    tt = min(256, t)
    const = lambda i, j: (0, 0)
    return pl.pallas_call(
        functools.partial(_conv_prompt_kernel, tt=tt),
        out_shape=[jax.ShapeDtypeStruct((b, t, ch), F32),
                   jax.ShapeDtypeStruct((b, CONV_W - 1, ch), F32)],
        grid=(b, t // tt),
        in_specs=[
            pl.BlockSpec((1, tt, ch), lambda i, j: (i, j, 0)),
            pl.BlockSpec((1, CONV_W - 1, ch), lambda i, j: (i, 0, 0)),
            pl.BlockSpec((CONV_W, ch), const),
            pl.BlockSpec((1, ch), const),
            pl.BlockSpec((1, ch), const),
            pl.BlockSpec((1, ch), const),
        ],
        out_specs=[pl.BlockSpec((1, tt, ch), lambda i, j: (i, j, 0)),
                   pl.BlockSpec((1, CONV_W - 1, ch), lambda i, j: (i, 0, 0))],
        scratch_shapes=[pltpu.VMEM((_CONV_HALO + tt, ch), F32),
                        pltpu.VMEM((SUBLANES, _CONV_HALO + tt, ch), F32)],
        compiler_params=_params("parallel", "arbitrary"),
        name="conv_prompt",
    )(u, prev, w, bias.reshape(1, ch), ln_g.reshape(1, ch), ln_b.reshape(1, ch))


def _conv_sample_kernel(u_ref, prev_ref, w_ref, b_ref, g_ref, be_ref, c_ref, st_ref):
    hist = CONV_W - 1
    u = u_ref[...]
    y = w_ref[hist:hist + 1, :] * u + b_ref[...]
    for k in range(hist):
        row = prev_ref[k]
        y = y + w_ref[k:k + 1, :] * row
        if k > 0:
            st_ref[k - 1] = row
    st_ref[hist - 1] = u
    c_ref[...] = _ln_silu(y, g_ref[...], be_ref[...])


def _conv_sample(u, prev_t, w, bias, ln_g, ln_b):
    nb, ch = u.shape
    bs = min(32, nb)
    hist = CONV_W - 1
    const = lambda i: (0, 0)
    return pl.pallas_call(
        _conv_sample_kernel,
        out_shape=[jax.ShapeDtypeStruct((nb, ch), F32),
                   jax.ShapeDtypeStruct((hist, nb, ch), F32)],
        grid=(nb // bs,),
        in_specs=[
            pl.BlockSpec((bs, ch), lambda i: (i, 0)),
            pl.BlockSpec((hist, bs, ch), lambda i: (0, i, 0)),
            pl.BlockSpec((CONV_W, ch), const),
            pl.BlockSpec((1, ch), const),
            pl.BlockSpec((1, ch), const),
            pl.BlockSpec((1, ch), const),
        ],
        out_specs=[pl.BlockSpec((bs, ch), lambda i: (i, 0)),
                   pl.BlockSpec((hist, bs, ch), lambda i: (0, i, 0))],
        compiler_params=_params("parallel"),
        name="conv_sample",
    )(u, prev_t, w, bias.reshape(1, ch), ln_g.reshape(1, ch), ln_b.reshape(1, ch))


def kernel(x_prompt, x_sample, cache_fox_kv, cache_fox_logf, cache_diff_kv, state_mlstm_C, state_mlstm_n,
           state_mlstm_m, state_conv, page_table, norm_g, final_g, ffn_w_in, ffn_w_out, attn_w_in, attn_b_f,
           diff_lam, diff_subln_g, attn_w_out, rec_w_in, rec_b_i, rec_b_f, mlstm_norm_g, conv_w, conv_b,
           conv_ln_g, conv_ln_b, rec_w_out):
    bp, t, d = x_prompt.shape
    nb = x_sample.shape[0]
    depth = norm_g.shape[0]
    n_pool, page = cache_fox_kv.shape[1], cache_fox_kv.shape[2]
    xp = x_prompt.reshape(bp * t, d)
    xs = x_sample.reshape(nb, d)
    w_in_b = ffn_w_in
    w_out_b = ffn_w_out.astype(BF16)

    outs = {k: [] for k in ("fkv_p", "fkv_s", "flf_p", "flf_s", "dkv_p", "dkv_s",
                            "c_p", "c_s", "n_p", "n_s", "m_p", "m_s", "cv_p", "cv_s")}
    for l in range(depth):
        j = l // 2
        xp = _ffn(xp, norm_g[l, 0], w_in_b, w_out_b, l, 0)
        xs = _ffn(xs, norm_g[l, 0], w_in_b, w_out_b, l, 0)
        if l % 2 == 0:
            lam_init = 0.8 - 0.6 * math.exp(-0.3 * l)
            w = attn_w_in[j]
            c1, c2 = 3 * FOX_W, 3 * FOX_W + FOX_HEADS
            w_main = jnp.concatenate([w[:, :c1], w[:, c2:]], axis=1).astype(BF16)
            wft = w[:, c1:c2].T.astype(BF16)
            bf = attn_b_f[j].reshape(FOX_HEADS, 1)
            w_o = attn_w_out[j].astype(BF16)
            qf, kvf, qd, kvd, lft, cumt, kvd_rows = _attn_proj(xp.reshape(bp, t, d), norm_g[l, 1], w_main, wft, bf)
            o_f = _attn_prompt(True, qf, kvf, jnp.swapaxes(cumt, 1, 2), cumt, lam_init)
            o_d = _attn_prompt(False, qd, kvd, diff_lam[j], diff_subln_g[j].reshape(1, -1), lam_init)
            xp = _merge(xp, o_f.reshape(bp * t, FOX_W), o_d.reshape(bp * t, DIFF_W), w_o)
            outs["fkv_p"].append(kvf.reshape(bp, t, 2, FOX_HEADS, FOX_DH))
            outs["flf_p"].append(jnp.swapaxes(lft, 1, 2))
            outs["dkv_p"].append(kvd_rows.reshape(bp, t, 2, DIFF_HEADS, 2 * DIFF_DH))
            sqf, skvf, sqd, skvd, slft, _, skvd_rows = _attn_proj(xs.reshape(1, nb, d), norm_g[l, 1], w_main, wft, bf)
            so_f, so_d = _decode(
                page_table,
                sqf.reshape(nb, 1, FOX_W), sqd.reshape(nb, 1, DIFF_W),
                skvf.reshape(nb, 1, 2 * FOX_W), skvd.reshape(nb, 1, 2 * DIFF_W),
                slft[0], diff_lam[j], diff_subln_g[j].reshape(1, -1),
                jnp.transpose(cache_fox_kv[j], (0, 2, 3, 4, 1)).reshape(n_pool, 2, FOX_W, page),
                jnp.swapaxes(cache_fox_logf[j], 1, 2),
                cache_diff_kv[j].reshape(n_pool, page * 2 * DIFF_HEADS, 2 * DIFF_DH),
                lam_init)
            xs = _merge(xs, so_f.reshape(nb, FOX_W), so_d.reshape(nb, DIFF_W), w_o)
            outs["fkv_s"].append(skvf.reshape(nb, 1, 2, FOX_HEADS, FOX_DH))
            outs["flf_s"].append(jnp.swapaxes(slft, 1, 2).reshape(nb, 1, FOX_HEADS))
            outs["dkv_s"].append(skvd_rows.reshape(nb, 1, 2, DIFF_HEADS, 2 * DIFF_DH))
        else:
            w = rec_w_in[j]
            c1 = 2 * MLSTM_QK + MLSTM_V
            c2 = c1 + 2 * MLSTM_HEADS
            w_main = jnp.concatenate([w[:, :c1], w[:, c2:]], axis=1).astype(BF16)
            wgt = w[:, c1:c2].T.astype(BF16)
            bg = jnp.concatenate([rec_b_i[j], rec_b_f[j]]).reshape(2 * MLSTM_HEADS, 1)
            w_o = rec_w_out[j].astype(BF16)
            q, k, v, og, u, gt = _rec_proj(xp.reshape(bp, t, d), norm_g[l, 1], w_main, wgt, bg)
            hm, c_p, n_p, m_p = _mlstm_prompt(q, k, v, og, gt, jnp.swapaxes(gt, 1, 2), mlstm_norm_g[j])
            cv, st_p = _conv_prompt(u, jnp.zeros((bp, CONV_W - 1, CONV_CH), F32), conv_w[j], conv_b[j],
                                    conv_ln_g[j], conv_ln_b[j])
            xp = _merge(xp, hm.reshape(bp * t, MLSTM_V), cv.reshape(bp * t, CONV_CH), w_o)
            outs["c_p"].append(c_p)
            outs["n_p"].append(n_p)
            outs["m_p"].append(m_p.reshape(bp, MLSTM_HEADS))
            outs["cv_p"].append(st_p)
            sq, sk, sv, sog, su, sgt = _rec_proj(xs.reshape(1, nb, d), norm_g[l, 1], w_main, wgt, bg)
            shm, c_s_t, n_s, m_s = _mlstm_sample(sq[0], sk[0], sv[0], sog[0], sgt[0].T, mlstm_norm_g[j],
                                                 jnp.swapaxes(state_mlstm_C[j], -1, -2),
                                                 state_mlstm_n[j], state_mlstm_m[j])
            scv, st_s = _conv_sample(su[0], jnp.swapaxes(state_conv[j], 0, 1), conv_w[j], conv_b[j],
                                     conv_ln_g[j], conv_ln_b[j])
            xs = _merge(xs, shm, scv, w_o)
            outs["c_s"].append(jnp.swapaxes(c_s_t, -1, -2))
            outs["n_s"].append(n_s)
            outs["m_s"].append(m_s)
            outs["cv_s"].append(jnp.swapaxes(st_s, 0, 1))
        last = l == depth - 1
        xp = _ffn(xp, norm_g[l, 2], w_in_b, w_out_b, l, 1, final_g if last else None)
        xs = _ffn(xs, norm_g[l, 2], w_in_b, w_out_b, l, 1, final_g if last else None)

    st = jnp.stack
    return (xp.reshape(bp, t, d), xs.reshape(nb, 1, d),
            st(outs["fkv_p"]), st(outs["fkv_s"]), st(outs["flf_p"]), st(outs["flf_s"]),
            st(outs["dkv_p"]), st(outs["dkv_s"]),
            st(outs["c_p"]), st(outs["c_s"]), st(outs["n_p"]), st(outs["n_s"]),
            st(outs["m_p"]), st(outs["m_s"]), st(outs["cv_p"]), st(outs["cv_s"]))
```

```python
import functools
import math

import jax
import jax.numpy as jnp
from jax import lax
from jax.experimental import pallas as pl
from jax.experimental.pallas import tpu as pltpu

F32 = jnp.float32
BF16 = jnp.bfloat16
NORM_EPS = 1e-6
NEG = -1e30
LOG2E = 1.4426950408889634

FOX_HEADS = 8
FOX_DH = 64
DIFF_HEADS = 4
DIFF_DH = 64
MLSTM_HEADS = 4
MLSTM_DK = 64
MLSTM_DV = 128
CONV_CH = 512
CONV_W = 31
FOX_W = FOX_HEADS * FOX_DH
DIFF_W = DIFF_HEADS * 2 * DIFF_DH
MLSTM_QK = MLSTM_HEADS * MLSTM_DK
MLSTM_V = MLSTM_HEADS * MLSTM_DV

LANES = 128
SUBLANES = 8
VMEM_LIMIT_BYTES = 56 * 1024 * 1024

_NT = (((1,), (1,)), ((), ()))


def _params(*sem):
    return pltpu.CompilerParams(dimension_semantics=sem, vmem_limit_bytes=VMEM_LIMIT_BYTES)


def _rms(x, g):
    return x * lax.rsqrt(jnp.mean(x * x, axis=-1, keepdims=True) + NORM_EPS) * g


def _log_sigmoid(z):
    return jnp.minimum(z, 0.0) - jnp.log(1.0 + jnp.exp(-jnp.abs(z)))


def _split3(x):
    hi = x.astype(BF16)
    r = x - hi.astype(F32)
    mid = r.astype(BF16)
    lo = (r - mid.astype(F32)).astype(BF16)
    return hi, mid, lo


def _dot(a, b):
    return jnp.dot(a, b, preferred_element_type=F32)


def _dot_nt(a, b):
    return lax.dot_general(a, b, _NT, preferred_element_type=F32)


def _dot3_left(x, t):
    hi, mid, lo = _split3(x)
    return _dot(hi, t) + _dot(mid, t) + _dot(lo, t)


def _dot3_right(t, x):
    hi, mid, lo = _split3(x)
    return _dot(t, hi) + _dot(t, mid) + _dot(t, lo)


def _tri(n, upper):
    r = lax.broadcasted_iota(jnp.int32, (n, n), 0)
    c = lax.broadcasted_iota(jnp.int32, (n, n), 1)
    keep = (r <= c) if upper else (r >= c)
    return jnp.where(keep, 1.0, 0.0).astype(BF16)


def _ffn_kernel(x_ref, g_ref, wa_ref, wb_ref, wo_ref, fg_ref, o_ref, h_scr, gated_scr, *, final_norm):
    j = pl.program_id(1)
    tf = wa_ref.shape[1]

    @pl.when(j == 0)
    def _():
        h_scr[...] = _rms(x_ref[...], g_ref[...]).astype(BF16)

    h = h_scr[...]
    a = _dot(h, wa_ref[...])
    b = _dot(h, wb_ref[...])
    gated_scr[:, pl.ds(pl.multiple_of(j * tf, tf), tf)] = (a * jax.nn.sigmoid(a) * b).astype(BF16)

    @pl.when(j == pl.num_programs(1) - 1)
    def _():
        y = x_ref[...] + 0.5 * _dot(gated_scr[...], wo_ref[...])
        if final_norm:
            y = _rms(y, fg_ref[...])
        o_ref[...] = y


def _ffn(x, g, w_in, w_out, layer, which, final_g=None):
    m, d = x.shape
    f = w_out.shape[2]
    tm = min(1024, m)
    tf = 256 if m > 256 else f // 2
    nf = f // tf
    assert m % tm == 0 and f % tf == 0 and tf % LANES == 0, (m, f)
    fg = jnp.ones((1, d), F32) if final_g is None else final_g.reshape(1, d)
    return pl.pallas_call(
        functools.partial(_ffn_kernel, final_norm=final_g is not None),
        out_shape=jax.ShapeDtypeStruct((m, d), F32),
        grid=(m // tm, nf),
        in_specs=[
            pl.BlockSpec((tm, d), lambda i, j: (i, 0)),
            pl.BlockSpec((1, d), lambda i, j: (0, 0)),
            pl.BlockSpec((None, None, d, tf), lambda i, j: (layer, which, 0, j)),
            pl.BlockSpec((None, None, d, tf), lambda i, j: (layer, which, 0, j + nf)),
            pl.BlockSpec((None, None, f, d), lambda i, j: (layer, which, 0, 0)),
            pl.BlockSpec((1, d), lambda i, j: (0, 0)),
        ],
        out_specs=pl.BlockSpec((tm, d), lambda i, j: (i, 0)),
        scratch_shapes=[pltpu.VMEM((tm, d), BF16), pltpu.VMEM((tm, f), BF16)],
        compiler_params=_params("parallel", "arbitrary"),
        name="ffn",
    )(x, g.reshape(1, d), w_in, w_in, w_out, fg)


def _merge_kernel(x_ref, oa_ref, ob_ref, w_ref, o_ref):
    wa = oa_ref.shape[-1]
    y = _dot(oa_ref[...].astype(BF16), w_ref[:wa, :])
    y += _dot(ob_ref[...].astype(BF16), w_ref[wa:, :])
    o_ref[...] = x_ref[...] + y


def _merge(x, oa, ob, w_out):
    m, d = x.shape
    tm = min(1024, m)
    wa, wb = oa.shape[1], ob.shape[1]
    return pl.pallas_call(
        _merge_kernel,
        out_shape=jax.ShapeDtypeStruct((m, d), F32),
        grid=(m // tm,),
        in_specs=[
            pl.BlockSpec((tm, d), lambda i: (i, 0)),
            pl.BlockSpec((tm, wa), lambda i: (i, 0)),
            pl.BlockSpec((tm, wb), lambda i: (i, 0)),
            pl.BlockSpec((wa + wb, d), lambda i: (0, 0)),
        ],
        out_specs=pl.BlockSpec((tm, d), lambda i: (i, 0)),
        compiler_params=_params("parallel"),
        name="merge",
    )(x, oa, ob, w_out)


def _attn_proj_kernel(x_ref, g_ref, w_ref, wft_ref, bf_ref, qf_ref, kvf_ref, qd_ref, kvd_ref,
                      lf_ref, cum_ref, kvdr_ref, carry_scr):
    t = pl.program_id(1)
    h = _rms(x_ref[0], g_ref[...]).astype(BF16)
    qf_ref[0] = _dot(h, w_ref[:, 0:FOX_W])
    kvf_ref[0] = _dot(h, w_ref[:, FOX_W:3 * FOX_W])
    qd_ref[0] = _dot(h, w_ref[:, 3 * FOX_W:3 * FOX_W + DIFF_W])
    kvd = _dot(h, w_ref[:, 3 * FOX_W + DIFF_W:])
    kvd_ref[0] = kvd
    groups = kvd.shape[1] // LANES
    for gi in range(groups):
        kvdr_ref[0, pl.ds(gi, kvd.shape[0], stride=groups), :] = kvd[:, gi * LANES:(gi + 1) * LANES]
    logf = _log_sigmoid(_dot_nt(wft_ref[...], h) + bf_ref[...])
    lf_ref[0] = logf

    @pl.when(t == 0)
    def _():
        carry_scr[...] = jnp.zeros_like(carry_scr)

    tm = logf.shape[1]
    cum = _dot3_left(logf, _tri(tm, upper=True)) + carry_scr[:, 0:1]
    cum_ref[0] = cum
    carry_scr[...] = jnp.broadcast_to(cum[:, tm - 1:tm], carry_scr.shape)


def _attn_proj(x, g, w_main, wft, bf):
    b, t, d = x.shape
    tm = min(512, t)
    n_main = w_main.shape[1]
    row = lambda i, j: (i, j, 0)
    col = lambda i, j: (i, 0, j)
    const = lambda i, j: (0, 0)
    return pl.pallas_call(
        _attn_proj_kernel,
        out_shape=[
            jax.ShapeDtypeStruct((b, t, FOX_W), F32),
            jax.ShapeDtypeStruct((b, t, 2 * FOX_W), F32),
            jax.ShapeDtypeStruct((b, t, DIFF_W), F32),
            jax.ShapeDtypeStruct((b, t, 2 * DIFF_W), F32),
            jax.ShapeDtypeStruct((b, FOX_HEADS, t), F32),
            jax.ShapeDtypeStruct((b, FOX_HEADS, t), F32),
            jax.ShapeDtypeStruct((b, t * (2 * DIFF_W // LANES), LANES), F32),
        ],
        grid=(b, t // tm),
        in_specs=[
            pl.BlockSpec((1, tm, d), row),
            pl.BlockSpec((1, d), const),
            pl.BlockSpec((d, n_main), const),
            pl.BlockSpec((FOX_HEADS, d), const),
            pl.BlockSpec((FOX_HEADS, 1), const),
        ],
        out_specs=[
            pl.BlockSpec((1, tm, FOX_W), row),
            pl.BlockSpec((1, tm, 2 * FOX_W), row),
            pl.BlockSpec((1, tm, DIFF_W), row),
            pl.BlockSpec((1, tm, 2 * DIFF_W), row),
            pl.BlockSpec((1, FOX_HEADS, tm), col),
            pl.BlockSpec((1, FOX_HEADS, tm), col),
            pl.BlockSpec((1, tm * (2 * DIFF_W // LANES), LANES), row),
        ],
        scratch_shapes=[pltpu.VMEM((FOX_HEADS, LANES), F32)],
        compiler_params=_params("parallel", "arbitrary"),
        name="attn_proj",
    )(x, g.reshape(1, d), w_main, wft, bf)


_V_ROWS = 144
_ATTN_GROUP = 4


def _attn_prompt_kernel(*refs, fox, tq, lam_init):
    if fox:
        q_ref, k_ref, v_ref, ccol_ref, crow_ref, o_ref, ka_scr, vt_scr, m_scr, acc_scr = refs
    else:
        q_ref, k_ref, v_ref, lam_ref, g_ref, o_ref, ka_scr, vt_scr, m_scr, acc_scr = refs
    p = pl.program_id(1)
    qi = pl.program_id(2)
    half = LANES // 2
    t_all = k_ref.shape[1]

    @pl.when(qi == 0)
    def _():
        ka_scr[:, 0:LANES] = k_ref[0].astype(BF16)
        if fox:
            cc = ccol_ref[0]
            hl = lax.broadcasted_iota(jnp.int32, cc.shape, 1)
            lane = lax.broadcasted_iota(jnp.int32, (t_all, LANES), 1)
            extra = jnp.zeros((t_all, LANES), F32)
            for c in range(2):
                fk = jnp.sum(jnp.where(hl == 2 * p + c, cc, 0.0), axis=1, keepdims=True) * LOG2E
                for i, piece in enumerate(_split3(fk)):
                    extra = jnp.where(lane == 3 * c + i, piece.astype(F32), extra)
            ka_scr[:, LANES:] = extra.astype(BF16)
        vt_scr[0:LANES, :] = v_ref[0].T.astype(BF16)
        r = lax.broadcasted_iota(jnp.int32, (_V_ROWS - LANES, t_all), 0)
        vt_scr[LANES:, :] = jnp.where(r == 0, 1.0, 0.0).astype(BF16)

    q = q_ref[0] * (FOX_DH ** -0.5 * LOG2E)
    lane = lax.broadcasted_iota(jnp.int32, q.shape, 1)
    members = []
    for c in range(2):
        x = jnp.where((lane < half) if c == 0 else (lane >= half), q, 0.0)
        if fox:
            pick = (lane >= 3 * c) & (lane < 3 * c + 3)
            x = jnp.concatenate([x, jnp.where(pick, -1.0, 0.0)], axis=1)
        members.append(x)
    qa = jnp.concatenate(members, axis=0).astype(BF16)
    if fox:
        fq = jnp.concatenate([crow_ref[0, pl.ds(2 * p + c, 1), :] for c in range(2)], axis=1) * LOG2E
    m_scr[...] = jnp.full(m_scr.shape, NEG, F32)
    acc_scr[...] = jnp.zeros_like(acc_scr)

    def scores(kb, masked):
        start = pl.multiple_of(kb * tq, tq)
        s = _dot_nt(ka_scr[pl.ds(start, tq), :], qa)
        if masked:
            krow = lax.broadcasted_iota(jnp.int32, (tq, tq), 0)
            qcol = lax.broadcasted_iota(jnp.int32, (tq, tq), 1)
            keep = krow <= qcol
            s = jnp.where(jnp.concatenate([keep, keep], axis=1), s, NEG)
        col_max = jnp.max(s, axis=0, keepdims=True)
        if fox:
            col_max = col_max + fq
        return start, s, col_max

    def accumulate(start, s, col_max):
        m_old = m_scr[...]
        m_new = jnp.maximum(m_old, col_max)
        shift = (m_new - fq) if fox else m_new
        pr = jnp.exp2(s - shift).astype(BF16)
        alpha = jnp.exp2(m_old - m_new)
        acc_scr[...] = alpha * acc_scr[...] + _dot(vt_scr[:, pl.ds(start, tq)], pr)
        m_scr[...] = m_new

    def group(kb0, n, last_masked):
        parts = [scores(kb0 + i, last_masked and i == n - 1) for i in range(n)]
        for part in parts:
            accumulate(*part)

    def body(i, carry):
        group(_ATTN_GROUP * i, _ATTN_GROUP, False)
        return carry

    lax.fori_loop(0, qi // _ATTN_GROUP, body, 0)
    rest = qi % _ATTN_GROUP
    for n_full in range(_ATTN_GROUP):
        @pl.when(rest == n_full)
        def _(n_full=n_full):
            group(qi - n_full, n_full + 1, True)

    acc = acc_scr[...]
    outs_t = [acc[0:LANES, c * tq:(c + 1) * tq] / acc[LANES:LANES + 1, c * tq:(c + 1) * tq] for c in range(2)]
    if fox:
        o_ref[0] = jnp.concatenate([outs_t[0][:half], outs_t[1][half:]], axis=0).T
    else:
        lp = lam_ref[...]
        lam = (jnp.exp(jnp.sum(lp[0:1] * lp[1:2], axis=1, keepdims=True))
               - jnp.exp(jnp.sum(lp[2:3] * lp[3:4], axis=1, keepdims=True)) + lam_init)
        o = (outs_t[0] - lam * outs_t[1]).T
        o_ref[0] = _rms(o, g_ref[...]) * (1.0 - lam_init)


def _attn_prompt(fox, q, kv, extra_a, extra_b, lam_init):
    b, t, w = q.shape
    groups = w // LANES
    tq = min(512, t)
    kernel = functools.partial(_attn_prompt_kernel, fox=fox, tq=tq, lam_init=lam_init)
    if fox:
        extra_specs = [pl.BlockSpec((1, t, FOX_HEADS), lambda i, p, j: (i, 0, 0)),
                       pl.BlockSpec((1, FOX_HEADS, tq), lambda i, p, j: (i, 0, j))]
    else:
        extra_specs = [pl.BlockSpec(extra_a.shape, lambda i, p, j: (0, 0)),
                       pl.BlockSpec(extra_b.shape, lambda i, p, j: (0, 0))]
    return pl.pallas_call(
        kernel,
        out_shape=jax.ShapeDtypeStruct((b, t, w), F32),
        grid=(b, groups, t // tq),
        in_specs=[
            pl.BlockSpec((1, tq, LANES), lambda i, p, j: (i, j, p)),
            pl.BlockSpec((1, t, LANES), lambda i, p, j: (i, 0, p)),
            pl.BlockSpec((1, t, LANES), lambda i, p, j: (i, 0, groups + p)),
        ] + extra_specs,
        out_specs=pl.BlockSpec((1, tq, LANES), lambda i, p, j: (i, j, p)),
        scratch_shapes=[pltpu.VMEM((t, 2 * LANES if fox else LANES), BF16), pltpu.VMEM((_V_ROWS, t), BF16),
                        pltpu.VMEM((1, 2 * tq), F32), pltpu.VMEM((_V_ROWS, 2 * tq), F32)],
        compiler_params=_params("parallel", "parallel", "arbitrary"),
        name="fox_prompt" if fox else "diff_prompt",
    )(q, kv, kv, extra_a, extra_b)


_PAGES_PER_STEP = 8


def _decode_kernel(pt_ref, qf_ref, qd_ref, nkf_ref, nkd_ref, nlf_ref, lam_ref, g_ref, sfx_ref, *rest,
                   pps, lam_init):
    del pt_ref
    page_refs = rest[:3 * pps]
    of_ref, od_ref = rest[3 * pps:3 * pps + 2]
    mf_scr, lf_scr, af_scr, md_scr, ld_scr, ad_scr, carry_scr = rest[3 * pps + 2:]
    b = pl.program_id(0)
    j = pl.program_id(1)
    scale = FOX_DH ** -0.5
    dj = 2 * DIFF_HEADS
    hd = 2 * DIFF_DH

    q_row = qf_ref[0]
    rf = lax.broadcasted_iota(jnp.int32, (FOX_HEADS, FOX_W), 0)
    lf_ = lax.broadcasted_iota(jnp.int32, (FOX_HEADS, FOX_W), 1)
    own = (lf_ // FOX_DH) == rf
    qblk = jnp.where(own, jnp.broadcast_to(q_row, (FOX_HEADS, FOX_W)), 0.0)
    qblk_b = qblk.astype(BF16)

    qd_row = qd_ref[0]
    r8 = lax.broadcasted_iota(jnp.int32, (dj, hd), 0)
    l8 = lax.broadcasted_iota(jnp.int32, (dj, hd), 1)
    head_of_row = r8 % DIFF_HEADS
    in_map = (l8 // DIFF_DH) == (r8 // DIFF_HEADS)

    def rows_from(vec, offset):
        out = jnp.zeros((dj, hd), F32)
        for h in range(DIFF_HEADS):
            piece = jnp.broadcast_to(vec[:, offset + h * hd:offset + (h + 1) * hd], (dj, hd))
            out = jnp.where(head_of_row == h, piece, out)
        return out

    q8 = jnp.where(in_map, rows_from(qd_row, 0), 0.0)
    q8_h = [jnp.where(head_of_row == h, q8, 0.0).astype(BF16) for h in range(DIFF_HEADS)]

    @pl.when(j == 0)
    def _():
        kvn = nkf_ref[0]
        kn = jnp.broadcast_to(kvn[:, 0:FOX_W], (FOX_HEADS, FOX_W))
        mf_scr[...] = jnp.broadcast_to(jnp.sum(qblk * kn, axis=1, keepdims=True) * scale, mf_scr.shape)
        lf_scr[...] = jnp.ones_like(lf_scr)
        af_scr[...] = jnp.broadcast_to(kvn[:, FOX_W:], (FOX_HEADS, FOX_W))
        kvd = nkd_ref[0]
        md_scr[...] = jnp.broadcast_to(
            jnp.sum(q8 * rows_from(kvd, 0), axis=1, keepdims=True) * scale, md_scr.shape)
        ld_scr[...] = jnp.ones_like(ld_scr)
        ad_scr[...] = rows_from(kvd, DIFF_W)
        nl = nlf_ref[...]
        sl = lax.broadcasted_iota(jnp.int32, nl.shape, 1)
        carry_scr[...] = jnp.broadcast_to(
            jnp.sum(jnp.where(sl == b, nl, 0.0), axis=1, keepdims=True), carry_scr.shape)

    carry = carry_scr[:, 0:1]
    s_f = []
    for i in range(pps):
        kt = page_refs[3 * i][0, 0].astype(BF16)
        lt = page_refs[3 * i + 1][0]
        hi, mid, lo = _split3(lt)
        r3 = _dot(jnp.concatenate([hi, mid, lo], axis=0), sfx_ref[...])
        bias = carry + r3[0:8] + r3[8:16] + r3[16:24]
        s_f.append(_dot(qblk_b, kt) * scale + bias)
        carry = carry + jnp.sum(lt, axis=1, keepdims=True)
    carry_scr[...] = jnp.broadcast_to(carry, carry_scr.shape)

    page = page_refs[2].shape[1] // dj
    s_d = []
    for i in range(pps):
        xd = page_refs[3 * i + 2]
        s = jnp.zeros((dj, page), F32)
        for h in range(DIFF_HEADS):
            k_h = xd[0, pl.ds(h, page, stride=dj), :].astype(BF16)
            s = s + _dot_nt(q8_h[h], k_h)
        s_d.append(s * scale)

    m_old = mf_scr[:, 0:1]
    m_new = jnp.maximum(m_old, jnp.max(functools.reduce(jnp.maximum, s_f), axis=1, keepdims=True))
    alpha = jnp.exp(m_old - m_new)
    l_add = jnp.zeros((FOX_HEADS, 1), F32)
    pv = jnp.zeros((FOX_HEADS, FOX_W), F32)
    for i in range(pps):
        pr = jnp.exp(s_f[i] - m_new)
        l_add = l_add + jnp.sum(pr, axis=1, keepdims=True)
        vt = page_refs[3 * i][0, 1].astype(BF16)
        pv = pv + _dot_nt(pr.astype(BF16), vt)
    lf_scr[...] = alpha * lf_scr[...] + l_add
    af_scr[...] = alpha * af_scr[...] + pv
    mf_scr[...] = jnp.broadcast_to(m_new, mf_scr.shape)

    md_old = md_scr[:, 0:1]
    md_new = jnp.maximum(md_old, jnp.max(functools.reduce(jnp.maximum, s_d), axis=1, keepdims=True))
    alphad = jnp.exp(md_old - md_new)
    ld_add = jnp.zeros((dj, 1), F32)
    pvd = jnp.zeros((dj, hd), F32)
    for i in range(pps):
        xd = page_refs[3 * i + 2]
        pr = jnp.exp(s_d[i] - md_new)
        ld_add = ld_add + jnp.sum(pr, axis=1, keepdims=True)
        for h in range(DIFF_HEADS):
            v_h = xd[0, pl.ds(DIFF_HEADS + h, page, stride=dj), :].astype(BF16)
            pvd = pvd + _dot(jnp.where(head_of_row == h, pr, 0.0).astype(BF16), v_h)
    ld_scr[...] = alphad * ld_scr[...] + ld_add
    ad_scr[...] = alphad * ad_scr[...] + pvd
    md_scr[...] = jnp.broadcast_to(md_new, md_scr.shape)

    @pl.when(j == pl.num_programs(1) - 1)
    def _():
        o_all = af_scr[...] / lf_scr[:, 0:1]
        of_ref[0] = jnp.sum(jnp.where(own, o_all, 0.0), axis=0, keepdims=True)
        od_all = ad_scr[...] / ld_scr[:, 0:1]
        lp = lam_ref[...]
        lam = (jnp.exp(jnp.sum(lp[0:1] * lp[1:2], axis=1, keepdims=True))
               - jnp.exp(jnp.sum(lp[2:3] * lp[3:4], axis=1, keepdims=True)) + lam_init)
        o = od_all[0:DIFF_HEADS] - lam * od_all[DIFF_HEADS:]
        od_ref[0] = _rms(o, g_ref[...]) * (1.0 - lam_init)


def _decode(page_table, qf, qd, new_kvf, new_kvd, new_lft, lam_p, subln_g, cache_ft, cache_lt, cache_d, lam_init):
    nb, n_pages = page_table.shape
    page = cache_lt.shape[2]
    pps = math.gcd(_PAGES_PER_STEP, n_pages)
    dj, hd = 2 * DIFF_HEADS, 2 * DIFF_DH
    kk = lax.broadcasted_iota(jnp.int32, (page, page), 0)
    kc = lax.broadcasted_iota(jnp.int32, (page, page), 1)
    sfx = jnp.where(kk > kc, 1.0, 0.0).astype(BF16)

    def page_idx(i):
        return lambda s, j, pt: pt[s, n_pages - 1 - (j * pps + i)]

    page_specs = []
    page_args = []
    for i in range(pps):
        pick = page_idx(i)
        page_specs += [
            pl.BlockSpec((1, 2, FOX_W, page), lambda s, j, pt, pick=pick: (pick(s, j, pt), 0, 0, 0)),
            pl.BlockSpec((1, FOX_HEADS, page), lambda s, j, pt, pick=pick: (pick(s, j, pt), 0, 0)),
            pl.BlockSpec((1, page * dj, hd), lambda s, j, pt, pick=pick: (pick(s, j, pt), 0, 0)),
        ]
        page_args += [cache_ft, cache_lt, cache_d]
    samp = lambda s, j, pt: (s, 0, 0)
    const = lambda s, j, pt: (0, 0)
    grid_spec = pltpu.PrefetchScalarGridSpec(
        num_scalar_prefetch=1,
        grid=(nb, n_pages // pps),
        in_specs=[
            pl.BlockSpec((1, 1, FOX_W), samp),
            pl.BlockSpec((1, 1, DIFF_W), samp),
            pl.BlockSpec((1, 1, 2 * FOX_W), samp),
            pl.BlockSpec((1, 1, 2 * DIFF_W), samp),
            pl.BlockSpec(new_lft.shape, const),
            pl.BlockSpec(lam_p.shape, const),
            pl.BlockSpec(subln_g.shape, const),
            pl.BlockSpec(sfx.shape, const),
        ] + page_specs,
        out_specs=[
            pl.BlockSpec((1, 1, FOX_W), samp),
            pl.BlockSpec((1, DIFF_HEADS, hd), samp),
        ],
        scratch_shapes=[
            pltpu.VMEM((FOX_HEADS, LANES), F32), pltpu.VMEM((FOX_HEADS, LANES), F32),
            pltpu.VMEM((FOX_HEADS, FOX_W), F32),
            pltpu.VMEM((dj, LANES), F32), pltpu.VMEM((dj, LANES), F32),
            pltpu.VMEM((dj, hd), F32),
            pltpu.VMEM((FOX_HEADS, LANES), F32),
        ],
    )
    return pl.pallas_call(
        functools.partial(_decode_kernel, pps=pps, lam_init=lam_init),
        out_shape=[jax.ShapeDtypeStruct((nb, 1, FOX_W), F32),
                   jax.ShapeDtypeStruct((nb, DIFF_HEADS, hd), F32)],
        grid_spec=grid_spec,
        compiler_params=_params("parallel", "arbitrary"),
        name="decode_attn",
    )(page_table, qf, qd, new_kvf, new_kvd, new_lft, lam_p, subln_g, sfx, *page_args)


def _rec_proj_kernel(x_ref, g_ref, w_ref, wgt_ref, bg_ref, q_ref, k_ref, v_ref, og_ref, u_ref, gt_ref):
    h = _rms(x_ref[0], g_ref[...]).astype(BF16)
    c0 = 0
    q_ref[0] = _dot(h, w_ref[:, c0:c0 + MLSTM_QK]) * (MLSTM_DK ** -0.5)
    c0 += MLSTM_QK
    k_ref[0] = _dot(h, w_ref[:, c0:c0 + MLSTM_QK])
    c0 += MLSTM_QK
    v_ref[0] = _dot(h, w_ref[:, c0:c0 + MLSTM_V])
    c0 += MLSTM_V
    og_ref[0] = _dot(h, w_ref[:, c0:c0 + MLSTM_V])
    c0 += MLSTM_V
    ua = _dot(h, w_ref[:, c0:c0 + CONV_CH])
    c0 += CONV_CH
    ub = _dot(h, w_ref[:, c0:c0 + CONV_CH])
    u_ref[0] = ua * jax.nn.sigmoid(ub)
    z = _dot_nt(wgt_ref[...], h) + bg_ref[...]
    rowi = lax.broadcasted_iota(jnp.int32, z.shape, 0)
    gt_ref[0] = jnp.where(rowi < MLSTM_HEADS, z, _log_sigmoid(z))


def _rec_proj(x, g, w_main, wgt, bg):
    b, t, d = x.shape
    tm = min(512, t)
    row = lambda i, j: (i, j, 0)
    const = lambda i, j: (0, 0)
    widths = [MLSTM_QK, MLSTM_QK, MLSTM_V, MLSTM_V, CONV_CH]
    return pl.pallas_call(
        _rec_proj_kernel,
        out_shape=[jax.ShapeDtypeStruct((b, t, w), F32) for w in widths]
        + [jax.ShapeDtypeStruct((b, 2 * MLSTM_HEADS, t), F32)],
        grid=(b, t // tm),
        in_specs=[
            pl.BlockSpec((1, tm, d), row),
            pl.BlockSpec((1, d), const),
            pl.BlockSpec(w_main.shape, const),
            pl.BlockSpec(wgt.shape, const),
            pl.BlockSpec(bg.shape, const),
        ],
        out_specs=[pl.BlockSpec((1, tm, w), row) for w in widths]
        + [pl.BlockSpec((1, 2 * MLSTM_HEADS, tm), lambda i, j: (i, 0, j))],
        compiler_params=_params("parallel", "parallel"),
        name="rec_proj",
    )(x, g.reshape(1, d), w_main, wgt, bg)


def _mlstm_prompt_kernel(q_ref, k_ref, v_ref, og_ref, gt_ref, gc_ref, ng_ref,
                         hm_ref, c_ref, n_ref, m_ref, ct_scr, n_scr, m_scr, *, chunk):
    t = pl.program_id(1)
    L = chunk
    half = LANES // 2

    @pl.when(t == 0)
    def _():
        ct_scr[...] = jnp.zeros_like(ct_scr)
        n_scr[...] = jnp.zeros_like(n_scr)
        m_scr[...] = jnp.full(m_scr.shape, NEG, F32)

    gt = gt_ref[0]
    gc = gc_ref[0]
    cum_rows = _dot3_left(gt, _tri(L, upper=True))
    cum_cols = _dot3_right(_tri(L, upper=False), gc)
    rr = lax.broadcasted_iota(jnp.int32, (L, L), 0)
    cc = lax.broadcasted_iota(jnp.int32, (L, L), 1)
    tri = cc <= rr
    lane = lax.broadcasted_iota(jnp.int32, (L, LANES), 1)

    heads = []
    for h in range(MLSTM_HEADS):
        pair, c = divmod(h, 2)
        qp = q_ref[0, :, pair * LANES:(pair + 1) * LANES]
        kp = k_ref[0, :, pair * LANES:(pair + 1) * LANES]
        mine = (lane < half) if c == 0 else (lane >= half)
        qm = jnp.where(mine, qp, 0.0)
        qm_b = qm.astype(BF16)
        qk = _dot_nt(qm_b, kp.astype(BF16))
        qc = _dot(qm_b, ct_scr[pair].astype(BF16))
        heads.append(dict(pair=pair, c=c, kp=kp, qm=qm, qk=qk, qc=qc))

    for h, hd_ in enumerate(heads):
        a_col = cum_cols[:, MLSTM_HEADS + h:MLSTM_HEADS + h + 1]
        ig_col = gc[:, h:h + 1]
        b_row = gt[h:h + 1, :] - cum_rows[MLSTM_HEADS + h:MLSTM_HEADS + h + 1, :]
        m_prev = m_scr[h:h + 1, 0:1]
        d = jnp.where(tri, a_col + b_row, NEG)
        m_inter = a_col + m_prev
        m_t = jnp.maximum(jnp.max(d, axis=1, keepdims=True), m_inter)
        s = hd_["qk"] * jnp.exp(d - m_t)
        w_inter = jnp.exp(m_inter - m_t)
        m_new = m_t[L - 1:L, :]
        cum_last = a_col[L - 1:L, :]
        decay = jnp.exp(cum_last + m_prev - m_new)
        w_key = jnp.exp(cum_last - a_col + ig_col - m_new)
        vh = v_ref[0, :, h * MLSTM_DV:(h + 1) * MLSTM_DV]
        hd_.update(s=s, w_inter=w_inter, m_t=m_t, m_new=m_new, decay=decay, w_key=w_key, vh=vh)

    for hd_ in heads:
        hd_["sv"] = _dot(hd_["s"].astype(BF16), hd_["vh"].astype(BF16))
        kt_b = hd_["kp"].T.astype(BF16)
        hd_["upd"] = _dot(kt_b, (hd_["w_key"] * hd_["vh"]).astype(BF16))

    lane1 = lax.broadcasted_iota(jnp.int32, (1, LANES), 1)
    for pair in range(MLSTM_HEADS // 2):
        ct = ct_scr[pair]
        n_row = n_scr[pair]
        ct_new = []
        n_new = []
        for c in range(2):
            h = 2 * pair + c
            hd_ = heads[h]
            num = hd_["sv"] + hd_["w_inter"] * hd_["qc"]
            den = (jnp.sum(hd_["s"], axis=1, keepdims=True)
                   + hd_["w_inter"] * jnp.sum(hd_["qm"] * n_row, axis=1, keepdims=True))
            hout = num / jnp.maximum(jnp.abs(den), jnp.exp(-hd_["m_t"]))
            gh = ng_ref[:, h * MLSTM_DV:(h + 1) * MLSTM_DV]
            oh = og_ref[0, :, h * MLSTM_DV:(h + 1) * MLSTM_DV]
            hm_ref[0, :, h * MLSTM_DV:(h + 1) * MLSTM_DV] = _rms(hout, gh) * jax.nn.sigmoid(oh)
            lo, hi_ = c * half, (c + 1) * half
            ct_new.append(hd_["decay"] * ct[lo:hi_] + hd_["upd"][lo:hi_])
            n_new.append(hd_["decay"] * n_row + jnp.sum(hd_["w_key"] * hd_["kp"], axis=0, keepdims=True))
            m_scr[h:h + 1, :] = jnp.broadcast_to(hd_["m_new"], (1, LANES))
        ct_scr[pair] = jnp.concatenate(ct_new, axis=0)
        n_scr[pair] = jnp.where(lane1 < half, n_new[0], n_new[1])

    @pl.when(t == pl.num_programs(1) - 1)
    def _():
        for pair in range(MLSTM_HEADS // 2):
            c_pair = ct_scr[pair].T
            n_row = n_scr[pair]
            for c in range(2):
                h = 2 * pair + c
                c_ref[0, h] = c_pair[:, c * half:(c + 1) * half]
                n_ref[0, h:h + 1, :] = n_row[:, c * half:(c + 1) * half]
                m_ref[0, :, h:h + 1] = m_scr[h:h + 1, 0:1]


def _mlstm_prompt(q, k, v, og, gt, gc, norm_g):
    b, t, _ = q.shape
    chunk = min(256, t)
    row = lambda i, j: (i, j, 0)
    const = lambda i, j: (0, 0)
    return pl.pallas_call(
        functools.partial(_mlstm_prompt_kernel, chunk=chunk),
        out_shape=[
            jax.ShapeDtypeStruct((b, t, MLSTM_V), F32),
            jax.ShapeDtypeStruct((b, MLSTM_HEADS, MLSTM_DV, MLSTM_DK), F32),
            jax.ShapeDtypeStruct((b, MLSTM_HEADS, MLSTM_DK), F32),
            jax.ShapeDtypeStruct((b, 1, MLSTM_HEADS), F32),
        ],
        grid=(b, t // chunk),
        in_specs=[
            pl.BlockSpec((1, chunk, MLSTM_QK), row),
            pl.BlockSpec((1, chunk, MLSTM_QK), row),
            pl.BlockSpec((1, chunk, MLSTM_V), row),
            pl.BlockSpec((1, chunk, MLSTM_V), row),
            pl.BlockSpec((1, 2 * MLSTM_HEADS, chunk), lambda i, j: (i, 0, j)),
            pl.BlockSpec((1, chunk, 2 * MLSTM_HEADS), row),
            pl.BlockSpec((1, MLSTM_V), const),
        ],
        out_specs=[
            pl.BlockSpec((1, chunk, MLSTM_V), row),
            pl.BlockSpec((1, MLSTM_HEADS, MLSTM_DV, MLSTM_DK), lambda i, j: (i, 0, 0, 0)),
            pl.BlockSpec((1, MLSTM_HEADS, MLSTM_DK), lambda i, j: (i, 0, 0)),
            pl.BlockSpec((1, 1, MLSTM_HEADS), lambda i, j: (i, 0, 0)),
        ],
        scratch_shapes=[
            pltpu.VMEM((MLSTM_HEADS // 2, LANES, MLSTM_DV), F32),
            pltpu.VMEM((MLSTM_HEADS // 2, 1, LANES), F32),
            pltpu.VMEM((SUBLANES, LANES), F32),
        ],
        compiler_params=_params("parallel", "arbitrary"),
        name="mlstm_prompt",
    )(q, k, v, og, gt, gc, norm_g.reshape(1, MLSTM_V))


def _mlstm_sample_kernel(q_ref, k_ref, v_ref, og_ref, gc_ref, m_ref, kall_ref, vall_ref, ng_ref, c_ref, n_ref,
                         hm_ref, co_ref, no_ref, mo_ref, kt_scr, *, bs):
    i = pl.program_id(0)
    nb = kall_ref.shape[0]
    half = LANES // 2

    @pl.when(i == 0)
    def _():
        for pair in range(MLSTM_HEADS // 2):
            kt_scr[pair] = kall_ref[:, pair * LANES:(pair + 1) * LANES].T

    rows = lax.broadcasted_iota(jnp.int32, (nb, MLSTM_DV), 0)

    def body(bl, _):
        b = i * bs + bl
        q_row = q_ref[bl]
        k_row = k_ref[bl]
        v_row = v_ref[bl]
        og_row = og_ref[bl]
        g_row = gc_ref[bl]
        m_row = m_ref[bl]
        cms, cqs, outers = [], [], []
        for h in range(MLSTM_HEADS):
            qh = q_row[:, h * MLSTM_DK:(h + 1) * MLSTM_DK]
            cm = c_ref[bl, h]
            v_all = vall_ref[:, h * MLSTM_DV:(h + 1) * MLSTM_DV]
            v_sel = jnp.where(rows == b, v_all, 0.0).astype(BF16)
            pair, c = divmod(h, 2)
            kt_h = kt_scr[pair, c * half:(c + 1) * half, :]
            cms.append(cm)
            cqs.append(_dot(qh.astype(BF16), cm.astype(BF16)))
            outers.append(_dot(kt_h.astype(BF16), v_sel))
        for h in range(MLSTM_HEADS):
            qh = q_row[:, h * MLSTM_DK:(h + 1) * MLSTM_DK]
            kh = k_row[:, h * MLSTM_DK:(h + 1) * MLSTM_DK]
            vh = v_row[:, h * MLSTM_DV:(h + 1) * MLSTM_DV]
            it = g_row[:, h:h + 1]
            lf = g_row[:, MLSTM_HEADS + h:MLSTM_HEADS + h + 1]
            m_prev = m_row[:, h:h + 1]
            m_inter = lf + m_prev
            m_t = jnp.maximum(it, m_inter)
            e_i = jnp.exp(it - m_t)
            w_inter = jnp.exp(m_inter - m_t)
            s = jnp.sum(qh * kh, axis=1, keepdims=True) * e_i
            nh = n_ref[bl, h:h + 1, :]
            num = s * vh + w_inter * cqs[h]
            den = s + w_inter * jnp.sum(nh * qh, axis=1, keepdims=True)
            hrow = num / jnp.maximum(jnp.abs(den), jnp.exp(-m_t))
            gh = ng_ref[:, h * MLSTM_DV:(h + 1) * MLSTM_DV]
            oh = og_row[:, h * MLSTM_DV:(h + 1) * MLSTM_DV]
            hm_ref[bl, :, h * MLSTM_DV:(h + 1) * MLSTM_DV] = _rms(hrow, gh) * jax.nn.sigmoid(oh)
            co_ref[bl, h] = w_inter * cms[h] + e_i * outers[h]
            no_ref[bl, h:h + 1, :] = w_inter * nh + e_i * kh
            mo_ref[bl, :, h:h + 1] = m_t
        return 0

    lax.fori_loop(0, bs, body, 0)


def _mlstm_sample(q, k, v, og, gc, norm_g, c0_t, n0, m0):
    nb = q.shape[0]
    bs = min(16, nb)
    full = lambda i: (0, 0)
    rows = lambda i: (i, 0, 0)
    per_sample = [a.reshape(nb, 1, a.shape[1]) for a in (q, k, v, og, gc, m0)]
    hm, c1, n1, m1 = pl.pallas_call(
        functools.partial(_mlstm_sample_kernel, bs=bs),
        out_shape=[
            jax.ShapeDtypeStruct((nb, 1, MLSTM_V), F32),
            jax.ShapeDtypeStruct(c0_t.shape, F32),
            jax.ShapeDtypeStruct(n0.shape, F32),
            jax.ShapeDtypeStruct((nb, 1, MLSTM_HEADS), F32),
        ],
        grid=(nb // bs,),
        in_specs=[pl.BlockSpec((bs, 1, a.shape[2]), rows) for a in per_sample] + [
            pl.BlockSpec(k.shape, full),
            pl.BlockSpec(v.shape, full),
            pl.BlockSpec((1, MLSTM_V), full),
            pl.BlockSpec((bs, MLSTM_HEADS, MLSTM_DK, MLSTM_DV), lambda i: (i, 0, 0, 0)),
            pl.BlockSpec((bs, MLSTM_HEADS, MLSTM_DK), rows),
        ],
        out_specs=[
            pl.BlockSpec((bs, 1, MLSTM_V), rows),
            pl.BlockSpec((bs, MLSTM_HEADS, MLSTM_DK, MLSTM_DV), lambda i: (i, 0, 0, 0)),
            pl.BlockSpec((bs, MLSTM_HEADS, MLSTM_DK), rows),
            pl.BlockSpec((bs, 1, MLSTM_HEADS), rows),
        ],
        scratch_shapes=[pltpu.VMEM((MLSTM_HEADS // 2, LANES, nb), F32)],
        compiler_params=_params("arbitrary"),
        name="mlstm_sample",
    )(*per_sample, k, v, norm_g.reshape(1, MLSTM_V), c0_t, n0)
    return hm.reshape(nb, MLSTM_V), c1, n1, m1.reshape(nb, MLSTM_HEADS)


_CONV_HALO = 32
_CONV_ROWS = 32


def _ln_silu(y, g, b):
    yc = y - jnp.mean(y, axis=-1, keepdims=True)
    var = jnp.mean(yc * yc, axis=-1, keepdims=True)
    z = yc * lax.rsqrt(var + NORM_EPS) * g + b
    return z * jax.nn.sigmoid(z)


def _conv_prompt_kernel(u_ref, prev_ref, w_ref, b_ref, g_ref, be_ref, c_ref, st_ref, full_scr, shift_scr, *, tt):
    t = pl.program_id(1)
    pad = _CONV_HALO - (CONV_W - 1)

    @pl.when(t == 0)
    def _():
        full_scr[0:_CONV_HALO, :] = jnp.zeros((_CONV_HALO, CONV_CH), F32)
        full_scr[pad:_CONV_HALO, :] = prev_ref[0]

    full_scr[_CONV_HALO:, :] = u_ref[0]
    span = tt + _CONV_HALO - SUBLANES
    for r in range(1, SUBLANES):
        shift_scr[r, 0:span, :] = full_scr[r:r + span, :]
    for r0 in range(0, tt, _CONV_ROWS):
        acc = jnp.broadcast_to(b_ref[...], (_CONV_ROWS, CONV_CH))
        for k in range(CONV_W):
            off = r0 + pad + k
            r = off % SUBLANES
            if r == 0:
                rows = full_scr[off:off + _CONV_ROWS, :]
            else:
                rows = shift_scr[r, off - r:off - r + _CONV_ROWS, :]
            acc = acc + w_ref[k:k + 1, :] * rows
        c_ref[0, r0:r0 + _CONV_ROWS, :] = _ln_silu(acc, g_ref[...], be_ref[...])

    @pl.when(t == pl.num_programs(1) - 1)
    def _():
        st_ref[0] = full_scr[_CONV_HALO + tt - (CONV_W - 1):, :]

    full_scr[0:_CONV_HALO, :] = full_scr[tt:tt + _CONV_HALO, :]


def _conv_prompt(u, prev, w, bias, ln_g, ln_b):
    b, t, ch = u.shape
    tt = min(256, t)
    const = lambda i, j: (0, 0)
    return pl.pallas_call(
        functools.partial(_conv_prompt_kernel, tt=tt),
        out_shape=[jax.ShapeDtypeStruct((b, t, ch), F32),
                   jax.ShapeDtypeStruct((b, CONV_W - 1, ch), F32)],
        grid=(b, t // tt),
        in_specs=[
            pl.BlockSpec((1, tt, ch), lambda i, j: (i, j, 0)),
            pl.BlockSpec((1, CONV_W - 1, ch), lambda i, j: (i, 0, 0)),
            pl.BlockSpec((CONV_W, ch), const),
            pl.BlockSpec((1, ch), const),
            pl.BlockSpec((1, ch), const),
            pl.BlockSpec((1, ch), const),
        ],
        out_specs=[pl.BlockSpec((1, tt, ch), lambda i, j: (i, j, 0)),
                   pl.BlockSpec((1, CONV_W - 1, ch), lambda i, j: (i, 0, 0))],
        scratch_shapes=[pltpu.VMEM((_CONV_HALO + tt, ch), F32),
                        pltpu.VMEM((SUBLANES, _CONV_HALO + tt, ch), F32)],
        compiler_params=_params("parallel", "arbitrary"),
        name="conv_prompt",
    )(u, prev, w, bias.reshape(1, ch), ln_g.reshape(1, ch), ln_b.reshape(1, ch))


def _conv_sample_kernel(u_ref, prev_ref, w_ref, b_ref, g_ref, be_ref, c_ref, st_ref):
    hist = CONV_W - 1
    u = u_ref[...]
    y = w_ref[hist:hist + 1, :] * u + b_ref[...]
    for k in range(hist):
        row = prev_ref[k]
        y = y + w_ref[k:k + 1, :] * row
        if k > 0:
            st_ref[k - 1] = row
    st_ref[hist - 1] = u
    c_ref[...] = _ln_silu(y, g_ref[...], be_ref[...])


def _conv_sample(u, prev_t, w, bias, ln_g, ln_b):
    nb, ch = u.shape
    bs = min(32, nb)
    hist = CONV_W - 1
    const = lambda i: (0, 0)
    return pl.pallas_call(
        _conv_sample_kernel,
        out_shape=[jax.ShapeDtypeStruct((nb, ch), F32),
                   jax.ShapeDtypeStruct((hist, nb, ch), F32)],
        grid=(nb // bs,),
        in_specs=[
            pl.BlockSpec((bs, ch), lambda i: (i, 0)),
            pl.BlockSpec((hist, bs, ch), lambda i: (0, i, 0)),
            pl.BlockSpec((CONV_W, ch), const),
            pl.BlockSpec((1, ch), const),
            pl.BlockSpec((1, ch), const),
            pl.BlockSpec((1, ch), const),
        ],
        out_specs=[pl.BlockSpec((bs, ch), lambda i: (i, 0)),
                   pl.BlockSpec((hist, bs, ch), lambda i: (0, i, 0))],
        compiler_params=_params("parallel"),
        name="conv_sample",
    )(u, prev_t, w, bias.reshape(1, ch), ln_g.reshape(1, ch), ln_b.reshape(1, ch))


def kernel(x_prompt, x_sample, cache_fox_kv, cache_fox_logf, cache_diff_kv, state_mlstm_C, state_mlstm_n,
           state_mlstm_m, state_conv, page_table, norm_g, final_g, ffn_w_in, ffn_w_out, attn_w_in, attn_b_f,
           diff_lam, diff_subln_g, attn_w_out, rec_w_in, rec_b_i, rec_b_f, mlstm_norm_g, conv_w, conv_b,
           conv_ln_g, conv_ln_b, rec_w_out):
    bp, t, d = x_prompt.shape
    nb = x_sample.shape[0]
    depth = norm_g.shape[0]
    n_pool, page = cache_fox_kv.shape[1], cache_fox_kv.shape[2]
    xp = x_prompt.reshape(bp * t, d)
    xs = x_sample.reshape(nb, d)
    w_in_b = ffn_w_in.astype(BF16)
    w_out_b = ffn_w_out.astype(BF16)

    outs = {k: [] for k in ("fkv_p", "fkv_s", "flf_p", "flf_s", "dkv_p", "dkv_s",
                            "c_p", "c_s", "n_p", "n_s", "m_p", "m_s", "cv_p", "cv_s")}
    for l in range(depth):
        j = l // 2
        xp = _ffn(xp, norm_g[l, 0], w_in_b, w_out_b, l, 0)
        xs = _ffn(xs, norm_g[l, 0], w_in_b, w_out_b, l, 0)
        if l % 2 == 0:
            lam_init = 0.8 - 0.6 * math.exp(-0.3 * l)
            w = attn_w_in[j]
            c1, c2 = 3 * FOX_W, 3 * FOX_W + FOX_HEADS
            w_main = jnp.concatenate([w[:, :c1], w[:, c2:]], axis=1).astype(BF16)
            wft = w[:, c1:c2].T.astype(BF16)
            bf = attn_b_f[j].reshape(FOX_HEADS, 1)
            w_o = attn_w_out[j].astype(BF16)
            qf, kvf, qd, kvd, lft, cumt, kvd_rows = _attn_proj(xp.reshape(bp, t, d), norm_g[l, 1], w_main, wft, bf)
            o_f = _attn_prompt(True, qf, kvf, jnp.swapaxes(cumt, 1, 2), cumt, lam_init)
            o_d = _attn_prompt(False, qd, kvd, diff_lam[j], diff_subln_g[j].reshape(1, -1), lam_init)
            xp = _merge(xp, o_f.reshape(bp * t, FOX_W), o_d.reshape(bp * t, DIFF_W), w_o)
            outs["fkv_p"].append(kvf.reshape(bp, t, 2, FOX_HEADS, FOX_DH))
            outs["flf_p"].append(jnp.swapaxes(lft, 1, 2))
            outs["dkv_p"].append(kvd_rows.reshape(bp, t, 2, DIFF_HEADS, 2 * DIFF_DH))
            sqf, skvf, sqd, skvd, slft, _, skvd_rows = _attn_proj(xs.reshape(1, nb, d), norm_g[l, 1], w_main, wft, bf)
            so_f, so_d = _decode(
                page_table,
                sqf.reshape(nb, 1, FOX_W), sqd.reshape(nb, 1, DIFF_W),
                skvf.reshape(nb, 1, 2 * FOX_W), skvd.reshape(nb, 1, 2 * DIFF_W),
                slft[0], diff_lam[j], diff_subln_g[j].reshape(1, -1),
                jnp.transpose(cache_fox_kv[j], (0, 2, 3, 4, 1)).reshape(n_pool, 2, FOX_W, page),
                jnp.swapaxes(cache_fox_logf[j], 1, 2),
                cache_diff_kv[j].reshape(n_pool, page * 2 * DIFF_HEADS, 2 * DIFF_DH),
                lam_init)
            xs = _merge(xs, so_f.reshape(nb, FOX_W), so_d.reshape(nb, DIFF_W), w_o)
            outs["fkv_s"].append(skvf.reshape(nb, 1, 2, FOX_HEADS, FOX_DH))
            outs["flf_s"].append(jnp.swapaxes(slft, 1, 2).reshape(nb, 1, FOX_HEADS))
            outs["dkv_s"].append(skvd_rows.reshape(nb, 1, 2, DIFF_HEADS, 2 * DIFF_DH))
        else:
            w = rec_w_in[j]
            c1 = 2 * MLSTM_QK + MLSTM_V
            c2 = c1 + 2 * MLSTM_HEADS
            w_main = jnp.concatenate([w[:, :c1], w[:, c2:]], axis=1).astype(BF16)
            wgt = w[:, c1:c2].T.astype(BF16)
            bg = jnp.concatenate([rec_b_i[j], rec_b_f[j]]).reshape(2 * MLSTM_HEADS, 1)
            w_o = rec_w_out[j].astype(BF16)
            q, k, v, og, u, gt = _rec_proj(xp.reshape(bp, t, d), norm_g[l, 1], w_main, wgt, bg)
            hm, c_p, n_p, m_p = _mlstm_prompt(q, k, v, og, gt, jnp.swapaxes(gt, 1, 2), mlstm_norm_g[j])
            cv, st_p = _conv_prompt(u, jnp.zeros((bp, CONV_W - 1, CONV_CH), F32), conv_w[j], conv_b[j],
                                    conv_ln_g[j], conv_ln_b[j])
            xp = _merge(xp, hm.reshape(bp * t, MLSTM_V), cv.reshape(bp * t, CONV_CH), w_o)
            outs["c_p"].append(c_p)
            outs["n_p"].append(n_p)
            outs["m_p"].append(m_p.reshape(bp, MLSTM_HEADS))
            outs["cv_p"].append(st_p)
            sq, sk, sv, sog, su, sgt = _rec_proj(xs.reshape(1, nb, d), norm_g[l, 1], w_main, wgt, bg)
            shm, c_s_t, n_s, m_s = _mlstm_sample(sq[0], sk[0], sv[0], sog[0], sgt[0].T, mlstm_norm_g[j],
                                                 jnp.swapaxes(state_mlstm_C[j], -1, -2),
                                                 state_mlstm_n[j], state_mlstm_m[j])
            scv, st_s = _conv_sample(su[0], jnp.swapaxes(state_conv[j], 0, 1), conv_w[j], conv_b[j],
                                     conv_ln_g[j], conv_ln_b[j])
            xs = _merge(xs, shm, scv, w_o)
            outs["c_s"].append(jnp.swapaxes(c_s_t, -1, -2))
            outs["n_s"].append(n_s)
            outs["m_s"].append(m_s)
            outs["cv_s"].append(jnp.swapaxes(st_s, 0, 1))
        last = l == depth - 1
        xp = _ffn(xp, norm_g[l, 2], w_in_b, w_out_b, l, 1, final_g if last else None)
        xs = _ffn(xs, norm_g[l, 2], w_in_b, w_out_b, l, 1, final_g if last else None)

    st = jnp.stack
    return (xp.reshape(bp, t, d), xs.reshape(nb, 1, d),
            st(outs["fkv_p"]), st(outs["fkv_s"]), st(outs["flf_p"]), st(outs["flf_s"]),
            st(outs["dkv_p"]), st(outs["dkv_s"]),
            st(outs["c_p"]), st(outs["c_s"]), st(outs["n_p"]), st(outs["n_s"]),
            st(outs["m_p"]), st(outs["m_s"]), st(outs["cv_p"]), st(outs["cv_s"]))
```

```python
import functools
import math

import jax
import jax.numpy as jnp
from jax import lax
from jax.experimental import pallas as pl
from jax.experimental.pallas import tpu as pltpu

F32 = jnp.float32
BF16 = jnp.bfloat16
NORM_EPS = 1e-6
NEG = -1e30
LOG2E = 1.4426950408889634

FOX_HEADS = 8
FOX_DH = 64
DIFF_HEADS = 4
DIFF_DH = 64
MLSTM_HEADS = 4
MLSTM_DK = 64
MLSTM_DV = 128
CONV_CH = 512
CONV_W = 31
FOX_W = FOX_HEADS * FOX_DH
DIFF_W = DIFF_HEADS * 2 * DIFF_DH
MLSTM_QK = MLSTM_HEADS * MLSTM_DK
MLSTM_V = MLSTM_HEADS * MLSTM_DV

LANES = 128
SUBLANES = 8
VMEM_LIMIT_BYTES = 56 * 1024 * 1024

_NT = (((1,), (1,)), ((), ()))


def _params(*sem):
    return pltpu.CompilerParams(dimension_semantics=sem, vmem_limit_bytes=VMEM_LIMIT_BYTES)


def _rms(x, g):
    return x * lax.rsqrt(jnp.mean(x * x, axis=-1, keepdims=True) + NORM_EPS) * g


def _log_sigmoid(z):
    return jnp.minimum(z, 0.0) - jnp.log(1.0 + jnp.exp(-jnp.abs(z)))


def _split3(x):
    hi = x.astype(BF16)
    r = x - hi.astype(F32)
    mid = r.astype(BF16)
    lo = (r - mid.astype(F32)).astype(BF16)
    return hi, mid, lo


def _dot(a, b):
    return jnp.dot(a, b, preferred_element_type=F32)


def _dot_nt(a, b):
    return lax.dot_general(a, b, _NT, preferred_element_type=F32)


def _dot3_left(x, t):
    hi, mid, lo = _split3(x)
    return _dot(hi, t) + _dot(mid, t) + _dot(lo, t)


def _dot3_right(t, x):
    hi, mid, lo = _split3(x)
    return _dot(t, hi) + _dot(t, mid) + _dot(t, lo)


def _tri(n, upper):
    r = lax.broadcasted_iota(jnp.int32, (n, n), 0)
    c = lax.broadcasted_iota(jnp.int32, (n, n), 1)
    keep = (r <= c) if upper else (r >= c)
    return jnp.where(keep, 1.0, 0.0).astype(BF16)


def _ffn_kernel(x_ref, g_ref, wa_ref, wb_ref, wo_ref, fg_ref, o_ref, h_scr, gated_scr, *, final_norm):
    j = pl.program_id(1)
    tf = wa_ref.shape[1]

    @pl.when(j == 0)
    def _():
        h_scr[...] = _rms(x_ref[...], g_ref[...]).astype(BF16)

    h = h_scr[...]
    a = _dot(h, wa_ref[...])
    b = _dot(h, wb_ref[...])
    gated_scr[:, pl.ds(pl.multiple_of(j * tf, tf), tf)] = (a * jax.nn.sigmoid(a) * b).astype(BF16)

    @pl.when(j == pl.num_programs(1) - 1)
    def _():
        y = x_ref[...] + 0.5 * _dot(gated_scr[...], wo_ref[...])
        if final_norm:
            y = _rms(y, fg_ref[...])
        o_ref[...] = y


def _ffn(x, g, w_in, w_out, layer, which, final_g=None):
    m, d = x.shape
    f = w_out.shape[2]
    tm = min(1024, m)
    tf = 256 if m > 256 else f // 2
    nf = f // tf
    assert m % tm == 0 and f % tf == 0 and tf % LANES == 0, (m, f)
    fg = jnp.ones((1, d), F32) if final_g is None else final_g.reshape(1, d)
    return pl.pallas_call(
        functools.partial(_ffn_kernel, final_norm=final_g is not None),
        out_shape=jax.ShapeDtypeStruct((m, d), F32),
        grid=(m // tm, nf),
        in_specs=[
            pl.BlockSpec((tm, d), lambda i, j: (i, 0)),
            pl.BlockSpec((1, d), lambda i, j: (0, 0)),
            pl.BlockSpec((None, None, d, tf), lambda i, j: (layer, which, 0, j)),
            pl.BlockSpec((None, None, d, tf), lambda i, j: (layer, which, 0, j + nf)),
            pl.BlockSpec((None, None, f, d), lambda i, j: (layer, which, 0, 0)),
            pl.BlockSpec((1, d), lambda i, j: (0, 0)),
        ],
        out_specs=pl.BlockSpec((tm, d), lambda i, j: (i, 0)),
        scratch_shapes=[pltpu.VMEM((tm, d), BF16), pltpu.VMEM((tm, f), BF16)],
        compiler_params=_params("parallel", "arbitrary"),
        name="ffn",
    )(x, g.reshape(1, d), w_in, w_in, w_out, fg)


def _merge_kernel(x_ref, oa_ref, ob_ref, w_ref, o_ref):
    wa = oa_ref.shape[-1]
    y = _dot(oa_ref[...].astype(BF16), w_ref[:wa, :])
    y += _dot(ob_ref[...].astype(BF16), w_ref[wa:, :])
    o_ref[...] = x_ref[...] + y


def _merge(x, oa, ob, w_out):
    m, d = x.shape
    tm = min(1024, m)
    wa, wb = oa.shape[1], ob.shape[1]
    return pl.pallas_call(
        _merge_kernel,
        out_shape=jax.ShapeDtypeStruct((m, d), F32),
        grid=(m // tm,),
        in_specs=[
            pl.BlockSpec((tm, d), lambda i: (i, 0)),
            pl.BlockSpec((tm, wa), lambda i: (i, 0)),
            pl.BlockSpec((tm, wb), lambda i: (i, 0)),
            pl.BlockSpec((wa + wb, d), lambda i: (0, 0)),
        ],
        out_specs=pl.BlockSpec((tm, d), lambda i: (i, 0)),
        compiler_params=_params("parallel"),
        name="merge",
    )(x, oa, ob, w_out)


def _attn_proj_kernel(x_ref, g_ref, w_ref, wft_ref, bf_ref, qf_ref, kvf_ref, qd_ref, kvd_ref,
                      lf_ref, cum_ref, kvdr_ref, carry_scr):
    t = pl.program_id(1)
    h = _rms(x_ref[0], g_ref[...]).astype(BF16)
    qf_ref[0] = _dot(h, w_ref[:, 0:FOX_W])
    kvf_ref[0] = _dot(h, w_ref[:, FOX_W:3 * FOX_W])
    qd_ref[0] = _dot(h, w_ref[:, 3 * FOX_W:3 * FOX_W + DIFF_W])
    kvd = _dot(h, w_ref[:, 3 * FOX_W + DIFF_W:])
    kvd_ref[0] = kvd
    groups = kvd.shape[1] // LANES
    for gi in range(groups):
        kvdr_ref[0, pl.ds(gi, kvd.shape[0], stride=groups), :] = kvd[:, gi * LANES:(gi + 1) * LANES]
    logf = _log_sigmoid(_dot_nt(wft_ref[...], h) + bf_ref[...])
    lf_ref[0] = logf

    @pl.when(t == 0)
    def _():
        carry_scr[...] = jnp.zeros_like(carry_scr)

    tm = logf.shape[1]
    cum = _dot3_left(logf, _tri(tm, upper=True)) + carry_scr[:, 0:1]
    cum_ref[0] = cum
    carry_scr[...] = jnp.broadcast_to(cum[:, tm - 1:tm], carry_scr.shape)


def _attn_proj(x, g, w_main, wft, bf):
    b, t, d = x.shape
    tm = min(512, t)
    n_main = w_main.shape[1]
    row = lambda i, j: (i, j, 0)
    col = lambda i, j: (i, 0, j)
    const = lambda i, j: (0, 0)
    return pl.pallas_call(
        _attn_proj_kernel,
        out_shape=[
            jax.ShapeDtypeStruct((b, t, FOX_W), F32),
            jax.ShapeDtypeStruct((b, t, 2 * FOX_W), F32),
            jax.ShapeDtypeStruct((b, t, DIFF_W), F32),
            jax.ShapeDtypeStruct((b, t, 2 * DIFF_W), F32),
            jax.ShapeDtypeStruct((b, FOX_HEADS, t), F32),
            jax.ShapeDtypeStruct((b, FOX_HEADS, t), F32),
            jax.ShapeDtypeStruct((b, t * (2 * DIFF_W // LANES), LANES), F32),
        ],
        grid=(b, t // tm),
        in_specs=[
            pl.BlockSpec((1, tm, d), row),
            pl.BlockSpec((1, d), const),
            pl.BlockSpec((d, n_main), const),
            pl.BlockSpec((FOX_HEADS, d), const),
            pl.BlockSpec((FOX_HEADS, 1), const),
        ],
        out_specs=[
            pl.BlockSpec((1, tm, FOX_W), row),
            pl.BlockSpec((1, tm, 2 * FOX_W), row),
            pl.BlockSpec((1, tm, DIFF_W), row),
            pl.BlockSpec((1, tm, 2 * DIFF_W), row),
            pl.BlockSpec((1, FOX_HEADS, tm), col),
            pl.BlockSpec((1, FOX_HEADS, tm), col),
            pl.BlockSpec((1, tm * (2 * DIFF_W // LANES), LANES), row),
        ],
        scratch_shapes=[pltpu.VMEM((FOX_HEADS, LANES), F32)],
        compiler_params=_params("parallel", "arbitrary"),
        name="attn_proj",
    )(x, g.reshape(1, d), w_main, wft, bf)


_V_ROWS = 144
_ATTN_GROUP = 4


def _attn_prompt_kernel(*refs, fox, tq, lam_init):
    if fox:
        q_ref, k_ref, v_ref, ccol_ref, crow_ref, o_ref, ka_scr, vt_scr, m_scr, acc_scr = refs
    else:
        q_ref, k_ref, v_ref, lam_ref, g_ref, o_ref, ka_scr, vt_scr, m_scr, acc_scr = refs
    p = pl.program_id(1)
    qi = pl.program_id(2)
    half = LANES // 2
    t_all = k_ref.shape[1]

    @pl.when(qi == 0)
    def _():
        ka_scr[:, 0:LANES] = k_ref[0].astype(BF16)
        if fox:
            cc = ccol_ref[0]
            hl = lax.broadcasted_iota(jnp.int32, cc.shape, 1)
            lane = lax.broadcasted_iota(jnp.int32, (t_all, LANES), 1)
            extra = jnp.zeros((t_all, LANES), F32)
            for c in range(2):
                fk = jnp.sum(jnp.where(hl == 2 * p + c, cc, 0.0), axis=1, keepdims=True) * LOG2E
                for i, piece in enumerate(_split3(fk)):
                    extra = jnp.where(lane == 3 * c + i, piece.astype(F32), extra)
            ka_scr[:, LANES:] = extra.astype(BF16)
        vt_scr[0:LANES, :] = v_ref[0].T.astype(BF16)
        r = lax.broadcasted_iota(jnp.int32, (_V_ROWS - LANES, t_all), 0)
        vt_scr[LANES:, :] = jnp.where(r == 0, 1.0, 0.0).astype(BF16)

    q = q_ref[0] * (FOX_DH ** -0.5 * LOG2E)
    lane = lax.broadcasted_iota(jnp.int32, q.shape, 1)
    members = []
    for c in range(2):
        x = jnp.where((lane < half) if c == 0 else (lane >= half), q, 0.0)
        if fox:
            pick = (lane >= 3 * c) & (lane < 3 * c + 3)
            x = jnp.concatenate([x, jnp.where(pick, -1.0, 0.0)], axis=1)
        members.append(x)
    qa = jnp.concatenate(members, axis=0).astype(BF16)
    if fox:
        fq = jnp.concatenate([crow_ref[0, pl.ds(2 * p + c, 1), :] for c in range(2)], axis=1) * LOG2E
    m_scr[...] = jnp.full(m_scr.shape, NEG, F32)
    acc_scr[...] = jnp.zeros_like(acc_scr)

    def scores(kb, masked):
        start = pl.multiple_of(kb * tq, tq)
        s = _dot_nt(ka_scr[pl.ds(start, tq), :], qa)
        if masked:
            krow = lax.broadcasted_iota(jnp.int32, (tq, tq), 0)
            qcol = lax.broadcasted_iota(jnp.int32, (tq, tq), 1)
            keep = krow <= qcol
            s = jnp.where(jnp.concatenate([keep, keep], axis=1), s, NEG)
        col_max = jnp.max(s, axis=0, keepdims=True)
        if fox:
            col_max = col_max + fq
        return start, s, col_max

    def accumulate(start, s, col_max):
        m_old = m_scr[...]
        m_new = jnp.maximum(m_old, col_max)
        shift = (m_new - fq) if fox else m_new
        pr = jnp.exp2(s - shift).astype(BF16)
        alpha = jnp.exp2(m_old - m_new)
        acc_scr[...] = alpha * acc_scr[...] + _dot(vt_scr[:, pl.ds(start, tq)], pr)
        m_scr[...] = m_new

    def group(kb0, n, last_masked):
        parts = [scores(kb0 + i, last_masked and i == n - 1) for i in range(n)]
        for part in parts:
            accumulate(*part)

    def body(i, carry):
        group(_ATTN_GROUP * i, _ATTN_GROUP, False)
        return carry

    lax.fori_loop(0, qi // _ATTN_GROUP, body, 0)
    rest = qi % _ATTN_GROUP
    for n_full in range(_ATTN_GROUP):
        @pl.when(rest == n_full)
        def _(n_full=n_full):
            group(qi - n_full, n_full + 1, True)

    acc = acc_scr[...]
    outs_t = [acc[0:LANES, c * tq:(c + 1) * tq] / acc[LANES:LANES + 1, c * tq:(c + 1) * tq] for c in range(2)]
    if fox:
        o_ref[0] = jnp.concatenate([outs_t[0][:half], outs_t[1][half:]], axis=0).T
    else:
        lp = lam_ref[...]
        lam = (jnp.exp(jnp.sum(lp[0:1] * lp[1:2], axis=1, keepdims=True))
               - jnp.exp(jnp.sum(lp[2:3] * lp[3:4], axis=1, keepdims=True)) + lam_init)
        o = (outs_t[0] - lam * outs_t[1]).T
        o_ref[0] = _rms(o, g_ref[...]) * (1.0 - lam_init)


def _attn_prompt(fox, q, kv, extra_a, extra_b, lam_init):
    b, t, w = q.shape
    groups = w // LANES
    tq = min(512, t)
    kernel = functools.partial(_attn_prompt_kernel, fox=fox, tq=tq, lam_init=lam_init)
    if fox:
        extra_specs = [pl.BlockSpec((1, t, FOX_HEADS), lambda i, p, j: (i, 0, 0)),
                       pl.BlockSpec((1, FOX_HEADS, tq), lambda i, p, j: (i, 0, j))]
    else:
        extra_specs = [pl.BlockSpec(extra_a.shape, lambda i, p, j: (0, 0)),
                       pl.BlockSpec(extra_b.shape, lambda i, p, j: (0, 0))]
    return pl.pallas_call(
        kernel,
        out_shape=jax.ShapeDtypeStruct((b, t, w), F32),
        grid=(b, groups, t // tq),
        in_specs=[
            pl.BlockSpec((1, tq, LANES), lambda i, p, j: (i, j, p)),
            pl.BlockSpec((1, t, LANES), lambda i, p, j: (i, 0, p)),
            pl.BlockSpec((1, t, LANES), lambda i, p, j: (i, 0, groups + p)),
        ] + extra_specs,
        out_specs=pl.BlockSpec((1, tq, LANES), lambda i, p, j: (i, j, p)),
        scratch_shapes=[pltpu.VMEM((t, 2 * LANES if fox else LANES), BF16), pltpu.VMEM((_V_ROWS, t), BF16),
                        pltpu.VMEM((1, 2 * tq), F32), pltpu.VMEM((_V_ROWS, 2 * tq), F32)],
        compiler_params=_params("parallel", "parallel", "arbitrary"),
        name="fox_prompt" if fox else "diff_prompt",
    )(q, kv, kv, extra_a, extra_b)


_PAGES_PER_STEP = 16


def _decode_kernel(pt_ref, qf_ref, qd_ref, nkf_ref, nkd_ref, nlf_ref, lam_ref, g_ref, sfx_ref, *rest,
                   pps, lam_init):
    del pt_ref
    page_refs = rest[:3 * pps]
    of_ref, od_ref = rest[3 * pps:3 * pps + 2]
    mf_scr, lf_scr, af_scr, md_scr, ld_scr, ad_scr, carry_scr = rest[3 * pps + 2:]
    b = pl.program_id(0)
    j = pl.program_id(1)
    scale = FOX_DH ** -0.5
    dj = 2 * DIFF_HEADS
    hd = 2 * DIFF_DH

    q_row = qf_ref[0]
    rf = lax.broadcasted_iota(jnp.int32, (FOX_HEADS, FOX_W), 0)
    lf_ = lax.broadcasted_iota(jnp.int32, (FOX_HEADS, FOX_W), 1)
    own = (lf_ // FOX_DH) == rf
    qblk = jnp.where(own, jnp.broadcast_to(q_row, (FOX_HEADS, FOX_W)), 0.0)
    qblk_b = qblk.astype(BF16)

    qd_row = qd_ref[0]
    r8 = lax.broadcasted_iota(jnp.int32, (dj, hd), 0)
    l8 = lax.broadcasted_iota(jnp.int32, (dj, hd), 1)
    head_of_row = r8 % DIFF_HEADS
    in_map = (l8 // DIFF_DH) == (r8 // DIFF_HEADS)

    def rows_from(vec, offset):
        out = jnp.zeros((dj, hd), F32)
        for h in range(DIFF_HEADS):
            piece = jnp.broadcast_to(vec[:, offset + h * hd:offset + (h + 1) * hd], (dj, hd))
            out = jnp.where(head_of_row == h, piece, out)
        return out

    q8 = jnp.where(in_map, rows_from(qd_row, 0), 0.0)
    q8_h = [jnp.where(head_of_row == h, q8, 0.0).astype(BF16) for h in range(DIFF_HEADS)]

    @pl.when(j == 0)
    def _():
        kvn = nkf_ref[0]
        kn = jnp.broadcast_to(kvn[:, 0:FOX_W], (FOX_HEADS, FOX_W))
        mf_scr[...] = jnp.broadcast_to(jnp.sum(qblk * kn, axis=1, keepdims=True) * scale, mf_scr.shape)
        lf_scr[...] = jnp.ones_like(lf_scr)
        af_scr[...] = jnp.broadcast_to(kvn[:, FOX_W:], (FOX_HEADS, FOX_W))
        kvd = nkd_ref[0]
        md_scr[...] = jnp.broadcast_to(
            jnp.sum(q8 * rows_from(kvd, 0), axis=1, keepdims=True) * scale, md_scr.shape)
        ld_scr[...] = jnp.ones_like(ld_scr)
        ad_scr[...] = rows_from(kvd, DIFF_W)
        nl = nlf_ref[...]
        sl = lax.broadcasted_iota(jnp.int32, nl.shape, 1)
        carry_scr[...] = jnp.broadcast_to(
            jnp.sum(jnp.where(sl == b, nl, 0.0), axis=1, keepdims=True), carry_scr.shape)

    carry = carry_scr[:, 0:1]
    s_f = []
    for i in range(pps):
        kt = page_refs[3 * i][0, 0].astype(BF16)
        lt = page_refs[3 * i + 1][0]
        hi, mid, lo = _split3(lt)
        r3 = _dot(jnp.concatenate([hi, mid, lo], axis=0), sfx_ref[...])
        bias = carry + r3[0:8] + r3[8:16] + r3[16:24]
        s_f.append(_dot(qblk_b, kt) * scale + bias)
        carry = carry + jnp.sum(lt, axis=1, keepdims=True)
    carry_scr[...] = jnp.broadcast_to(carry, carry_scr.shape)

    page = page_refs[2].shape[1] // dj
    s_d = []
    for i in range(pps):
        xd = page_refs[3 * i + 2]
        s = jnp.zeros((dj, page), F32)
        for h in range(DIFF_HEADS):
            k_h = xd[0, pl.ds(h, page, stride=dj), :].astype(BF16)
            s = s + _dot_nt(q8_h[h], k_h)
        s_d.append(s * scale)

    m_old = mf_scr[:, 0:1]
    m_new = jnp.maximum(m_old, jnp.max(functools.reduce(jnp.maximum, s_f), axis=1, keepdims=True))
    alpha = jnp.exp(m_old - m_new)
    l_add = jnp.zeros((FOX_HEADS, 1), F32)
    pv = jnp.zeros((FOX_HEADS, FOX_W), F32)
    for i in range(pps):
        pr = jnp.exp(s_f[i] - m_new)
        l_add = l_add + jnp.sum(pr, axis=1, keepdims=True)
        vt = page_refs[3 * i][0, 1].astype(BF16)
        pv = pv + _dot_nt(pr.astype(BF16), vt)
    lf_scr[...] = alpha * lf_scr[...] + l_add
    af_scr[...] = alpha * af_scr[...] + pv
    mf_scr[...] = jnp.broadcast_to(m_new, mf_scr.shape)

    md_old = md_scr[:, 0:1]
    md_new = jnp.maximum(md_old, jnp.max(functools.reduce(jnp.maximum, s_d), axis=1, keepdims=True))
    alphad = jnp.exp(md_old - md_new)
    ld_add = jnp.zeros((dj, 1), F32)
    pvd = jnp.zeros((dj, hd), F32)
    for i in range(pps):
        xd = page_refs[3 * i + 2]
        pr = jnp.exp(s_d[i] - md_new)
        ld_add = ld_add + jnp.sum(pr, axis=1, keepdims=True)
        for h in range(DIFF_HEADS):
            v_h = xd[0, pl.ds(DIFF_HEADS + h, page, stride=dj), :].astype(BF16)
            pvd = pvd + _dot(jnp.where(head_of_row == h, pr, 0.0).astype(BF16), v_h)
    ld_scr[...] = alphad * ld_scr[...] + ld_add
    ad_scr[...] = alphad * ad_scr[...] + pvd
    md_scr[...] = jnp.broadcast_to(md_new, md_scr.shape)

    @pl.when(j == pl.num_programs(1) - 1)
    def _():
        o_all = af_scr[...] / lf_scr[:, 0:1]
        of_ref[0] = jnp.sum(jnp.where(own, o_all, 0.0), axis=0, keepdims=True)
        od_all = ad_scr[...] / ld_scr[:, 0:1]
        lp = lam_ref[...]
        lam = (jnp.exp(jnp.sum(lp[0:1] * lp[1:2], axis=1, keepdims=True))
               - jnp.exp(jnp.sum(lp[2:3] * lp[3:4], axis=1, keepdims=True)) + lam_init)
        o = od_all[0:DIFF_HEADS] - lam * od_all[DIFF_HEADS:]
        od_ref[0] = _rms(o, g_ref[...]) * (1.0 - lam_init)


def _decode(page_table, qf, qd, new_kvf, new_kvd, new_lft, lam_p, subln_g, cache_ft, cache_lt, cache_d, lam_init):
    nb, n_pages = page_table.shape
    page = cache_lt.shape[2]
    pps = math.gcd(_PAGES_PER_STEP, n_pages)
    dj, hd = 2 * DIFF_HEADS, 2 * DIFF_DH
    kk = lax.broadcasted_iota(jnp.int32, (page, page), 0)
    kc = lax.broadcasted_iota(jnp.int32, (page, page), 1)
    sfx = jnp.where(kk > kc, 1.0, 0.0).astype(BF16)

    def page_idx(i):
        return lambda s, j, pt: pt[s, n_pages - 1 - (j * pps + i)]

    page_specs = []
    page_args = []
    for i in range(pps):
        pick = page_idx(i)
        page_specs += [
            pl.BlockSpec((1, 2, FOX_W, page), lambda s, j, pt, pick=pick: (pick(s, j, pt), 0, 0, 0)),
            pl.BlockSpec((1, FOX_HEADS, page), lambda s, j, pt, pick=pick: (pick(s, j, pt), 0, 0)),
            pl.BlockSpec((1, page * dj, hd), lambda s, j, pt, pick=pick: (pick(s, j, pt), 0, 0)),
        ]
        page_args += [cache_ft, cache_lt, cache_d]
    samp = lambda s, j, pt: (s, 0, 0)
    const = lambda s, j, pt: (0, 0)
    grid_spec = pltpu.PrefetchScalarGridSpec(
        num_scalar_prefetch=1,
        grid=(nb, n_pages // pps),
        in_specs=[
            pl.BlockSpec((1, 1, FOX_W), samp),
            pl.BlockSpec((1, 1, DIFF_W), samp),
            pl.BlockSpec((1, 1, 2 * FOX_W), samp),
            pl.BlockSpec((1, 1, 2 * DIFF_W), samp),
            pl.BlockSpec(new_lft.shape, const),
            pl.BlockSpec(lam_p.shape, const),
            pl.BlockSpec(subln_g.shape, const),
            pl.BlockSpec(sfx.shape, const),
        ] + page_specs,
        out_specs=[
            pl.BlockSpec((1, 1, FOX_W), samp),
            pl.BlockSpec((1, DIFF_HEADS, hd), samp),
        ],
        scratch_shapes=[
            pltpu.VMEM((FOX_HEADS, LANES), F32), pltpu.VMEM((FOX_HEADS, LANES), F32),
            pltpu.VMEM((FOX_HEADS, FOX_W), F32),
            pltpu.VMEM((dj, LANES), F32), pltpu.VMEM((dj, LANES), F32),
            pltpu.VMEM((dj, hd), F32),
            pltpu.VMEM((FOX_HEADS, LANES), F32),
        ],
    )
    return pl.pallas_call(
        functools.partial(_decode_kernel, pps=pps, lam_init=lam_init),
        out_shape=[jax.ShapeDtypeStruct((nb, 1, FOX_W), F32),
                   jax.ShapeDtypeStruct((nb, DIFF_HEADS, hd), F32)],
        grid_spec=grid_spec,
        compiler_params=_params("parallel", "arbitrary"),
        name="decode_attn",
    )(page_table, qf, qd, new_kvf, new_kvd, new_lft, lam_p, subln_g, sfx, *page_args)


def _rec_proj_kernel(x_ref, g_ref, w_ref, wgt_ref, bg_ref, q_ref, k_ref, v_ref, og_ref, u_ref, gt_ref):
    h = _rms(x_ref[0], g_ref[...]).astype(BF16)
    c0 = 0
    q_ref[0] = _dot(h, w_ref[:, c0:c0 + MLSTM_QK]) * (MLSTM_DK ** -0.5)
    c0 += MLSTM_QK
    k_ref[0] = _dot(h, w_ref[:, c0:c0 + MLSTM_QK])
    c0 += MLSTM_QK
    v_ref[0] = _dot(h, w_ref[:, c0:c0 + MLSTM_V])
    c0 += MLSTM_V
    og_ref[0] = _dot(h, w_ref[:, c0:c0 + MLSTM_V])
    c0 += MLSTM_V
    ua = _dot(h, w_ref[:, c0:c0 + CONV_CH])
    c0 += CONV_CH
    ub = _dot(h, w_ref[:, c0:c0 + CONV_CH])
    u_ref[0] = ua * jax.nn.sigmoid(ub)
    z = _dot_nt(wgt_ref[...], h) + bg_ref[...]
    rowi = lax.broadcasted_iota(jnp.int32, z.shape, 0)
    gt_ref[0] = jnp.where(rowi < MLSTM_HEADS, z, _log_sigmoid(z))


def _rec_proj(x, g, w_main, wgt, bg):
    b, t, d = x.shape
    tm = min(512, t)
    row = lambda i, j: (i, j, 0)
    const = lambda i, j: (0, 0)
    widths = [MLSTM_QK, MLSTM_QK, MLSTM_V, MLSTM_V, CONV_CH]
    return pl.pallas_call(
        _rec_proj_kernel,
        out_shape=[jax.ShapeDtypeStruct((b, t, w), F32) for w in widths]
        + [jax.ShapeDtypeStruct((b, 2 * MLSTM_HEADS, t), F32)],
        grid=(b, t // tm),
        in_specs=[
            pl.BlockSpec((1, tm, d), row),
            pl.BlockSpec((1, d), const),
            pl.BlockSpec(w_main.shape, const),
            pl.BlockSpec(wgt.shape, const),
            pl.BlockSpec(bg.shape, const),
        ],
        out_specs=[pl.BlockSpec((1, tm, w), row) for w in widths]
        + [pl.BlockSpec((1, 2 * MLSTM_HEADS, tm), lambda i, j: (i, 0, j))],
        compiler_params=_params("parallel", "parallel"),
        name="rec_proj",
    )(x, g.reshape(1, d), w_main, wgt, bg)


def _mlstm_prompt_kernel(q_ref, k_ref, v_ref, og_ref, gt_ref, gc_ref, ng_ref,
                         hm_ref, c_ref, n_ref, m_ref, ct_scr, n_scr, m_scr, *, chunk):
    t = pl.program_id(1)
    L = chunk
    half = LANES // 2

    @pl.when(t == 0)
    def _():
        ct_scr[...] = jnp.zeros_like(ct_scr)
        n_scr[...] = jnp.zeros_like(n_scr)
        m_scr[...] = jnp.full(m_scr.shape, NEG, F32)

    gt = gt_ref[0]
    gc = gc_ref[0]
    cum_rows = _dot3_left(gt, _tri(L, upper=True))
    cum_cols = _dot3_right(_tri(L, upper=False), gc)
    rr = lax.broadcasted_iota(jnp.int32, (L, L), 0)
    cc = lax.broadcasted_iota(jnp.int32, (L, L), 1)
    tri = cc <= rr
    lane = lax.broadcasted_iota(jnp.int32, (L, LANES), 1)

    heads = []
    for h in range(MLSTM_HEADS):
        pair, c = divmod(h, 2)
        qp = q_ref[0, :, pair * LANES:(pair + 1) * LANES]
        kp = k_ref[0, :, pair * LANES:(pair + 1) * LANES]
        mine = (lane < half) if c == 0 else (lane >= half)
        qm = jnp.where(mine, qp, 0.0)
        qm_b = qm.astype(BF16)
        qk = _dot_nt(qm_b, kp.astype(BF16))
        qc = _dot(qm_b, ct_scr[pair].astype(BF16))
        heads.append(dict(pair=pair, c=c, kp=kp, qm=qm, qk=qk, qc=qc))

    for h, hd_ in enumerate(heads):
        a_col = cum_cols[:, MLSTM_HEADS + h:MLSTM_HEADS + h + 1]
        ig_col = gc[:, h:h + 1]
        b_row = gt[h:h + 1, :] - cum_rows[MLSTM_HEADS + h:MLSTM_HEADS + h + 1, :]
        m_prev = m_scr[h:h + 1, 0:1]
        d = jnp.where(tri, a_col + b_row, NEG)
        m_inter = a_col + m_prev
        m_t = jnp.maximum(jnp.max(d, axis=1, keepdims=True), m_inter)
        s = hd_["qk"] * jnp.exp(d - m_t)
        w_inter = jnp.exp(m_inter - m_t)
        m_new = m_t[L - 1:L, :]
        cum_last = a_col[L - 1:L, :]
        decay = jnp.exp(cum_last + m_prev - m_new)
        w_key = jnp.exp(cum_last - a_col + ig_col - m_new)
        vh = v_ref[0, :, h * MLSTM_DV:(h + 1) * MLSTM_DV]
        hd_.update(s=s, w_inter=w_inter, m_t=m_t, m_new=m_new, decay=decay, w_key=w_key, vh=vh)

    for hd_ in heads:
        hd_["sv"] = _dot(hd_["s"].astype(BF16), hd_["vh"].astype(BF16))
        kt_b = hd_["kp"].T.astype(BF16)
        hd_["upd"] = _dot(kt_b, (hd_["w_key"] * hd_["vh"]).astype(BF16))

    lane1 = lax.broadcasted_iota(jnp.int32, (1, LANES), 1)
    for pair in range(MLSTM_HEADS // 2):
        ct = ct_scr[pair]
        n_row = n_scr[pair]
        ct_new = []
        n_new = []
        for c in range(2):
            h = 2 * pair + c
            hd_ = heads[h]
            num = hd_["sv"] + hd_["w_inter"] * hd_["qc"]
            den = (jnp.sum(hd_["s"], axis=1, keepdims=True)
                   + hd_["w_inter"] * jnp.sum(hd_["qm"] * n_row, axis=1, keepdims=True))
            hout = num / jnp.maximum(jnp.abs(den), jnp.exp(-hd_["m_t"]))
            gh = ng_ref[:, h * MLSTM_DV:(h + 1) * MLSTM_DV]
            oh = og_ref[0, :, h * MLSTM_DV:(h + 1) * MLSTM_DV]
            hm_ref[0, :, h * MLSTM_DV:(h + 1) * MLSTM_DV] = _rms(hout, gh) * jax.nn.sigmoid(oh)
            lo, hi_ = c * half, (c + 1) * half
            ct_new.append(hd_["decay"] * ct[lo:hi_] + hd_["upd"][lo:hi_])
            n_new.append(hd_["decay"] * n_row + jnp.sum(hd_["w_key"] * hd_["kp"], axis=0, keepdims=True))
            m_scr[h:h + 1, :] = jnp.broadcast_to(hd_["m_new"], (1, LANES))
        ct_scr[pair] = jnp.concatenate(ct_new, axis=0)
        n_scr[pair] = jnp.where(lane1 < half, n_new[0], n_new[1])

    @pl.when(t == pl.num_programs(1) - 1)
    def _():
        for pair in range(MLSTM_HEADS // 2):
            c_pair = ct_scr[pair].T
            n_row = n_scr[pair]
            for c in range(2):
                h = 2 * pair + c
                c_ref[0, h] = c_pair[:, c * half:(c + 1) * half]
                n_ref[0, h:h + 1, :] = n_row[:, c * half:(c + 1) * half]
                m_ref[0, :, h:h + 1] = m_scr[h:h + 1, 0:1]


def _mlstm_prompt(q, k, v, og, gt, gc, norm_g):
    b, t, _ = q.shape
    chunk = min(256, t)
    row = lambda i, j: (i, j, 0)
    const = lambda i, j: (0, 0)
    return pl.pallas_call(
        functools.partial(_mlstm_prompt_kernel, chunk=chunk),
        out_shape=[
            jax.ShapeDtypeStruct((b, t, MLSTM_V), F32),
            jax.ShapeDtypeStruct((b, MLSTM_HEADS, MLSTM_DV, MLSTM_DK), F32),
            jax.ShapeDtypeStruct((b, MLSTM_HEADS, MLSTM_DK), F32),
            jax.ShapeDtypeStruct((b, 1, MLSTM_HEADS), F32),
        ],
        grid=(b, t // chunk),
        in_specs=[
            pl.BlockSpec((1, chunk, MLSTM_QK), row),
            pl.BlockSpec((1, chunk, MLSTM_QK), row),
            pl.BlockSpec((1, chunk, MLSTM_V), row),
            pl.BlockSpec((1, chunk, MLSTM_V), row),
            pl.BlockSpec((1, 2 * MLSTM_HEADS, chunk), lambda i, j: (i, 0, j)),
            pl.BlockSpec((1, chunk, 2 * MLSTM_HEADS), row),
            pl.BlockSpec((1, MLSTM_V), const),
        ],
        out_specs=[
            pl.BlockSpec((1, chunk, MLSTM_V), row),
            pl.BlockSpec((1, MLSTM_HEADS, MLSTM_DV, MLSTM_DK), lambda i, j: (i, 0, 0, 0)),
            pl.BlockSpec((1, MLSTM_HEADS, MLSTM_DK), lambda i, j: (i, 0, 0)),
            pl.BlockSpec((1, 1, MLSTM_HEADS), lambda i, j: (i, 0, 0)),
        ],
        scratch_shapes=[
            pltpu.VMEM((MLSTM_HEADS // 2, LANES, MLSTM_DV), F32),
            pltpu.VMEM((MLSTM_HEADS // 2, 1, LANES), F32),
            pltpu.VMEM((SUBLANES, LANES), F32),
        ],
        compiler_params=_params("parallel", "arbitrary"),
        name="mlstm_prompt",
    )(q, k, v, og, gt, gc, norm_g.reshape(1, MLSTM_V))


def _mlstm_sample_kernel(q_ref, k_ref, v_ref, og_ref, gc_ref, m_ref, kall_ref, vall_ref, ng_ref, c_ref, n_ref,
                         hm_ref, co_ref, no_ref, mo_ref, kt_scr, *, bs):
    i = pl.program_id(0)
    nb = kall_ref.shape[0]
    half = LANES // 2

    @pl.when(i == 0)
    def _():
        for pair in range(MLSTM_HEADS // 2):
            kt_scr[pair] = kall_ref[:, pair * LANES:(pair + 1) * LANES].T

    rows = lax.broadcasted_iota(jnp.int32, (nb, MLSTM_DV), 0)

    def body(bl, _):
        b = i * bs + bl
        q_row = q_ref[bl]
        k_row = k_ref[bl]
        v_row = v_ref[bl]
        og_row = og_ref[bl]
        g_row = gc_ref[bl]
        m_row = m_ref[bl]
        cms, cqs, outers = [], [], []
        for h in range(MLSTM_HEADS):
            qh = q_row[:, h * MLSTM_DK:(h + 1) * MLSTM_DK]
            cm = c_ref[bl, h]
            v_all = vall_ref[:, h * MLSTM_DV:(h + 1) * MLSTM_DV]
            v_sel = jnp.where(rows == b, v_all, 0.0).astype(BF16)
            pair, c = divmod(h, 2)
            kt_h = kt_scr[pair, c * half:(c + 1) * half, :]
            cms.append(cm)
            cqs.append(_dot(qh.astype(BF16), cm.astype(BF16)))
            outers.append(_dot(kt_h.astype(BF16), v_sel))
        for h in range(MLSTM_HEADS):
            qh = q_row[:, h * MLSTM_DK:(h + 1) * MLSTM_DK]
            kh = k_row[:, h * MLSTM_DK:(h + 1) * MLSTM_DK]
            vh = v_row[:, h * MLSTM_DV:(h + 1) * MLSTM_DV]
            it = g_row[:, h:h + 1]
            lf = g_row[:, MLSTM_HEADS + h:MLSTM_HEADS + h + 1]
            m_prev = m_row[:, h:h + 1]
            m_inter = lf + m_prev
            m_t = jnp.maximum(it, m_inter)
            e_i = jnp.exp(it - m_t)
            w_inter = jnp.exp(m_inter - m_t)
            s = jnp.sum(qh * kh, axis=1, keepdims=True) * e_i
            nh = n_ref[bl, h:h + 1, :]
            num = s * vh + w_inter * cqs[h]
            den = s + w_inter * jnp.sum(nh * qh, axis=1, keepdims=True)
            hrow = num / jnp.maximum(jnp.abs(den), jnp.exp(-m_t))
            gh = ng_ref[:, h * MLSTM_DV:(h + 1) * MLSTM_DV]
            oh = og_row[:, h * MLSTM_DV:(h + 1) * MLSTM_DV]
            hm_ref[bl, :, h * MLSTM_DV:(h + 1) * MLSTM_DV] = _rms(hrow, gh) * jax.nn.sigmoid(oh)
            co_ref[bl, h] = w_inter * cms[h] + e_i * outers[h]
            no_ref[bl, h:h + 1, :] = w_inter * nh + e_i * kh
            mo_ref[bl, :, h:h + 1] = m_t
        return 0

    lax.fori_loop(0, bs, body, 0)


def _mlstm_sample(q, k, v, og, gc, norm_g, c0_t, n0, m0):
    nb = q.shape[0]
    bs = min(16, nb)
    full = lambda i: (0, 0)
    rows = lambda i: (i, 0, 0)
    per_sample = [a.reshape(nb, 1, a.shape[1]) for a in (q, k, v, og, gc, m0)]
    hm, c1, n1, m1 = pl.pallas_call(
        functools.partial(_mlstm_sample_kernel, bs=bs),
        out_shape=[
            jax.ShapeDtypeStruct((nb, 1, MLSTM_V), F32),
            jax.ShapeDtypeStruct(c0_t.shape, F32),
            jax.ShapeDtypeStruct(n0.shape, F32),
            jax.ShapeDtypeStruct((nb, 1, MLSTM_HEADS), F32),
        ],
        grid=(nb // bs,),
        in_specs=[pl.BlockSpec((bs, 1, a.shape[2]), rows) for a in per_sample] + [
            pl.BlockSpec(k.shape, full),
            pl.BlockSpec(v.shape, full),
            pl.BlockSpec((1, MLSTM_V), full),
            pl.BlockSpec((bs, MLSTM_HEADS, MLSTM_DK, MLSTM_DV), lambda i: (i, 0, 0, 0)),
            pl.BlockSpec((bs, MLSTM_HEADS, MLSTM_DK), rows),
        ],
        out_specs=[
            pl.BlockSpec((bs, 1, MLSTM_V), rows),
            pl.BlockSpec((bs, MLSTM_HEADS, MLSTM_DK, MLSTM_DV), lambda i: (i, 0, 0, 0)),
            pl.BlockSpec((bs, MLSTM_HEADS, MLSTM_DK), rows),
            pl.BlockSpec((bs, 1, MLSTM_HEADS), rows),
        ],
        scratch_shapes=[pltpu.VMEM((MLSTM_HEADS // 2, LANES, nb), F32)],
        compiler_params=_params("arbitrary"),
        name="mlstm_sample",
    )(*per_sample, k, v, norm_g.reshape(1, MLSTM_V), c0_t, n0)
    return hm.reshape(nb, MLSTM_V), c1, n1, m1.reshape(nb, MLSTM_HEADS)


_CONV_HALO = 32
_CONV_ROWS = 32


def _ln_silu(y, g, b):
    yc = y - jnp.mean(y, axis=-1, keepdims=True)
    var = jnp.mean(yc * yc, axis=-1, keepdims=True)
    z = yc * lax.rsqrt(var + NORM_EPS) * g + b
    return z * jax.nn.sigmoid(z)


def _conv_prompt_kernel(u_ref, prev_ref, w_ref, b_ref, g_ref, be_ref, c_ref, st_ref, full_scr, shift_scr, *, tt):
    t = pl.program_id(1)
    pad = _CONV_HALO - (CONV_W - 1)

    @pl.when(t == 0)
    def _():
        full_scr[0:_CONV_HALO, :] = jnp.zeros((_CONV_HALO, CONV_CH), F32)
        full_scr[pad:_CONV_HALO, :] = prev_ref[0]

    full_scr[_CONV_HALO:, :] = u_ref[0]
    span = tt + _CONV_HALO - SUBLANES
    for r in range(1, SUBLANES):
        shift_scr[r, 0:span, :] = full_scr[r:r + span, :]
    for r0 in range(0, tt, _CONV_ROWS):
        acc = jnp.broadcast_to(b_ref[...], (_CONV_ROWS, CONV_CH))
        for k in range(CONV_W):
            off = r0 + pad + k
            r = off % SUBLANES
            if r == 0:
                rows = full_scr[off:off + _CONV_ROWS, :]
            else:
                rows = shift_scr[r, off - r:off - r + _CONV_ROWS, :]
            acc = acc + w_ref[k:k + 1, :] * rows
        c_ref[0, r0:r0 + _CONV_ROWS, :] = _ln_silu(acc, g_ref[...], be_ref[...])

    @pl.when(t == pl.num_programs(1) - 1)
    def _():
        st_ref[0] = full_scr[_CONV_HALO + tt - (CONV_W - 1):, :]

    full_scr[0:_CONV_HALO, :] = full_scr[tt:tt + _CONV_HALO, :]


def _conv_prompt(u, prev, w, bias, ln_g, ln_b):
    b, t, ch = u.shape
    tt = min(256, t)
    const = lambda i, j: (0, 0)
    return pl.pallas_call(
        functools.partial(_conv_prompt_kernel, tt=tt),
        out_shape=[jax.ShapeDtypeStruct((b, t, ch), F32),
                   jax.ShapeDtypeStruct((b, CONV_W - 1, ch), F32)],
        grid=(b, t // tt),
        in_specs=[
            pl.BlockSpec((1, tt, ch), lambda i, j: (i, j, 0)),
            pl.BlockSpec((1, CONV_W - 1, ch), lambda i, j: (i, 0, 0)),
            pl.BlockSpec((CONV_W, ch), const),
            pl.BlockSpec((1, ch), const),
            pl.BlockSpec((1, ch), const),
            pl.BlockSpec((1, ch), const),
        ],
        out_specs=[pl.BlockSpec((1, tt, ch), lambda i, j: (i, j, 0)),
                   pl.BlockSpec((1, CONV_W - 1, ch), lambda i, j: (i, 0, 0))],
        scratch_shapes=[pltpu.VMEM((_CONV_HALO + tt, ch), F32),
                        pltpu.VMEM((SUBLANES, _CONV_HALO + tt, ch), F32)],
        compiler_params=_params("parallel", "arbitrary"),
        name="conv_prompt",
    )(u, prev, w, bias.reshape(1, ch), ln_g.reshape(1, ch), ln_b.reshape(1, ch))


def _conv_sample_kernel(u_ref, prev_ref, w_ref, b_ref, g_ref, be_ref, c_ref, st_ref):
    hist = CONV_W - 1
    u = u_ref[...]
    y = w_ref[hist:hist + 1, :] * u + b_ref[...]
    for k in range(hist):
        row = prev_ref[k]
        y = y + w_ref[k:k + 1, :] * row
        if k > 0:
            st_ref[k - 1] = row
    st_ref[hist - 1] = u
    c_ref[...] = _ln_silu(y, g_ref[...], be_ref[...])


def _conv_sample(u, prev_t, w, bias, ln_g, ln_b):
    nb, ch = u.shape
    bs = min(32, nb)
    hist = CONV_W - 1
    const = lambda i: (0, 0)
    return pl.pallas_call(
        _conv_sample_kernel,
        out_shape=[jax.ShapeDtypeStruct((nb, ch), F32),
                   jax.ShapeDtypeStruct((hist, nb, ch), F32)],
        grid=(nb // bs,),
        in_specs=[
            pl.BlockSpec((bs, ch), lambda i: (i, 0)),
            pl.BlockSpec((hist, bs, ch), lambda i: (0, i, 0)),
            pl.BlockSpec((CONV_W, ch), const),
            pl.BlockSpec((1, ch), const),
            pl.BlockSpec((1, ch), const),
            pl.BlockSpec((1, ch), const),
        ],
        out_specs=[pl.BlockSpec((bs, ch), lambda i: (i, 0)),
                   pl.BlockSpec((hist, bs, ch), lambda i: (0, i, 0))],
        compiler_params=_params("parallel"),
        name="conv_sample",
    )(u, prev_t, w, bias.reshape(1, ch), ln_g.reshape(1, ch), ln_b.reshape(1, ch))


def kernel(x_prompt, x_sample, cache_fox_kv, cache_fox_logf, cache_diff_kv, state_mlstm_C, state_mlstm_n,
           state_mlstm_m, state_conv, page_table, norm_g, final_g, ffn_w_in, ffn_w_out, attn_w_in, attn_b_f,
           diff_lam, diff_subln_g, attn_w_out, rec_w_in, rec_b_i, rec_b_f, mlstm_norm_g, conv_w, conv_b,
           conv_ln_g, conv_ln_b, rec_w_out):
    bp, t, d = x_prompt.shape
    nb = x_sample.shape[0]
    depth = norm_g.shape[0]
    n_pool, page = cache_fox_kv.shape[1], cache_fox_kv.shape[2]
    xp = x_prompt.reshape(bp * t, d)
    xs = x_sample.reshape(nb, d)
    w_in_b = ffn_w_in.astype(BF16)
    w_out_b = ffn_w_out.astype(BF16)

    outs = {k: [] for k in ("fkv_p", "fkv_s", "flf_p", "flf_s", "dkv_p", "dkv_s",
                            "c_p", "c_s", "n_p", "n_s", "m_p", "m_s", "cv_p", "cv_s")}
    for l in range(depth):
        j = l // 2
        xp = _ffn(xp, norm_g[l, 0], w_in_b, w_out_b, l, 0)
        xs = _ffn(xs, norm_g[l, 0], w_in_b, w_out_b, l, 0)
        if l % 2 == 0:
            lam_init = 0.8 - 0.6 * math.exp(-0.3 * l)
            w = attn_w_in[j]
            c1, c2 = 3 * FOX_W, 3 * FOX_W + FOX_HEADS
            w_main = jnp.concatenate([w[:, :c1], w[:, c2:]], axis=1).astype(BF16)
            wft = w[:, c1:c2].T.astype(BF16)
            bf = attn_b_f[j].reshape(FOX_HEADS, 1)
            w_o = attn_w_out[j].astype(BF16)
            qf, kvf, qd, kvd, lft, cumt, kvd_rows = _attn_proj(xp.reshape(bp, t, d), norm_g[l, 1], w_main, wft, bf)
            o_f = _attn_prompt(True, qf, kvf, jnp.swapaxes(cumt, 1, 2), cumt, lam_init)
            o_d = _attn_prompt(False, qd, kvd, diff_lam[j], diff_subln_g[j].reshape(1, -1), lam_init)
            xp = _merge(xp, o_f.reshape(bp * t, FOX_W), o_d.reshape(bp * t, DIFF_W), w_o)
            outs["fkv_p"].append(kvf.reshape(bp, t, 2, FOX_HEADS, FOX_DH))
            outs["flf_p"].append(jnp.swapaxes(lft, 1, 2))
            outs["dkv_p"].append(kvd_rows.reshape(bp, t, 2, DIFF_HEADS, 2 * DIFF_DH))
            sqf, skvf, sqd, skvd, slft, _, skvd_rows = _attn_proj(xs.reshape(1, nb, d), norm_g[l, 1], w_main, wft, bf)
            so_f, so_d = _decode(
                page_table,
                sqf.reshape(nb, 1, FOX_W), sqd.reshape(nb, 1, DIFF_W),
                skvf.reshape(nb, 1, 2 * FOX_W), skvd.reshape(nb, 1, 2 * DIFF_W),
                slft[0], diff_lam[j], diff_subln_g[j].reshape(1, -1),
                jnp.transpose(cache_fox_kv[j], (0, 2, 3, 4, 1)).reshape(n_pool, 2, FOX_W, page),
                jnp.swapaxes(cache_fox_logf[j], 1, 2),
                cache_diff_kv[j].reshape(n_pool, page * 2 * DIFF_HEADS, 2 * DIFF_DH),
                lam_init)
            xs = _merge(xs, so_f.reshape(nb, FOX_W), so_d.reshape(nb, DIFF_W), w_o)
            outs["fkv_s"].append(skvf.reshape(nb, 1, 2, FOX_HEADS, FOX_DH))
            outs["flf_s"].append(jnp.swapaxes(slft, 1, 2).reshape(nb, 1, FOX_HEADS))
            outs["dkv_s"].append(skvd_rows.reshape(nb, 1, 2, DIFF_HEADS, 2 * DIFF_DH))
        else:
            w = rec_w_in[j]
            c1 = 2 * MLSTM_QK + MLSTM_V
            c2 = c1 + 2 * MLSTM_HEADS
            w_main = jnp.concatenate([w[:, :c1], w[:, c2:]], axis=1).astype(BF16)
            wgt = w[:, c1:c2].T.astype(BF16)
            bg = jnp.concatenate([rec_b_i[j], rec_b_f[j]]).reshape(2 * MLSTM_HEADS, 1)
            w_o = rec_w_out[j].astype(BF16)
            q, k, v, og, u, gt = _rec_proj(xp.reshape(bp, t, d), norm_g[l, 1], w_main, wgt, bg)
            hm, c_p, n_p, m_p = _mlstm_prompt(q, k, v, og, gt, jnp.swapaxes(gt, 1, 2), mlstm_norm_g[j])
            cv, st_p = _conv_prompt(u, jnp.zeros((bp, CONV_W - 1, CONV_CH), F32), conv_w[j], conv_b[j],
                                    conv_ln_g[j], conv_ln_b[j])
            xp = _merge(xp, hm.reshape(bp * t, MLSTM_V), cv.reshape(bp * t, CONV_CH), w_o)
            outs["c_p"].append(c_p)
            outs["n_p"].append(n_p)
            outs["m_p"].append(m_p.reshape(bp, MLSTM_HEADS))
            outs["cv_p"].append(st_p)
            sq, sk, sv, sog, su, sgt = _rec_proj(xs.reshape(1, nb, d), norm_g[l, 1], w_main, wgt, bg)
            shm, c_s_t, n_s, m_s = _mlstm_sample(sq[0], sk[0], sv[0], sog[0], sgt[0].T, mlstm_norm_g[j],
                                                 jnp.swapaxes(state_mlstm_C[j], -1, -2),
                                                 state_mlstm_n[j], state_mlstm_m[j])
            scv, st_s = _conv_sample(su[0], jnp.swapaxes(state_conv[j], 0, 1), conv_w[j], conv_b[j],
                                     conv_ln_g[j], conv_ln_b[j])
            xs = _merge(xs, shm, scv, w_o)
            outs["c_s"].append(jnp.swapaxes(c_s_t, -1, -2))
            outs["n_s"].append(n_s)
            outs["m_s"].append(m_s)
            outs["cv_s"].append(jnp.swapaxes(st_s, 0, 1))
        last = l == depth - 1
        xp = _ffn(xp, norm_g[l, 2], w_in_b, w_out_b, l, 1, final_g if last else None)
        xs = _ffn(xs, norm_g[l, 2], w_in_b, w_out_b, l, 1, final_g if last else None)

    st = jnp.stack
    return (xp.reshape(bp, t, d), xs.reshape(nb, 1, d),
            st(outs["fkv_p"]), st(outs["fkv_s"]), st(outs["flf_p"]), st(outs["flf_s"]),
            st(outs["dkv_p"]), st(outs["dkv_s"]),
            st(outs["c_p"]), st(outs["c_s"]), st(outs["n_p"]), st(outs["n_s"]),
            st(outs["m_p"]), st(outs["m_s"]), st(outs["cv_p"]), st(outs["cv_s"]))
```
